```python
import math
import jax, jax.numpy as jnp
from jax import lax
import numpy as np

D_MODEL = 1024
BATCH = 4
SEQ = 8192
DEPTH = 1
DEC_BATCH = 32
DEC_SEQ = 1
PAST_LEN = 16384
PAGE_SIZE = 128

HEAD_DIM = 64
N_HEADS_A = (D_MODEL // 2) // HEAD_DIM
D_A = N_HEADS_A * HEAD_DIM
MOBA_BLOCK = 256
MOBA_TOPK = 3
Q_CHUNK = 32
D_POOL = D_MODEL - D_A
POOL_WINDOWS = (2, 4, 8, 16)
N_POOL_GROUPS = len(POOL_WINDOWS)
POOL_GROUP = D_POOL // N_POOL_GROUPS
POOL_STATE = max(POOL_WINDOWS) - 1
N_BUCKETS = 32
MAX_DISTANCE = 128
N_MEM = 256
N_HEADS_X = 4
HEAD_DIM_X = D_MODEL // N_HEADS_X
D_FF = 2816
N_SUB = 4
LN_EPS = 1e-5
ALPHA = (2 * DEPTH) ** 0.25
BETA = (8 * DEPTH) ** -0.25
NEG = -1e30

kernel_name = "hymba_moba_pool_macaron_deepnorm_step"


def layer_norm(x, g, b):
    xf = x.astype(jnp.float32)
    mu = jnp.mean(xf, axis=-1, keepdims=True)
    var = jnp.mean(jnp.square(xf - mu), axis=-1, keepdims=True)
    y = (xf - mu) * lax.rsqrt(var + LN_EPS)
    return (y * g.astype(jnp.float32) + b.astype(jnp.float32)).astype(x.dtype)


def post_norm(x, sub, g, b):
    return layer_norm(ALPHA * x + sub.astype(x.dtype), g, b)


def swiglu(x, w_gate, w_up, w_down):
    return (jax.nn.silu(x @ w_gate) * (x @ w_up)) @ w_down


def t5_bucket(dist):
    n = jnp.maximum(dist, 0)
    max_exact = N_BUCKETS // 2
    nf = jnp.maximum(n, 1).astype(jnp.float32)
    large = max_exact + (jnp.log(nf / max_exact) / math.log(MAX_DISTANCE / max_exact)
                         * (N_BUCKETS - max_exact)).astype(jnp.int32)
    large = jnp.minimum(large, N_BUCKETS - 1)
    return jnp.where(n < max_exact, n, large)


def to_blocks(parts):
    L = sum(p.shape[1] for p in parts)
    nb = -(-L // MOBA_BLOCK)
    pad = nb * MOBA_BLOCK - L
    B, _, H, D = parts[0].shape
    if pad:
        parts = parts + [jnp.zeros((B, pad, H, D), parts[0].dtype)]
    full = parts[0] if len(parts) == 1 else jnp.concatenate(parts, axis=1)
    return full.reshape(B, nb, MOBA_BLOCK, H, D)


def moba_attend(q, q_pos, kb, vb, kmean, rel_bias):
    B, T, H, D = q.shape
    nb = kb.shape[1]
    qf = q.astype(jnp.float32)
    own = q_pos // MOBA_BLOCK
    gate = jnp.einsum('bthd,bnhd->bhtn', qf, kmean)
    fully_past = jnp.arange(nb, dtype=jnp.int32)[None, :] < own[:, None]
    gate = jnp.where(fully_past, gate, NEG)
    _, top = lax.top_k(gate, min(MOBA_TOPK, nb))
    top_ok = top < own[:, None]
    idx = jnp.concatenate([top, jnp.broadcast_to(own[:, None], (B, H, T, 1))], axis=-1)
    blk_ok = jnp.concatenate([top_ok, jnp.ones((B, H, T, 1), bool)], axis=-1)
    b_ix = jnp.arange(B)[:, None, None, None]
    h_ix = jnp.arange(H)[None, :, None, None]
    ks = kb[b_ix, idx, :, h_ix, :]
    vs = vb[b_ix, idx, :, h_ix, :]
    k_pos = idx[..., None] * MOBA_BLOCK + jnp.arange(MOBA_BLOCK, dtype=jnp.int32)
    dist = q_pos[:, None, None] - k_pos
    bias = rel_bias[t5_bucket(dist), h_ix[..., None]].astype(jnp.float32)
    logits = jnp.einsum('bthd,bhtnkd->bhtnk', qf, ks) * (D ** -0.5) + bias
    logits = jnp.where(blk_ok[..., None] & (dist >= 0), logits, NEG)
    p = jax.nn.softmax(logits.reshape(B, H, T, -1), axis=-1).reshape(logits.shape)
    o = jnp.einsum('bhtnk,bhtnkd->bthd', p, vs.astype(jnp.float32))
    return o.astype(q.dtype)


def pool_mixer(u_ext, pos_new, pool_w, pool_scale):
    B, R, _ = u_ext.shape
    T = R - POOL_STATE
    uf = u_ext.astype(jnp.float32)
    cs = jnp.pad(jnp.cumsum(uf, axis=1), ((0, 0), (1, 0), (0, 0)))
    end = cs[:, POOL_STATE + 1:]
    u_new = uf[:, POOL_STATE:]
    diffs = []
    for g, w in enumerate(POOL_WINDOWS):
        sl = slice(g * POOL_GROUP, (g + 1) * POOL_GROUP)
        start = cs[:, POOL_STATE + 1 - w: POOL_STATE + 1 - w + T, sl]
        cnt = jnp.minimum(w, pos_new + 1).astype(jnp.float32)[None, :, None]
        diffs.append((end[..., sl] - start) / cnt - u_new[..., sl])
    d = jnp.stack(diffs, axis=2)
    y = jnp.einsum('btgc,gce->btge', d, pool_w.astype(jnp.float32))
    y = y * pool_scale.astype(jnp.float32).reshape(N_POOL_GROUPS, POOL_GROUP)
    return y.reshape(B, T, D_POOL).astype(u_ext.dtype)


def mem_attend(x, mk, mv, w_xq, w_xo):
    B, T, _ = x.shape
    q = (x @ w_xq).reshape(B, T, N_HEADS_X, HEAD_DIM_X)
    logits = jnp.einsum('bthd,bmhd->bhtm', q.astype(jnp.float32), mk.astype(jnp.float32)) * (HEAD_DIM_X ** -0.5)
    p = jax.nn.softmax(logits, axis=-1)
    o = jnp.einsum('bhtm,bmhd->bthd', p, mv.astype(jnp.float32)).reshape(B, T, D_MODEL)
    return o.astype(x.dtype) @ w_xo


def split_heads(h):
    B, T, _ = h.shape
    q = h[..., :D_A].reshape(B, T, N_HEADS_A, HEAD_DIM)
    k = h[..., D_A:2 * D_A].reshape(B, T, N_HEADS_A, HEAD_DIM)
    v = h[..., 2 * D_A:3 * D_A].reshape(B, T, N_HEADS_A, HEAD_DIM)
    u = h[..., 3 * D_A:]
    return q, k, v, u


def setup_inputs(seed: int = 0) -> dict:
    key = jax.random.key(seed)
    ks = jax.random.split(key, 32)
    f = jnp.float32
    n_pages = PAST_LEN // PAGE_SIZE
    n_used = DEC_BATCH * n_pages
    n_pool = n_used + n_used // 4

    def nrm(k, shape, scale):
        return jax.random.normal(k, shape, f) * scale

    perm = jax.random.permutation(ks[7], n_pool)
    return {
        "x_prompt": nrm(ks[0], (BATCH, SEQ, D_MODEL), 1.0),
        "x_sample": nrm(ks[1], (DEC_BATCH, DEC_SEQ, D_MODEL), 1.0),
        "cache_k": nrm(ks[2], (DEPTH, n_pool, PAGE_SIZE, N_HEADS_A, HEAD_DIM), 1.0),
        "cache_v": nrm(ks[3], (DEPTH, n_pool, PAGE_SIZE, N_HEADS_A, HEAD_DIM), 1.0),
        "cache_mem_k": nrm(ks[4], (DEPTH, DEC_BATCH, N_MEM, N_HEADS_X, HEAD_DIM_X), 1.0),
        "cache_mem_v": nrm(ks[5], (DEPTH, DEC_BATCH, N_MEM, N_HEADS_X, HEAD_DIM_X), 1.0),
        "state_pool": nrm(ks[6], (DEPTH, DEC_BATCH, POOL_STATE, D_POOL), 1.0),
        "page_table": perm[:n_used].reshape(DEC_BATCH, n_pages).astype(jnp.int32),
        "mem_prompt": nrm(ks[8], (BATCH, N_MEM, D_MODEL), 1.0),
        "rel_bias": nrm(ks[9], (N_BUCKETS, N_HEADS_A), 0.2),
        "ln_g": 1.0 + nrm(ks[10], (DEPTH, N_SUB, D_MODEL), 0.02),
        "ln_b": nrm(ks[11], (DEPTH, N_SUB, D_MODEL), 0.02),
        "w_ff1_gate": nrm(ks[12], (DEPTH, D_MODEL, D_FF), D_MODEL ** -0.5),
        "w_ff1_up": nrm(ks[13], (DEPTH, D_MODEL, D_FF), D_MODEL ** -0.5),
        "w_ff1_down": nrm(ks[14], (DEPTH, D_FF, D_MODEL), BETA * D_FF ** -0.5),
        "w_in": nrm(ks[15], (DEPTH, D_MODEL, 3 * D_A + D_POOL), D_MODEL ** -0.5),
        "pool_w": nrm(ks[16], (DEPTH, N_POOL_GROUPS, POOL_GROUP, POOL_GROUP), POOL_GROUP ** -0.5),
        "pool_scale": 1.0 + nrm(ks[17], (DEPTH, D_POOL), 0.02),
        "w_out": nrm(ks[18], (DEPTH, D_A + D_POOL, D_MODEL), BETA * (D_A + D_POOL) ** -0.5),
        "w_xq": nrm(ks[19], (DEPTH, D_MODEL, D_MODEL), D_MODEL ** -0.5),
        "w_xk": nrm(ks[20], (DEPTH, D_MODEL, D_MODEL), D_MODEL ** -0.5),
        "w_xv": nrm(ks[21], (DEPTH, D_MODEL, D_MODEL), D_MODEL ** -0.5),
        "w_xo": nrm(ks[22], (DEPTH, D_MODEL, D_MODEL), BETA * D_MODEL ** -0.5),
        "w_ff2_gate": nrm(ks[23], (DEPTH, D_MODEL, D_FF), D_MODEL ** -0.5),
        "w_ff2_up": nrm(ks[24], (DEPTH, D_MODEL, D_FF), D_MODEL ** -0.5),
        "w_ff2_down": nrm(ks[25], (DEPTH, D_FF, D_MODEL), BETA * D_FF ** -0.5),
    }


def reference(x_prompt, x_sample, cache_k, cache_v, cache_mem_k, cache_mem_v, state_pool, page_table,
              mem_prompt, rel_bias, ln_g, ln_b, w_ff1_gate, w_ff1_up, w_ff1_down, w_in, pool_w,
              pool_scale, w_out, w_xq, w_xk, w_xv, w_xo, w_ff2_gate, w_ff2_up, w_ff2_down):
    B, S, _ = x_prompt.shape
    DB, TS, _ = x_sample.shape
    past_len = page_table.shape[1] * PAGE_SIZE
    pos_p = jnp.arange(S, dtype=jnp.int32)
    pos_s = past_len + jnp.arange(TS, dtype=jnp.int32)
    n_chunks = S // Q_CHUNK

    xp, xs = x_prompt, x_sample
    kp_l, vp_l, pp_l, mkp_l, mvp_l, ks_l, vs_l, ps_l = [], [], [], [], [], [], [], []
    for l in range(DEPTH):
        g, bb = ln_g[l], ln_b[l]
        xp = post_norm(xp, 0.5 * swiglu(xp, w_ff1_gate[l], w_ff1_up[l], w_ff1_down[l]), g[0], bb[0])
        xs = post_norm(xs, 0.5 * swiglu(xs, w_ff1_gate[l], w_ff1_up[l], w_ff1_down[l]), g[0], bb[0])
        qp, kp, vp, up = split_heads(xp @ w_in[l])
        qs, kn, vn, un = split_heads(xs @ w_in[l])

        kb = to_blocks([kp])
        vb = to_blocks([vp])
        kmean = jnp.mean(kb.astype(jnp.float32), axis=2)
        q_chunks = qp.reshape(B, n_chunks, Q_CHUNK, N_HEADS_A, HEAD_DIM).transpose(1, 0, 2, 3, 4)
        a_chunks = lax.map(lambda c: moba_attend(c[0], c[1], kb, vb, kmean, rel_bias),
                           (q_chunks, pos_p.reshape(n_chunks, Q_CHUNK)))
        a_p = a_chunks.transpose(1, 0, 2, 3, 4).reshape(B, S, D_A)
        u_ext_p = jnp.concatenate([jnp.zeros((B, POOL_STATE, D_POOL), up.dtype), up], axis=1)
        p_p = pool_mixer(u_ext_p, pos_p, pool_w[l], pool_scale[l])

        k_past = cache_k[l][page_table].reshape(DB, past_len, N_HEADS_A, HEAD_DIM)
        v_past = cache_v[l][page_table].reshape(DB, past_len, N_HEADS_A, HEAD_DIM)
        kb_s = to_blocks([k_past.astype(kn.dtype), kn])
        vb_s = to_blocks([v_past.astype(vn.dtype), vn])
        kmean_s = jnp.mean(kb_s.astype(jnp.float32), axis=2)
        a_s = moba_attend(qs, pos_s, kb_s, vb_s, kmean_s, rel_bias).reshape(DB, TS, D_A)
        u_ext_s = jnp.concatenate([state_pool[l].astype(un.dtype), un], axis=1)
        p_s = pool_mixer(u_ext_s, pos_s, pool_w[l], pool_scale[l])

        xp = post_norm(xp, jnp.concatenate([a_p, p_p], axis=-1) @ w_out[l], g[1], bb[1])
        xs = post_norm(xs, jnp.concatenate([a_s, p_s], axis=-1) @ w_out[l], g[1], bb[1])

        mk_p = (mem_prompt @ w_xk[l]).reshape(B, N_MEM, N_HEADS_X, HEAD_DIM_X)
        mv_p = (mem_prompt @ w_xv[l]).reshape(B, N_MEM, N_HEADS_X, HEAD_DIM_X)
        xp = post_norm(xp, mem_attend(xp, mk_p, mv_p, w_xq[l], w_xo[l]), g[2], bb[2])
        xs = post_norm(xs, mem_attend(xs, cache_mem_k[l], cache_mem_v[l], w_xq[l], w_xo[l]), g[2], bb[2])

        xp = post_norm(xp, 0.5 * swiglu(xp, w_ff2_gate[l], w_ff2_up[l], w_ff2_down[l]), g[3], bb[3])
        xs = post_norm(xs, 0.5 * swiglu(xs, w_ff2_gate[l], w_ff2_up[l], w_ff2_down[l]), g[3], bb[3])

        kp_l.append(kp)
        vp_l.append(vp)
        pp_l.append(up[:, S - POOL_STATE:])
        mkp_l.append(mk_p)
        mvp_l.append(mv_p)
        ks_l.append(kn)
        vs_l.append(vn)
        ps_l.append(u_ext_s[:, TS:])

    k_prompt = jnp.stack(kp_l)
    v_prompt = jnp.stack(vp_l)
    pool_prompt = jnp.stack(pp_l)
    memk_prompt = jnp.stack(mkp_l)
    memv_prompt = jnp.stack(mvp_l)
    k_sample = jnp.stack(ks_l)
    v_sample = jnp.stack(vs_l)
    pool_sample = jnp.stack(ps_l)
    return (xp, xs, k_prompt, v_prompt, pool_prompt, memk_prompt, memv_prompt, k_sample, v_sample, pool_sample)
```

```python
import functools
import math

import jax
import jax.numpy as jnp
import numpy as np
from jax import lax
from jax.experimental import pallas as pl
from jax.experimental.pallas import tpu as pltpu

F32 = jnp.float32
BF16 = jnp.bfloat16

D_MODEL = 1024
HEAD_DIM = 64
N_HEADS_A = 8
D_A = N_HEADS_A * HEAD_DIM
MOBA_BLOCK = 256
MOBA_TOPK = 3
D_POOL = D_MODEL - D_A
POOL_WINDOWS = (2, 4, 8, 16)
POOL_GROUP = D_POOL // len(POOL_WINDOWS)
POOL_STATE = max(POOL_WINDOWS) - 1
N_BUCKETS = 32
MAX_DISTANCE = 128
N_HEADS_X = 4
HEAD_DIM_X = D_MODEL // N_HEADS_X
D_FF = 2816
LN_EPS = 1e-5
PAGE_SIZE = 128
NEG = -1e30

LANES = 128
SUBLANES = 8
VMEM_LIMIT_BYTES = 56 * 1024 * 1024

HEADS_PER_TILE = LANES // HEAD_DIM
POOL_HALO = POOL_STATE + 1
PAGES_PER_BLOCK = MOBA_BLOCK // PAGE_SIZE
FFN_CHUNK = 256
KMEAN_RING = 8

_NT = (((1,), (1,)), ((), ()))


def _params(n_grid_dims):
    return pltpu.CompilerParams(
        dimension_semantics=("arbitrary",) * n_grid_dims, vmem_limit_bytes=VMEM_LIMIT_BYTES)


def _resident(shape):
    return pl.BlockSpec(shape, lambda *_: (0,) * len(shape), pipeline_mode=pl.Buffered(1))


def _layer_norm(z, g, b):
    mu = jnp.mean(z, axis=-1, keepdims=True)
    zc = z - mu
    var = jnp.mean(zc * zc, axis=-1, keepdims=True)
    return zc * lax.rsqrt(var + LN_EPS) * g + b


def _swiglu(x, wg_ref, wu_ref, wd_ref):
    xb = x.astype(BF16)
    acc = None
    for c in range(D_FF // FFN_CHUNK):
        sl = slice(c * FFN_CHUNK, (c + 1) * FFN_CHUNK)
        g = jnp.dot(xb, wg_ref[:, sl], preferred_element_type=F32)
        u = jnp.dot(xb, wu_ref[:, sl], preferred_element_type=F32)
        h = (g * jax.nn.sigmoid(g) * u).astype(BF16)
        part = jnp.dot(h, wd_ref[sl, :], preferred_element_type=F32)
        acc = part if acc is None else acc + part
    return acc


def _ffn_kernel(x_ref, wg_ref, wu_ref, wd_ref, g_ref, b_ref, y_ref, *, alpha):
    x = x_ref[...]
    y_ref[...] = _layer_norm(alpha * x + 0.5 * _swiglu(x, wg_ref, wu_ref, wd_ref), g_ref[...], b_ref[...])


def _ffn_proj_kernel(x_ref, wg_ref, wu_ref, wd_ref, g_ref, b_ref, wqu_ref, wkvt_ref,
                     x1_ref, q_ref, kt_ref, vt_ref, u_ref, *, alpha):
    x = x_ref[...]
    x1 = _layer_norm(alpha * x + 0.5 * _swiglu(x, wg_ref, wu_ref, wd_ref), g_ref[...], b_ref[...])
    x1_ref[...] = x1
    xb = x1.astype(BF16)
    qu = jnp.dot(xb, wqu_ref[...], preferred_element_type=F32)
    q_ref[...] = (qu[:, :D_A] * HEAD_DIM ** -0.5).astype(BF16)
    u_ref[...] = qu[:, D_A:]
    kvt = lax.dot_general(wkvt_ref[...], xb, _NT, preferred_element_type=F32)
    kt_ref[0] = kvt[:D_A]
    vt_ref[0] = kvt[D_A:]


def _ffn_call(x, wg, wu, wd, g, b, *, alpha, tm):
    n = x.shape[0]
    row = pl.BlockSpec((tm, D_MODEL), lambda i: (i, 0))
    return pl.pallas_call(
        functools.partial(_ffn_kernel, alpha=alpha),
        grid=(n // tm,),
        in_specs=[row, _resident(wg.shape), _resident(wu.shape), _resident(wd.shape),
                  _resident(g.shape), _resident(b.shape)],
        out_specs=row,
        out_shape=jax.ShapeDtypeStruct((n, D_MODEL), F32),
        compiler_params=_params(1),
        name="ffn",
    )(x, wg, wu, wd, g, b)


def _ffn_proj_call(x, wg, wu, wd, g, b, wqu, wkvt, *, alpha, n_batch, tm):
    n = x.shape[0]
    seq = n // n_batch
    tpb = seq // tm
    row = lambda w: pl.BlockSpec((tm, w), lambda i: (i, 0))
    tcol = pl.BlockSpec((1, D_A, tm), lambda i: (i // tpb, 0, i % tpb))
    return pl.pallas_call(
        functools.partial(_ffn_proj_kernel, alpha=alpha),
        grid=(n // tm,),
        in_specs=[row(D_MODEL), _resident(wg.shape), _resident(wu.shape), _resident(wd.shape),
                  _resident(g.shape), _resident(b.shape), _resident(wqu.shape), _resident(wkvt.shape)],
        out_specs=[row(D_MODEL), row(D_A), tcol, tcol, row(D_POOL)],
        out_shape=[jax.ShapeDtypeStruct((n, D_MODEL), F32),
                   jax.ShapeDtypeStruct((n, D_A), BF16),
                   jax.ShapeDtypeStruct((n_batch, D_A, seq), F32),
                   jax.ShapeDtypeStruct((n_batch, D_A, seq), F32),
                   jax.ShapeDtypeStruct((n, D_POOL), F32)],
        compiler_params=_params(1),
        name="ffn_proj",
    )(x, wg, wu, wd, g, b, wqu, wkvt)


def _memkv_kernel(m_ref, wk_ref, wv_ref, k_ref, v_ref, kb_ref, vb_ref):
    mb = m_ref[...].astype(BF16)
    k = jnp.dot(mb, wk_ref[...], preferred_element_type=F32)
    v = jnp.dot(mb, wv_ref[...], preferred_element_type=F32)
    k_ref[...] = k
    v_ref[...] = v
    kb_ref[...] = k.astype(BF16)
    vb_ref[...] = v.astype(BF16)


def _memkv_call(mem, wk, wv, *, tm):
    n = mem.shape[0]
    row = pl.BlockSpec((tm, D_MODEL), lambda i: (i, 0))
    sds = lambda dt: jax.ShapeDtypeStruct((n, D_MODEL), dt)
    return pl.pallas_call(
        _memkv_kernel,
        grid=(n // tm,),
        in_specs=[row, _resident(wk.shape), _resident(wv.shape)],
        out_specs=[row, row, row, row],
        out_shape=[sds(F32), sds(F32), sds(BF16), sds(BF16)],
        compiler_params=_params(1),
        name="memkv",
    )(mem, wk, wv)


def _top3_rows(gate, n_valid, n_rows):
    rid = lax.broadcasted_iota(jnp.int32, gate.shape, 0).astype(F32)
    g = jnp.where(rid < n_valid, gate, NEG)
    sel = jnp.zeros(gate.shape, jnp.bool_)
    for _ in range(MOBA_TOPK):
        m = jnp.max(g, axis=0, keepdims=True)
        idx = jnp.min(jnp.where(g == m, rid, float(n_rows)), axis=0, keepdims=True)
        pick = rid == idx
        sel = jnp.logical_or(sel, jnp.logical_and(pick, idx < n_valid))
        g = jnp.where(pick, -jnp.inf, g)
    return sel


def _moba_kernel(bfar_ref, q_ref, kt_ref, vt_ref, town_ref, tprev_ref, o_ref,
                 kb_ref, vb_ref, km_ref, selb_ref, *, n_blk):
    pair = pl.program_id(1)
    i = pl.program_id(2)

    @pl.when(i == 0)
    def _prepare():
        for j in range(n_blk):
            cols = slice(j * MOBA_BLOCK, (j + 1) * MOBA_BLOCK)
            k = kt_ref[0, :, cols].T
            kb_ref[j * MOBA_BLOCK:(j + 1) * MOBA_BLOCK, :] = k.astype(BF16)
            km_ref[j:j + 1, :] = jnp.mean(k, axis=0, keepdims=True)
            vb_ref[j] = vt_ref[0, :, cols].astype(BF16)

    q2 = q_ref[0]
    lane_head = lax.broadcasted_iota(jnp.int32, q2.shape, 1) // HEAD_DIM
    kmb = km_ref[...].astype(BF16)
    i_f = i.astype(F32)

    qs = []
    for h in range(HEADS_PER_TILE):
        qh = jnp.where(lane_head == h, q2, jnp.zeros_like(q2))
        qs.append(qh)
        gate = lax.dot_general(kmb, qh, _NT, preferred_element_type=F32)
        sel = _top3_rows(gate, i_f, n_blk)
        selb_ref[h] = jnp.where(sel, bfar_ref[pair * HEADS_PER_TILE + h], NEG)

    def attend(h, kblk, vblk, bias, state):
        m, l, acc = state
        s = lax.dot_general(kblk, qs[h], _NT, preferred_element_type=F32) + bias
        m_new = jnp.maximum(m, jnp.max(s, axis=0, keepdims=True))
        a = jnp.exp(m - m_new)
        p = jnp.exp(s - m_new)
        l = a * l + jnp.sum(p, axis=0, keepdims=True)
        acc = a * acc + jnp.dot(vblk, p.astype(BF16), preferred_element_type=F32)
        return m_new, l, acc

    def vhead(vblk2, h):
        return vblk2[h * HEAD_DIM:(h + 1) * HEAD_DIM, :]

    init = (jnp.full((1, MOBA_BLOCK), NEG, F32), jnp.zeros((1, MOBA_BLOCK), F32),
            jnp.zeros((HEAD_DIM, MOBA_BLOCK), F32))
    k_own = kb_ref[pl.ds(pl.multiple_of(i * MOBA_BLOCK, MOBA_BLOCK), MOBA_BLOCK), :]
    v_own = vb_ref[i]
    states = [attend(h, k_own, vhead(v_own, h), town_ref[h], init) for h in range(HEADS_PER_TILE)]
    jp = jnp.maximum(i - 1, 0)
    k_prev = kb_ref[pl.ds(pl.multiple_of(jp * MOBA_BLOCK, MOBA_BLOCK), MOBA_BLOCK), :]
    v_prev = vb_ref[jp]
    for h in range(HEADS_PER_TILE):
        picked = selb_ref[h, pl.ds(jp, 1), :] > 0.5 * NEG
        bias = tprev_ref[h] + jnp.where(picked, 0.0, NEG)
        states[h] = attend(h, k_prev, vhead(v_prev, h), bias, states[h])

    def far_block(j, carry):
        kblk = kb_ref[pl.ds(pl.multiple_of(j * MOBA_BLOCK, MOBA_BLOCK), MOBA_BLOCK), :]
        vblk2 = vb_ref[j]
        return tuple(attend(h, kblk, vhead(vblk2, h), selb_ref[h, pl.ds(j, 1), :], carry[h])
                     for h in range(HEADS_PER_TILE))

    states = lax.fori_loop(0, i - 1, far_block, tuple(states))
    out_t = jnp.concatenate([acc / l for (_, l, acc) in states], axis=0)
    o_ref[0] = out_t.T.astype(BF16)


def _moba_call(q, kt, vt, town, tprev, bfar):
    n_batch, seq, _ = q.shape
    n_blk = seq // MOBA_BLOCK
    n_pair = N_HEADS_A // HEADS_PER_TILE
    qo = pl.BlockSpec((1, MOBA_BLOCK, LANES), lambda b, p, i, *_: (b, i, p))
    kv = pl.BlockSpec((1, LANES, seq), lambda b, p, i, *_: (b, p, 0))
    tbl = pl.BlockSpec((HEADS_PER_TILE, MOBA_BLOCK, MOBA_BLOCK), lambda b, p, i, *_: (p, 0, 0))
    return pl.pallas_call(
        functools.partial(_moba_kernel, n_blk=n_blk),
        grid_spec=pltpu.PrefetchScalarGridSpec(
            num_scalar_prefetch=1,
            grid=(n_batch, n_pair, n_blk),
            in_specs=[qo, kv, kv, tbl, tbl],
            out_specs=qo,
            scratch_shapes=[pltpu.VMEM((seq, LANES), BF16),
                            pltpu.VMEM((n_blk, LANES, MOBA_BLOCK), BF16),
                            pltpu.VMEM((n_blk, LANES), F32),
                            pltpu.VMEM((HEADS_PER_TILE, n_blk, MOBA_BLOCK), F32)]),
        out_shape=jax.ShapeDtypeStruct((n_batch, seq, D_A), BF16),
        compiler_params=_params(3),
        name="moba_prompt",
    )(bfar, q, kt, vt, town, tprev)


def _pool_project(window_sum, u_new, cnt, pw_ref, ps_ref):
    ys = []
    for g, w in enumerate(POOL_WINDOWS):
        lanes = slice(g * POOL_GROUP, (g + 1) * POOL_GROUP)
        d = window_sum(g, w) / cnt(w) - u_new[:, lanes]
        ys.append(jnp.dot(d.astype(BF16), pw_ref[g], preferred_element_type=F32))
    return jnp.concatenate(ys, axis=1) * ps_ref[...]


def _out_project(x1, a, p, wout_ref, g_ref, b_ref, alpha):
    proj = (jnp.dot(a, wout_ref[:D_A, :], preferred_element_type=F32)
            + jnp.dot(p.astype(BF16), wout_ref[D_A:, :], preferred_element_type=F32))
    return _layer_norm(alpha * x1 + proj, g_ref[...], b_ref[...])


def _mix_kernel(x1_ref, a_ref, u_ref, uh_ref, pw_ref, ps_ref, wout_ref, g1_ref, b1_ref,
                wxq_ref, wxo_ref, mk_ref, mv_ref, g2_ref, b2_ref, x3_ref, ext_ref, *, alpha, tm, tpb):
    t_in_b = pl.program_id(0) % tpb
    ext_ref[:POOL_HALO, :] = jnp.where(t_in_b == 0, 0.0, uh_ref[...])
    ext_ref[POOL_HALO:, :] = u_ref[...]
    pos = t_in_b * tm + lax.broadcasted_iota(jnp.int32, (tm, 1), 0)

    def window_sum(g, w):
        lanes = slice(g * POOL_GROUP, (g + 1) * POOL_GROUP)
        s = ext_ref[POOL_HALO:POOL_HALO + tm, lanes]
        for back in range(1, w):
            s = s + ext_ref[POOL_HALO - back:POOL_HALO - back + tm, lanes]
        return s

    p = _pool_project(window_sum, u_ref[...], lambda w: jnp.minimum(w, pos + 1).astype(F32), pw_ref, ps_ref)
    x2 = _out_project(x1_ref[...], a_ref[...], p, wout_ref, g1_ref, b1_ref, alpha)

    qx = jnp.dot(x2.astype(BF16), wxq_ref[...], preferred_element_type=F32) * HEAD_DIM_X ** -0.5
    outs = []
    for h in range(N_HEADS_X):
        lanes = slice(h * HEAD_DIM_X, (h + 1) * HEAD_DIM_X)
        logits = lax.dot_general(qx[:, lanes].astype(BF16), mk_ref[0, :, lanes], _NT, preferred_element_type=F32)
        e = jnp.exp(logits - jnp.max(logits, axis=-1, keepdims=True))
        o = jnp.dot(e.astype(BF16), mv_ref[0, :, lanes], preferred_element_type=F32)
        outs.append(o / jnp.sum(e, axis=-1, keepdims=True))
    o = jnp.concatenate(outs, axis=1).astype(BF16)
    x3_ref[...] = _layer_norm(alpha * x2 + jnp.dot(o, wxo_ref[...], preferred_element_type=F32),
                              g2_ref[...], b2_ref[...])


def _mix_call(x1, a, u, pw, ps, wout, g1, b1, wxq, wxo, mk, mv, g2, b2, *, alpha, n_batch, tm):
    n = x1.shape[0]
    tpb = n // n_batch // tm
    n_mem = mk.shape[1]
    row = lambda w: pl.BlockSpec((tm, w), lambda i: (i, 0))
    halo = pl.BlockSpec((POOL_HALO, D_POOL), lambda i: (jnp.maximum(i * (tm // POOL_HALO) - 1, 0), 0))
    mem = pl.BlockSpec((1, n_mem, D_MODEL), lambda i: (i // tpb, 0, 0))
    return pl.pallas_call(
        functools.partial(_mix_kernel, alpha=alpha, tm=tm, tpb=tpb),
        grid=(n // tm,),
        in_specs=[row(D_MODEL), row(D_A), row(D_POOL), halo, _resident(pw.shape), _resident(ps.shape),
                  _resident(wout.shape), _resident(g1.shape), _resident(b1.shape),
                  _resident(wxq.shape), _resident(wxo.shape), mem, mem,
                  _resident(g2.shape), _resident(b2.shape)],
        out_specs=row(D_MODEL),
        out_shape=jax.ShapeDtypeStruct((n, D_MODEL), F32),
        scratch_shapes=[pltpu.VMEM((tm + POOL_HALO, D_POOL), F32)],
        compiler_params=_params(1),
        name="mix_prompt",
    )(x1, a, u, u, pw, ps, wout, g1, b1, wxq, wxo, mk, mv, g2, b2)


def _page_copy(pool_ref, page, buf_ref, sem_ref, slot):
    return pltpu.make_async_copy(pool_ref.at[page], buf_ref.at[slot], sem_ref.at[slot])


def _kmean_kernel(pt_ref, kpool_ref, o_ref, buf_ref, sem_ref, *, n_pages):
    b = pl.program_id(0)
    n_blk = n_pages // PAGES_PER_BLOCK
    lane = lax.broadcasted_iota(jnp.int32, (D_A, LANES), 1)

    for s in range(KMEAN_RING):
        _page_copy(kpool_ref, pt_ref[b, s], buf_ref, sem_ref, s).start()
    o_ref[0] = jnp.zeros((D_A, LANES), F32)

    def block(j, _):
        total = None
        for half in range(PAGES_PER_BLOCK):
            page_no = j * PAGES_PER_BLOCK + half
            slot = page_no % KMEAN_RING
            _page_copy(kpool_ref, pt_ref[b, page_no], buf_ref, sem_ref, slot).wait()
            x = buf_ref[slot]
            total = x if total is None else total + x

            @pl.when(page_no + KMEAN_RING < n_pages)
            def _refill():
                _page_copy(kpool_ref, pt_ref[b, page_no + KMEAN_RING], buf_ref, sem_ref, slot).start()

        col = jnp.sum(total, axis=1, keepdims=True) * (1.0 / MOBA_BLOCK)
        o_ref[0] = jnp.where(lane == j, col, o_ref[0])
        return 0

    lax.fori_loop(0, n_blk, block, 0)


def _kmean_call(page_table, kpool_t):
    n_dec, n_pages = page_table.shape
    assert n_pages % PAGES_PER_BLOCK == 0 and n_pages >= KMEAN_RING and n_pages // PAGES_PER_BLOCK <= LANES
    return pl.pallas_call(
        functools.partial(_kmean_kernel, n_pages=n_pages),
        grid_spec=pltpu.PrefetchScalarGridSpec(
            num_scalar_prefetch=1,
            grid=(n_dec,),
            in_specs=[pl.BlockSpec(memory_space=pl.ANY)],
            out_specs=pl.BlockSpec((1, D_A, LANES), lambda b, *_: (b, 0, 0)),
            scratch_shapes=[pltpu.VMEM((KMEAN_RING, D_A, PAGE_SIZE), F32),
                            pltpu.SemaphoreType.DMA((KMEAN_RING,))]),
        out_shape=jax.ShapeDtypeStruct((n_dec, D_A, LANES), F32),
        compiler_params=_params(1),
        name="kmean_paged",
    )(page_table, kpool_t)


def _head_rows(q_row, n_rows):
    return _mask_heads(q_row, (n_rows, D_A), HEAD_DIM)


def _mask_heads(q_row, shape, head_dim):
    row = lax.broadcasted_iota(jnp.int32, shape, 0)
    lane_head = lax.broadcasted_iota(jnp.int32, shape, 1) // head_dim
    qb = jnp.broadcast_to(q_row.astype(F32), shape)
    return jnp.where(lane_head == row, qb, 0.0).astype(q_row.dtype)


def _gate_kernel(q_ref, km_ref, o_ref, *, n_blk):
    qm = _head_rows(q_ref[0], N_HEADS_A)
    gate = jnp.dot(qm, km_ref[0].astype(BF16), preferred_element_type=F32)
    lane = lax.broadcasted_iota(jnp.int32, gate.shape, 1)
    lane_f = lane.astype(F32)
    g = jnp.where(lane < n_blk, gate, NEG)
    out = jnp.zeros(gate.shape, F32)
    for t in range(MOBA_TOPK):
        m = jnp.max(g, axis=1, keepdims=True)
        idx = jnp.min(jnp.where(g == m, lane_f, float(LANES)), axis=1, keepdims=True)
        out = jnp.where(lane == t, idx, out)
        g = jnp.where(lane_f == idx, -jnp.inf, g)
    o_ref[0] = out.astype(jnp.int32)


def _gate_call(q, kmean_t, n_blk):
    n_dec = q.shape[0]
    return pl.pallas_call(
        functools.partial(_gate_kernel, n_blk=n_blk),
        grid=(n_dec,),
        in_specs=[pl.BlockSpec((1, 1, D_A), lambda b: (b, 0, 0)),
                  pl.BlockSpec((1, D_A, LANES), lambda b: (b, 0, 0))],
        out_specs=pl.BlockSpec((1, N_HEADS_A, LANES), lambda b: (b, 0, 0)),
        out_shape=jax.ShapeDtypeStruct((n_dec, N_HEADS_A, LANES), jnp.int32),
        compiler_params=_params(1),
        name="gate_sample",
    )(q, kmean_t)


def _head_page_copy(pool_ref, page, h, buf_ref, sem_ref, which, slot):
    rows = pl.ds(h * HEAD_DIM, HEAD_DIM)
    return pltpu.make_async_copy(pool_ref.at[page, rows, :], buf_ref.at[which, slot], sem_ref.at[which, slot])


def _sample_attn_kernel(top_ref, pt_ref, q_ref, kn_ref, vn_ref, tnear_ref, bias0_ref, bfar_ref,
                        kpool_ref, vpool_ref, o_ref, buf_ref, sem_ref, *, n_blk):
    b = pl.program_id(0)
    n_sel = MOBA_TOPK * PAGES_PER_BLOCK

    def copies(h):
        out = []
        for t in range(MOBA_TOPK):
            blk = top_ref[(b * N_HEADS_A + h) * MOBA_TOPK + t]
            for half in range(PAGES_PER_BLOCK):
                page = pt_ref[b, blk * PAGES_PER_BLOCK + half]
                slot = h * n_sel + t * PAGES_PER_BLOCK + half
                out.append(_head_page_copy(kpool_ref, page, h, buf_ref, sem_ref, 0, slot))
                out.append(_head_page_copy(vpool_ref, page, h, buf_ref, sem_ref, 1, slot))
        return out

    for h in range(N_HEADS_A):
        for c in copies(h):
            c.start()

    q = q_ref[0]
    kn = kn_ref[0].astype(BF16).astype(F32)
    vn = vn_ref[0].astype(BF16).astype(F32)
    s_new_all = q.astype(F32) * kn
    outs = []
    for h in range(N_HEADS_A):
        for c in copies(h):
            c.wait()
        lanes = slice(h * HEAD_DIM, (h + 1) * HEAD_DIM)
        qh = jnp.broadcast_to(q[:, lanes], (SUBLANES, HEAD_DIM))
        kt = jnp.concatenate([buf_ref[0, h * n_sel + s] for s in range(n_sel)], axis=1).astype(BF16)
        vt = jnp.concatenate([buf_ref[1, h * n_sel + s] for s in range(n_sel)], axis=1).astype(BF16)
        bias = []
        for t in range(MOBA_TOPK):
            blk = top_ref[(b * N_HEADS_A + h) * MOBA_TOPK + t]
            bias.append(jnp.where(blk == n_blk - 1, tnear_ref[h:h + 1, :], bfar_ref[h]))
        s = jnp.dot(qh, kt, preferred_element_type=F32)[:1] + jnp.concatenate(bias, axis=1)
        s_new = jnp.sum(s_new_all[:, lanes], axis=1, keepdims=True) + bias0_ref[h]
        m = jnp.maximum(jnp.max(s, axis=1, keepdims=True), s_new)
        p = jnp.exp(s - m)
        p_new = jnp.exp(s_new - m)
        l = jnp.sum(p, axis=1, keepdims=True) + p_new
        pb = jnp.broadcast_to(p.astype(BF16), (SUBLANES, p.shape[1]))
        o = lax.dot_general(pb, vt, _NT, preferred_element_type=F32)[:1]
        o = o + p_new.astype(BF16).astype(F32) * vn[:, lanes]
        outs.append(o / l)
    o_ref[0] = jnp.concatenate(outs, axis=1)


def _sample_attn_call(top, page_table, q, kn, vn, tnear, bias0, bfar, kpool_t, vpool_t, n_blk):
    n_dec = q.shape[0]
    row = pl.BlockSpec((1, 1, D_A), lambda b, *_: (b, 0, 0))
    smem = pl.BlockSpec(memory_space=pltpu.SMEM)
    any_ = pl.BlockSpec(memory_space=pl.ANY)
    n_slots = N_HEADS_A * MOBA_TOPK * PAGES_PER_BLOCK
    return pl.pallas_call(
        functools.partial(_sample_attn_kernel, n_blk=n_blk),
        grid_spec=pltpu.PrefetchScalarGridSpec(
            num_scalar_prefetch=2,
            grid=(n_dec,),
            in_specs=[row, row, row, pl.BlockSpec(tnear.shape, lambda b, *_: (0, 0)), smem, smem, any_, any_],
            out_specs=row,
            scratch_shapes=[pltpu.VMEM((2, n_slots, HEAD_DIM, PAGE_SIZE), F32),
                            pltpu.SemaphoreType.DMA((2, n_slots))]),
        out_shape=jax.ShapeDtypeStruct((n_dec, 1, D_A), F32),
        compiler_params=_params(1),
        name="attn_sample",
    )(top, page_table, q, kn, vn, tnear, bias0, bfar, kpool_t, vpool_t)


def _mix_sample_kernel(x1_ref, a_ref, u_ref, st_ref, pw_ref, ps_ref, wout_ref, g1_ref, b1_ref, wxq_ref,
                       x2_ref, qx_ref, *, alpha, pos):
    u = u_ref[...]

    def window_sum(g, w):
        lanes = slice(g * POOL_GROUP, (g + 1) * POOL_GROUP)
        s = u[:, lanes]
        for back in range(1, w):
            s = s + st_ref[POOL_STATE - back, :, lanes]
        return s

    p = _pool_project(window_sum, u, lambda w: float(min(w, pos + 1)), pw_ref, ps_ref)
    x2 = _out_project(x1_ref[...], a_ref[...].astype(BF16), p, wout_ref, g1_ref, b1_ref, alpha)
    x2_ref[...] = x2
    qx = jnp.dot(x2.astype(BF16), wxq_ref[...], preferred_element_type=F32) * HEAD_DIM_X ** -0.5
    qx_ref[...] = qx.astype(BF16)


def _mix_sample_call(x1, a, u, state_t, pw, ps, wout, g1, b1, wxq, *, alpha, pos):
    n = x1.shape[0]
    args = (x1, a, u, state_t, pw, ps, wout, g1, b1, wxq)
    return pl.pallas_call(
        functools.partial(_mix_sample_kernel, alpha=alpha, pos=pos),
        grid=(1,),
        in_specs=[_resident(v.shape) for v in args],
        out_specs=[_resident((n, D_MODEL)), _resident((n, D_MODEL))],
        out_shape=[jax.ShapeDtypeStruct((n, D_MODEL), F32), jax.ShapeDtypeStruct((n, D_MODEL), BF16)],
        compiler_params=_params(1),
        name="mix_sample",
    )(*args)


def _xattn_sample_kernel(qx_ref, mk_ref, mv_ref, o_ref):
    qm = _head_rows_x(qx_ref[0])
    logits = lax.dot_general(qm, mk_ref[0].astype(BF16), _NT, preferred_element_type=F32)
    e = jnp.exp(logits - jnp.max(logits, axis=-1, keepdims=True))
    o = jnp.dot(e.astype(BF16), mv_ref[0].astype(BF16), preferred_element_type=F32)
    o = o / jnp.sum(e, axis=-1, keepdims=True)
    row = lax.broadcasted_iota(jnp.int32, o.shape, 0)
    lane_head = lax.broadcasted_iota(jnp.int32, o.shape, 1) // HEAD_DIM_X
    o_ref[0] = jnp.sum(jnp.where(lane_head == row, o, 0.0), axis=0, keepdims=True)


def _head_rows_x(q_row):
    return _mask_heads(q_row, (SUBLANES, D_MODEL), HEAD_DIM_X)


def _xattn_sample_call(qx, mk, mv):
    n_dec, n_mem, _ = mk.shape
    row = pl.BlockSpec((1, 1, D_MODEL), lambda b: (b, 0, 0))
    mem = pl.BlockSpec((1, n_mem, D_MODEL), lambda b: (b, 0, 0))
    return pl.pallas_call(
        _xattn_sample_kernel,
        grid=(n_dec,),
        in_specs=[row, mem, mem],
        out_specs=row,
        out_shape=jax.ShapeDtypeStruct((n_dec, 1, D_MODEL), F32),
        compiler_params=_params(1),
        name="xattn_sample",
    )(qx, mk, mv)


def _proj_ln_kernel(x_ref, o_ref, w_ref, g_ref, b_ref, y_ref, *, alpha):
    proj = jnp.dot(o_ref[...].astype(BF16), w_ref[...], preferred_element_type=F32)
    y_ref[...] = _layer_norm(alpha * x_ref[...] + proj, g_ref[...], b_ref[...])


def _proj_ln_call(x, o, w, g, b, *, alpha):
    args = (x, o, w, g, b)
    return pl.pallas_call(
        functools.partial(_proj_ln_kernel, alpha=alpha),
        grid=(1,),
        in_specs=[_resident(v.shape) for v in args],
        out_specs=_resident(x.shape),
        out_shape=jax.ShapeDtypeStruct(x.shape, F32),
        compiler_params=_params(1),
        name="proj_ln",
    )(*args)


def _t5_bucket(dist):
    n = jnp.maximum(dist, 0)
    max_exact = N_BUCKETS // 2
    nf = jnp.maximum(n, 1).astype(F32)
    large = max_exact + (jnp.log(nf / max_exact) / math.log(MAX_DISTANCE / max_exact)
                         * (N_BUCKETS - max_exact)).astype(jnp.int32)
    return jnp.where(n < max_exact, n, jnp.minimum(large, N_BUCKETS - 1))


def _far_bucket_is_last():
    max_exact = N_BUCKETS // 2
    v = max_exact + int(math.log((MOBA_BLOCK + 1) / max_exact) / math.log(MAX_DISTANCE / max_exact)
                        * (N_BUCKETS - max_exact))
    return v >= N_BUCKETS - 1


def kernel(x_prompt, x_sample, cache_k, cache_v, cache_mem_k, cache_mem_v, state_pool, page_table, mem_prompt, rel_bias, ln_g, ln_b, w_ff1_gate, w_ff1_up, w_ff1_down, w_in, pool_w, pool_scale, w_out, w_xq, w_xk, w_xv, w_xo, w_ff2_gate, w_ff2_up, w_ff2_down):
    n_batch, seq, _ = x_prompt.shape
    n_dec, dec_seq, _ = x_sample.shape
    depth = ln_g.shape[0]
    n_pool = cache_k.shape[1]
    n_pages = page_table.shape[1]
    past_len = n_pages * PAGE_SIZE
    n_mem = mem_prompt.shape[1]
    assert depth == 1 and dec_seq == 1
    assert seq % MOBA_BLOCK == 0 and past_len % MOBA_BLOCK == 0
    assert past_len // MOBA_BLOCK >= MOBA_TOPK and _far_bucket_is_last()
    alpha = (2 * depth) ** 0.25
    tm = min(512, seq)

    bf = lambda w: w[0].astype(BF16)
    ff1 = (bf(w_ff1_gate), bf(w_ff1_up), bf(w_ff1_down))
    ff2 = (bf(w_ff2_gate), bf(w_ff2_up), bf(w_ff2_down))
    w_in0 = w_in[0]
    wqu = jnp.concatenate([w_in0[:, :D_A], w_in0[:, 3 * D_A:]], axis=1).astype(BF16)
    wkvt = w_in0[:, D_A:3 * D_A].T.astype(BF16)
    ln = lambda i: (ln_g[0, i:i + 1], ln_b[0, i:i + 1])
    (g0, b0), (g1, b1), (g2, b2), (g3, b3) = ln(0), ln(1), ln(2), ln(3)
    pw, ps = pool_w[0].astype(BF16), pool_scale
    wout, wxq, wxo = bf(w_out), bf(w_xq), bf(w_xo)

    bias_by_dist = rel_bias[_t5_bucket(jnp.arange(2 * MOBA_BLOCK, dtype=jnp.int32))].T
    bfar = bias_by_dist[:, 2 * MOBA_BLOCK - 1]
    key = jnp.arange(MOBA_BLOCK, dtype=jnp.int32)[:, None]
    qry = jnp.arange(MOBA_BLOCK, dtype=jnp.int32)[None, :]
    d_own = qry - key
    town = jnp.where(d_own >= 0, bias_by_dist[:, jnp.maximum(d_own, 0)], NEG)
    tprev = bias_by_dist[:, d_own + MOBA_BLOCK]

    xp = x_prompt.reshape(n_batch * seq, D_MODEL)
    x1, q, kt, vt, u = _ffn_proj_call(xp, *ff1, g0, b0, wqu, wkvt, alpha=alpha, n_batch=n_batch, tm=tm)
    a = _moba_call(q.reshape(n_batch, seq, D_A), kt, vt, town, tprev, bfar)
    mk, mv, mkb, mvb = _memkv_call(mem_prompt.reshape(n_batch * n_mem, D_MODEL), bf(w_xk), bf(w_xv),
                                   tm=min(512, n_batch * n_mem))
    x3 = _mix_call(x1, a.reshape(n_batch * seq, D_A), u, pw, ps, wout, g1, b1, wxq, wxo,
                   mkb.reshape(n_batch, n_mem, D_MODEL), mvb.reshape(n_batch, n_mem, D_MODEL), g2, b2,
                   alpha=alpha, n_batch=n_batch, tm=tm)
    y_prompt = _ffn_call(x3, *ff2, g3, b3, alpha=alpha, tm=tm).reshape(n_batch, seq, D_MODEL)

    to_heads = lambda t: t.reshape(t.shape[0], N_HEADS_A, HEAD_DIM, t.shape[2]).transpose(0, 3, 1, 2)[None]
    k_prompt, v_prompt = to_heads(kt), to_heads(vt)
    pool_prompt = u.reshape(n_batch, seq, D_POOL)[None, :, seq - POOL_STATE:]
    memk_prompt = mk.reshape(1, n_batch, n_mem, N_HEADS_X, HEAD_DIM_X)
    memv_prompt = mv.reshape(1, n_batch, n_mem, N_HEADS_X, HEAD_DIM_X)

    xs = x_sample.reshape(n_dec, D_MODEL)
    x1s, qs, kts, vts, us = _ffn_proj_call(xs, *ff1, g0, b0, wqu, wkvt, alpha=alpha, n_batch=1, tm=n_dec)
    kn, vn = kts[0].T, vts[0].T
    kpool_t = cache_k[0].transpose(0, 2, 3, 1).reshape(n_pool, D_A, PAGE_SIZE)
    vpool_t = cache_v[0].transpose(0, 2, 3, 1).reshape(n_pool, D_A, PAGE_SIZE)
    n_blk_s = past_len // MOBA_BLOCK
    kmean_t = _kmean_call(page_table, kpool_t)
    qs3 = qs.reshape(n_dec, 1, D_A)
    top = _gate_call(qs3, kmean_t, n_blk_s)[:, :, :MOBA_TOPK].reshape(-1)
    tnear = bias_by_dist[:, MOBA_BLOCK - jnp.arange(MOBA_BLOCK)]
    a_s = _sample_attn_call(top, page_table, qs3, kn.reshape(n_dec, 1, D_A), vn.reshape(n_dec, 1, D_A),
                            tnear, bias_by_dist[:, 0], bfar, kpool_t, vpool_t, n_blk_s)
    state_t = state_pool[0].transpose(1, 0, 2)
    x2s, qxs = _mix_sample_call(x1s, a_s.reshape(n_dec, D_A), us, state_t, pw, ps, wout, g1, b1, wxq,
                                alpha=alpha, pos=past_len)
    o_s = _xattn_sample_call(qxs.reshape(n_dec, 1, D_MODEL),
                             cache_mem_k[0].reshape(n_dec, n_mem, D_MODEL),
                             cache_mem_v[0].reshape(n_dec, n_mem, D_MODEL))
    x3s = _proj_ln_call(x2s, o_s.reshape(n_dec, D_MODEL), wxo, g2, b2, alpha=alpha)
    y_sample = _ffn_call(x3s, *ff2, g3, b3, alpha=alpha, tm=n_dec).reshape(n_dec, 1, D_MODEL)

    k_sample = kn.reshape(1, n_dec, 1, N_HEADS_A, HEAD_DIM)
    v_sample = vn.reshape(1, n_dec, 1, N_HEADS_A, HEAD_DIM)
    pool_sample = jnp.concatenate([state_t[1:], us[None]], axis=0).transpose(1, 0, 2)[None]
    return (y_prompt, y_sample, k_prompt, v_prompt, pool_prompt, memk_prompt, memv_prompt,
            k_sample, v_sample, pool_sample)
```

```python
import functools
import math

import jax
import jax.numpy as jnp
import numpy as np
from jax import lax
from jax.experimental import pallas as pl
from jax.experimental.pallas import tpu as pltpu

F32 = jnp.float32
BF16 = jnp.bfloat16

D_MODEL = 1024
HEAD_DIM = 64
N_HEADS_A = 8
D_A = N_HEADS_A * HEAD_DIM
MOBA_BLOCK = 256
MOBA_TOPK = 3
D_POOL = D_MODEL - D_A
POOL_WINDOWS = (2, 4, 8, 16)
POOL_GROUP = D_POOL // len(POOL_WINDOWS)
POOL_STATE = max(POOL_WINDOWS) - 1
N_BUCKETS = 32
MAX_DISTANCE = 128
N_HEADS_X = 4
HEAD_DIM_X = D_MODEL // N_HEADS_X
D_FF = 2816
LN_EPS = 1e-5
PAGE_SIZE = 128
NEG = -1e30

LANES = 128
SUBLANES = 8
VMEM_LIMIT_BYTES = 56 * 1024 * 1024

HEADS_PER_TILE = LANES // HEAD_DIM
POOL_HALO = POOL_STATE + 1
PAGES_PER_BLOCK = MOBA_BLOCK // PAGE_SIZE
FFN_CHUNK = 256
KMEAN_RING = 32
SCORE_CHUNK = 32

_NT = (((1,), (1,)), ((), ()))


def _params(n_grid_dims):
    return pltpu.CompilerParams(
        dimension_semantics=("arbitrary",) * n_grid_dims, vmem_limit_bytes=VMEM_LIMIT_BYTES)


def _resident(shape):
    return pl.BlockSpec(shape, lambda *_: (0,) * len(shape), pipeline_mode=pl.Buffered(1))


def _layer_norm(z, g, b):
    mu = jnp.mean(z, axis=-1, keepdims=True)
    zc = z - mu
    var = jnp.mean(zc * zc, axis=-1, keepdims=True)
    return zc * lax.rsqrt(var + LN_EPS) * g + b


def _swiglu(x, wg_ref, wu_ref, wd_ref):
    xb = x.astype(BF16)
    acc = None
    for c in range(D_FF // FFN_CHUNK):
        sl = slice(c * FFN_CHUNK, (c + 1) * FFN_CHUNK)
        g = jnp.dot(xb, wg_ref[:, sl], preferred_element_type=F32)
        u = jnp.dot(xb, wu_ref[:, sl], preferred_element_type=F32)
        h = (g * jax.nn.sigmoid(g) * u).astype(BF16)
        part = jnp.dot(h, wd_ref[sl, :], preferred_element_type=F32)
        acc = part if acc is None else acc + part
    return acc


def _ffn_kernel(x_ref, wg_ref, wu_ref, wd_ref, g_ref, b_ref, y_ref, *, alpha):
    x = x_ref[...]
    y_ref[...] = _layer_norm(alpha * x + 0.5 * _swiglu(x, wg_ref, wu_ref, wd_ref), g_ref[...], b_ref[...])


def _ffn_proj_kernel(x_ref, wg_ref, wu_ref, wd_ref, g_ref, b_ref, win_ref,
                     x1_ref, q_ref, k_ref, v_ref, u_ref, *moba_refs, alpha):
    x = x_ref[...]
    x1 = _layer_norm(alpha * x + 0.5 * _swiglu(x, wg_ref, wu_ref, wd_ref), g_ref[...], b_ref[...])
    x1_ref[...] = x1
    h = jnp.dot(x1.astype(BF16), win_ref[...], preferred_element_type=F32)
    q_ref[...] = (h[:, :D_A] * HEAD_DIM ** -0.5).astype(BF16)
    k = h[:, D_A:2 * D_A]
    v = h[:, 2 * D_A:3 * D_A]
    u_ref[...] = h[:, 3 * D_A:]
    if not moba_refs:
        k_ref[...] = k
        v_ref[...] = v
        return
    kb_ref, vtb_ref, km_ref = moba_refs
    vt = v.T
    k_ref[0] = k.T
    v_ref[0] = vt
    kb_ref[...] = k.astype(BF16)
    for blk in range(k.shape[0] // MOBA_BLOCK):
        rows = slice(blk * MOBA_BLOCK, (blk + 1) * MOBA_BLOCK)
        vtb_ref[0, blk] = vt[:, rows].astype(BF16)
        km_ref[blk] = jnp.mean(k[rows], axis=0, keepdims=True)


def _ffn_call(x, wg, wu, wd, g, b, *, alpha, tm):
    n = x.shape[0]
    row = pl.BlockSpec((tm, D_MODEL), lambda i: (i, 0))
    return pl.pallas_call(
        functools.partial(_ffn_kernel, alpha=alpha),
        grid=(n // tm,),
        in_specs=[row, _resident(wg.shape), _resident(wu.shape), _resident(wd.shape),
                  _resident(g.shape), _resident(b.shape)],
        out_specs=row,
        out_shape=jax.ShapeDtypeStruct((n, D_MODEL), F32),
        compiler_params=_params(1),
        name="ffn",
    )(x, wg, wu, wd, g, b)


def _ffn_proj_call(x, wg, wu, wd, g, b, win, *, alpha, tm, moba_batch=None):
    n = x.shape[0]
    row = lambda w: pl.BlockSpec((tm, w), lambda i: (i, 0))
    sds = jax.ShapeDtypeStruct
    out_specs = [row(D_MODEL), row(D_A)]
    out_shape = [sds((n, D_MODEL), F32), sds((n, D_A), BF16)]
    if moba_batch is None:
        out_specs += [row(D_A), row(D_A), row(D_POOL)]
        out_shape += [sds((n, D_A), F32), sds((n, D_A), F32), sds((n, D_POOL), F32)]
    else:
        seq = n // moba_batch
        tpb = seq // tm
        bpt = tm // MOBA_BLOCK
        tcol = pl.BlockSpec((1, D_A, tm), lambda i: (i // tpb, 0, i % tpb))
        out_specs += [tcol, tcol, row(D_POOL), row(D_A),
                      pl.BlockSpec((1, bpt, D_A, MOBA_BLOCK), lambda i: (i // tpb, i % tpb, 0, 0)),
                      pl.BlockSpec((bpt, 1, D_A), lambda i: (i, 0, 0))]
        out_shape += [sds((moba_batch, D_A, seq), F32), sds((moba_batch, D_A, seq), F32), sds((n, D_POOL), F32),
                      sds((n, D_A), BF16), sds((moba_batch, seq // MOBA_BLOCK, D_A, MOBA_BLOCK), BF16),
                      sds((n // MOBA_BLOCK, 1, D_A), F32)]
    return pl.pallas_call(
        functools.partial(_ffn_proj_kernel, alpha=alpha),
        grid=(n // tm,),
        in_specs=[row(D_MODEL), _resident(wg.shape), _resident(wu.shape), _resident(wd.shape),
                  _resident(g.shape), _resident(b.shape), _resident(win.shape)],
        out_specs=out_specs,
        out_shape=out_shape,
        compiler_params=_params(1),
        name="ffn_proj",
    )(x, wg, wu, wd, g, b, win)


def _memkv_kernel(m_ref, wk_ref, wv_ref, k_ref, v_ref, kb_ref, vb_ref):
    mb = m_ref[...].astype(BF16)
    k = jnp.dot(mb, wk_ref[...], preferred_element_type=F32)
    v = jnp.dot(mb, wv_ref[...], preferred_element_type=F32)
    k_ref[...] = k
    v_ref[...] = v
    kb_ref[...] = k.astype(BF16)
    vb_ref[...] = v.astype(BF16)


def _memkv_call(mem, wk, wv, *, tm):
    n = mem.shape[0]
    row = pl.BlockSpec((tm, D_MODEL), lambda i: (i, 0))
    sds = lambda dt: jax.ShapeDtypeStruct((n, D_MODEL), dt)
    return pl.pallas_call(
        _memkv_kernel,
        grid=(n // tm,),
        in_specs=[row, _resident(wk.shape), _resident(wv.shape)],
        out_specs=[row, row, row, row],
        out_shape=[sds(F32), sds(F32), sds(BF16), sds(BF16)],
        compiler_params=_params(1),
        name="memkv",
    )(mem, wk, wv)


def _top3_rows(gate, n_valid, n_rows):
    rid = lax.broadcasted_iota(jnp.int32, gate.shape, 0).astype(F32)
    g = jnp.where(rid < n_valid, gate, NEG)
    sel = jnp.zeros(gate.shape, jnp.bool_)
    for _ in range(MOBA_TOPK):
        m = jnp.max(g, axis=0, keepdims=True)
        idx = jnp.min(jnp.where(g == m, rid, float(n_rows)), axis=0, keepdims=True)
        pick = rid == idx
        sel = jnp.logical_or(sel, jnp.logical_and(pick, idx < n_valid))
        g = jnp.where(pick, -jnp.inf, g)
    return sel


def _moba_kernel(rb_ref, q_ref, kb_ref, vtb_ref, km_ref, bkt_ref, o_ref,
                 tbl_ref, rbt_ref, qh_ref, s_ref, p_ref, acc_ref, m_ref, l_ref, al_ref, *, n_blk, far_bucket):
    b = pl.program_id(0)
    i = pl.program_id(1)
    row_zero, row_prev = n_blk, n_blk + 1
    n_chunk = MOBA_BLOCK // SCORE_CHUNK
    chunks = [slice(c * SCORE_CHUNK, (c + 1) * SCORE_CHUNK) for c in range(n_chunk)]
    head_lanes = lambda h: slice((h // HEADS_PER_TILE) * LANES, (h // HEADS_PER_TILE + 1) * LANES)

    @pl.when(jnp.logical_and(b == 0, i == 0))
    def _build_bias_tiles():
        def per_head(h, _):
            for slot in range(2):
                bk = bkt_ref[slot]
                t = jnp.zeros(bk.shape, F32)
                for bucket in range(N_BUCKETS):
                    t = jnp.where(bk == bucket, rb_ref[bucket * N_HEADS_A + h], t)
                tbl_ref[h, slot] = jnp.where(bk < 0, NEG, t)
            return 0
        lax.fori_loop(0, N_HEADS_A, per_head, 0)

    q = q_ref[0]
    km = km_ref[0]
    i_f = i.astype(F32)
    jp_f = jnp.maximum(i - 1, 0).astype(F32)
    lane_head = lax.broadcasted_iota(jnp.int32, (MOBA_BLOCK, LANES), 1) // HEAD_DIM
    rid = lax.broadcasted_iota(jnp.int32, (n_blk, MOBA_BLOCK), 0).astype(F32)
    for h in range(N_HEADS_A):
        q2 = q[:, head_lanes(h)].astype(F32)
        qh = jnp.where(lane_head == h % HEADS_PER_TILE, q2, 0.0).astype(BF16)
        qh_ref[h] = qh
        gate = lax.dot_general(km[:, head_lanes(h)].astype(BF16), qh, _NT, preferred_element_type=F32)
        sel = _top3_rows(gate, i_f, n_blk)
        rbt_ref[h, :n_blk] = jnp.where(sel, rb_ref[far_bucket * N_HEADS_A + h], NEG)
        prev_sel = jnp.max(jnp.where(jnp.logical_and(sel, rid == jp_f), 1.0, 0.0), axis=0, keepdims=True)
        rbt_ref[h, row_zero:row_zero + 1] = jnp.zeros((1, MOBA_BLOCK), F32)
        rbt_ref[h, row_prev:row_prev + 1] = jnp.where(prev_sel > 0.5, 0.0, NEG)
        m_ref[h] = jnp.full((1, MOBA_BLOCK), NEG, F32)
        l_ref[h] = jnp.zeros((1, MOBA_BLOCK), F32)
        al_ref[h] = jnp.ones((1, MOBA_BLOCK), F32)
        acc_ref[h] = jnp.zeros((HEAD_DIM, MOBA_BLOCK), F32)
        p_ref[h] = jnp.zeros((MOBA_BLOCK, MOBA_BLOCK), BF16)

    def list_block(t):
        return jnp.clip(jnp.where(t == 0, i, jnp.where(t == 1, i - 1, t - 2)), 0, n_blk - 1)

    def stage_a(t):
        rows = pl.ds(pl.multiple_of(list_block(t) * MOBA_BLOCK, MOBA_BLOCK), MOBA_BLOCK)
        for h in range(N_HEADS_A):
            s_ref[h] = lax.dot_general(kb_ref[0, rows, head_lanes(h)], qh_ref[h], _NT, preferred_element_type=F32)

    def add_bias_tile(slot):
        for h in range(N_HEADS_A):
            s_ref[h] = s_ref[h] + tbl_ref[h, slot]

    def stage_b(t):
        row = jnp.where(t == 0, row_zero, jnp.where(t == 1, row_prev, t - 2))
        for h in range(N_HEADS_A):
            rb = rbt_ref[h, pl.ds(row, 1), :]
            mx = s_ref[h, chunks[0], :]
            for c in chunks[1:]:
                mx = jnp.maximum(mx, s_ref[h, c, :])
            m_old = m_ref[h]
            m_new = jnp.maximum(m_old, jnp.max(mx, axis=0, keepdims=True) + rb)
            shift = m_new - rb
            psum = None
            for c in chunks:
                p = jnp.exp(s_ref[h, c, :] - shift)
                p_ref[h, c, :] = p.astype(BF16)
                psum = p if psum is None else psum + p
            alpha = jnp.exp(m_old - m_new)
            l_ref[h] = alpha * l_ref[h] + jnp.sum(psum, axis=0, keepdims=True)
            m_ref[h] = m_new
            al_ref[h] = alpha

    def stage_c(t):
        j = list_block(t)
        for h in range(N_HEADS_A):
            vt = vtb_ref[0, j, h * HEAD_DIM:(h + 1) * HEAD_DIM, :]
            acc_ref[h] = al_ref[h] * acc_ref[h] + jnp.dot(vt, p_ref[h], preferred_element_type=F32)

    stage_a(0)
    add_bias_tile(0)

    def step(t, carry):
        stage_c(jnp.maximum(t - 2, 0))
        stage_b(t - 1)
        stage_a(t)

        @pl.when(t == 1)
        def _():
            add_bias_tile(1)

        return carry

    lax.fori_loop(1, i + 2, step, 0)
    stage_c(i)
    out_t = jnp.concatenate([acc_ref[h] / l_ref[h] for h in range(N_HEADS_A)], axis=0)
    o_ref[0] = out_t.T.astype(BF16)


def _bucket_np(dist):
    n = np.maximum(dist, 0)
    max_exact = N_BUCKETS // 2
    nf = np.maximum(n, 1).astype(np.float32)
    large = max_exact + (np.log(nf / np.float32(max_exact)) / np.float32(math.log(MAX_DISTANCE / max_exact))
                         * np.float32(N_BUCKETS - max_exact)).astype(np.int32)
    return np.where(n < max_exact, n, np.minimum(large, N_BUCKETS - 1)).astype(np.int32)


def _bucket_tiles():
    key = np.arange(MOBA_BLOCK)[:, None]
    qry = np.arange(MOBA_BLOCK)[None, :]
    own = np.where(qry >= key, _bucket_np(qry - key), -1)
    prev = _bucket_np(qry - key + MOBA_BLOCK)
    return np.stack([own, prev]).astype(np.int32)


def _far_bucket():
    far = _bucket_np(np.array([MOBA_BLOCK + 1, 1 << 30]))
    assert far[0] == far[1]
    return int(far[0])


def _moba_call(rel_bias_flat, q, kb, vtb, km):
    n_batch, seq, _ = q.shape
    n_blk = seq // MOBA_BLOCK
    qo = pl.BlockSpec((1, MOBA_BLOCK, D_A), lambda b, i, *_: (b, i, 0))
    per_batch = lambda shape: pl.BlockSpec((1,) + shape, lambda b, i, *_: (b,) + (0,) * len(shape))
    tile = (MOBA_BLOCK, MOBA_BLOCK)
    stat = pltpu.VMEM((N_HEADS_A, 1, MOBA_BLOCK), F32)
    return pl.pallas_call(
        functools.partial(_moba_kernel, n_blk=n_blk, far_bucket=_far_bucket()),
        grid_spec=pltpu.PrefetchScalarGridSpec(
            num_scalar_prefetch=1,
            grid=(n_batch, n_blk),
            in_specs=[qo, per_batch((seq, D_A)), per_batch((n_blk, D_A, MOBA_BLOCK)), per_batch((n_blk, D_A)),
                      _resident((2,) + tile)],
            out_specs=qo,
            scratch_shapes=[pltpu.VMEM((N_HEADS_A, 2) + tile, F32),
                            pltpu.VMEM((N_HEADS_A, n_blk + SUBLANES, MOBA_BLOCK), F32),
                            pltpu.VMEM((N_HEADS_A, MOBA_BLOCK, LANES), BF16),
                            pltpu.VMEM((N_HEADS_A,) + tile, F32),
                            pltpu.VMEM((N_HEADS_A,) + tile, BF16),
                            pltpu.VMEM((N_HEADS_A, HEAD_DIM, MOBA_BLOCK), F32),
                            stat, stat, stat]),
        out_shape=jax.ShapeDtypeStruct((n_batch, seq, D_A), BF16),
        compiler_params=_params(2),
        name="moba_prompt",
    )(rel_bias_flat, q, kb, vtb, km, jnp.asarray(_bucket_tiles()))


def _pool_project(window_sum, u_new, cnt, pw_ref, ps_ref):
    ys = []
    for g, w in enumerate(POOL_WINDOWS):
        lanes = slice(g * POOL_GROUP, (g + 1) * POOL_GROUP)
        d = window_sum(g, w) / cnt(w) - u_new[:, lanes]
        ys.append(jnp.dot(d.astype(BF16), pw_ref[g], preferred_element_type=F32))
    return jnp.concatenate(ys, axis=1) * ps_ref[...]


def _out_project(x1, a, p, wout_ref, g_ref, b_ref, alpha):
    proj = (jnp.dot(a, wout_ref[:D_A, :], preferred_element_type=F32)
            + jnp.dot(p.astype(BF16), wout_ref[D_A:, :], preferred_element_type=F32))
    return _layer_norm(alpha * x1 + proj, g_ref[...], b_ref[...])


def _mix_kernel(x1_ref, a_ref, u_ref, uh_ref, pw_ref, ps_ref, wout_ref, g1_ref, b1_ref,
                wxq_ref, wxo_ref, mk_ref, mv_ref, g2_ref, b2_ref, x3_ref, ext_ref, *, alpha, tm, tpb):
    t_in_b = pl.program_id(0) % tpb
    ext_ref[:POOL_HALO, :] = jnp.where(t_in_b == 0, 0.0, uh_ref[...])
    ext_ref[POOL_HALO:, :] = u_ref[...]
    pos = t_in_b * tm + lax.broadcasted_iota(jnp.int32, (tm, 1), 0)

    def window_sum(g, w):
        lanes = slice(g * POOL_GROUP, (g + 1) * POOL_GROUP)
        s = ext_ref[POOL_HALO:POOL_HALO + tm, lanes]
        for back in range(1, w):
            s = s + ext_ref[POOL_HALO - back:POOL_HALO - back + tm, lanes]
        return s

    p = _pool_project(window_sum, u_ref[...], lambda w: jnp.minimum(w, pos + 1).astype(F32), pw_ref, ps_ref)
    x2 = _out_project(x1_ref[...], a_ref[...], p, wout_ref, g1_ref, b1_ref, alpha)

    qx = jnp.dot(x2.astype(BF16), wxq_ref[...], preferred_element_type=F32) * HEAD_DIM_X ** -0.5
    outs = []
    for h in range(N_HEADS_X):
        lanes = slice(h * HEAD_DIM_X, (h + 1) * HEAD_DIM_X)
        logits = lax.dot_general(qx[:, lanes].astype(BF16), mk_ref[0, :, lanes], _NT, preferred_element_type=F32)
        e = jnp.exp(logits - jnp.max(logits, axis=-1, keepdims=True))
        o = jnp.dot(e.astype(BF16), mv_ref[0, :, lanes], preferred_element_type=F32)
        outs.append(o / jnp.sum(e, axis=-1, keepdims=True))
    o = jnp.concatenate(outs, axis=1).astype(BF16)
    x3_ref[...] = _layer_norm(alpha * x2 + jnp.dot(o, wxo_ref[...], preferred_element_type=F32),
                              g2_ref[...], b2_ref[...])


def _mix_call(x1, a, u, pw, ps, wout, g1, b1, wxq, wxo, mk, mv, g2, b2, *, alpha, n_batch, tm):
    n = x1.shape[0]
    tpb = n // n_batch // tm
    n_mem = mk.shape[1]
    row = lambda w: pl.BlockSpec((tm, w), lambda i: (i, 0))
    halo = pl.BlockSpec((POOL_HALO, D_POOL), lambda i: (jnp.maximum(i * (tm // POOL_HALO) - 1, 0), 0))
    mem = pl.BlockSpec((1, n_mem, D_MODEL), lambda i: (i // tpb, 0, 0))
    return pl.pallas_call(
        functools.partial(_mix_kernel, alpha=alpha, tm=tm, tpb=tpb),
        grid=(n // tm,),
        in_specs=[row(D_MODEL), row(D_A), row(D_POOL), halo, _resident(pw.shape), _resident(ps.shape),
                  _resident(wout.shape), _resident(g1.shape), _resident(b1.shape),
                  _resident(wxq.shape), _resident(wxo.shape), mem, mem,
                  _resident(g2.shape), _resident(b2.shape)],
        out_specs=row(D_MODEL),
        out_shape=jax.ShapeDtypeStruct((n, D_MODEL), F32),
        scratch_shapes=[pltpu.VMEM((tm + POOL_HALO, D_POOL), F32)],
        compiler_params=_params(1),
        name="mix_prompt",
    )(x1, a, u, u, pw, ps, wout, g1, b1, wxq, wxo, mk, mv, g2, b2)


def _page_copy(pool_ref, page, buf_ref, sem_ref, slot):
    return pltpu.make_async_copy(pool_ref.at[page], buf_ref.at[slot], sem_ref.at[slot])


def _kmean_kernel(pt_ref, kpool_ref, o_ref, buf_ref, sem_ref, *, n_dec, n_pages):
    n_blk = n_pages // PAGES_PER_BLOCK
    n_total = n_dec * n_pages
    ring = min(KMEAN_RING, n_total)
    lane = lax.broadcasted_iota(jnp.int32, (D_A, LANES), 1)

    for s in range(ring):
        _page_copy(kpool_ref, pt_ref[s], buf_ref, sem_ref, s).start()

    def batch(b, carry):
        o_ref[b] = jnp.zeros((D_A, LANES), F32)

        def block(j, carry):
            total = None
            for half in range(PAGES_PER_BLOCK):
                page_no = (b * n_blk + j) * PAGES_PER_BLOCK + half
                slot = page_no % ring
                _page_copy(kpool_ref, pt_ref[page_no], buf_ref, sem_ref, slot).wait()
                x = buf_ref[slot]
                total = x if total is None else total + x

                @pl.when(page_no + ring < n_total)
                def _refill():
                    _page_copy(kpool_ref, pt_ref[page_no + ring], buf_ref, sem_ref, slot).start()

            col = jnp.sum(total, axis=1, keepdims=True) * (1.0 / MOBA_BLOCK)
            o_ref[b] = jnp.where(lane == j, col, o_ref[b])
            return carry

        return lax.fori_loop(0, n_blk, block, carry)

    lax.fori_loop(0, n_dec, batch, 0)


def _kmean_call(page_table, kpool_t):
    n_dec, n_pages = page_table.shape
    assert n_pages % PAGES_PER_BLOCK == 0 and n_pages // PAGES_PER_BLOCK <= LANES
    assert KMEAN_RING % PAGES_PER_BLOCK == 0
    out = (n_dec, D_A, LANES)
    return pl.pallas_call(
        functools.partial(_kmean_kernel, n_dec=n_dec, n_pages=n_pages),
        grid_spec=pltpu.PrefetchScalarGridSpec(
            num_scalar_prefetch=1,
            grid=(1,),
            in_specs=[pl.BlockSpec(memory_space=pl.ANY)],
            out_specs=pl.BlockSpec(out, lambda i, *_: (0, 0, 0)),
            scratch_shapes=[pltpu.VMEM((KMEAN_RING, D_A, PAGE_SIZE), F32),
                            pltpu.SemaphoreType.DMA((KMEAN_RING,))]),
        out_shape=jax.ShapeDtypeStruct(out, F32),
        compiler_params=_params(1),
        name="kmean_paged",
    )(page_table.reshape(-1), kpool_t)


def _head_rows(q_row, n_rows):
    return _mask_heads(q_row, (n_rows, D_A), HEAD_DIM)


def _mask_heads(q_row, shape, head_dim):
    row = lax.broadcasted_iota(jnp.int32, shape, 0)
    lane_head = lax.broadcasted_iota(jnp.int32, shape, 1) // head_dim
    qb = jnp.broadcast_to(q_row.astype(F32), shape)
    return jnp.where(lane_head == row, qb, 0.0).astype(q_row.dtype)


def _gate_kernel(q_ref, km_ref, o_ref, *, n_blk):
    qm = _head_rows(q_ref[0], N_HEADS_A)
    gate = jnp.dot(qm, km_ref[0].astype(BF16), preferred_element_type=F32)
    lane = lax.broadcasted_iota(jnp.int32, gate.shape, 1)
    lane_f = lane.astype(F32)
    g = jnp.where(lane < n_blk, gate, NEG)
    out = jnp.zeros(gate.shape, F32)
    for t in range(MOBA_TOPK):
        m = jnp.max(g, axis=1, keepdims=True)
        idx = jnp.min(jnp.where(g == m, lane_f, float(LANES)), axis=1, keepdims=True)
        out = jnp.where(lane == t, idx, out)
        g = jnp.where(lane_f == idx, -jnp.inf, g)
    o_ref[0] = out.astype(jnp.int32)


def _gate_call(q, kmean_t, n_blk):
    n_dec = q.shape[0]
    return pl.pallas_call(
        functools.partial(_gate_kernel, n_blk=n_blk),
        grid=(n_dec,),
        in_specs=[pl.BlockSpec((1, 1, D_A), lambda b: (b, 0, 0)),
                  pl.BlockSpec((1, D_A, LANES), lambda b: (b, 0, 0))],
        out_specs=pl.BlockSpec((1, N_HEADS_A, LANES), lambda b: (b, 0, 0)),
        out_shape=jax.ShapeDtypeStruct((n_dec, N_HEADS_A, LANES), jnp.int32),
        compiler_params=_params(1),
        name="gate_sample",
    )(q, kmean_t)


def _head_page_copy(pool_ref, page, h, buf_ref, sem_ref, which, slot):
    rows = pl.ds(h * HEAD_DIM, HEAD_DIM)
    return pltpu.make_async_copy(pool_ref.at[page, rows, :], buf_ref.at[which, slot], sem_ref.at[which, slot])


def _sample_attn_kernel(top_ref, pt_ref, q_ref, kn_ref, vn_ref, tnear_ref, bias0_ref, bfar_ref,
                        kpool_ref, vpool_ref, o_ref, buf_ref, sem_ref, *, n_blk):
    b = pl.program_id(0)
    n_sel = MOBA_TOPK * PAGES_PER_BLOCK

    def copies(h):
        out = []
        for t in range(MOBA_TOPK):
            blk = top_ref[(b * N_HEADS_A + h) * MOBA_TOPK + t]
            for half in range(PAGES_PER_BLOCK):
                page = pt_ref[b, blk * PAGES_PER_BLOCK + half]
                slot = h * n_sel + t * PAGES_PER_BLOCK + half
                out.append(_head_page_copy(kpool_ref, page, h, buf_ref, sem_ref, 0, slot))
                out.append(_head_page_copy(vpool_ref, page, h, buf_ref, sem_ref, 1, slot))
        return out

    for h in range(N_HEADS_A):
        for c in copies(h):
            c.start()

    q = q_ref[0]
    kn = kn_ref[0].astype(BF16).astype(F32)
    vn = vn_ref[0].astype(BF16).astype(F32)
    s_new_all = q.astype(F32) * kn
    outs = []
    for h in range(N_HEADS_A):
        for c in copies(h):
            c.wait()
        lanes = slice(h * HEAD_DIM, (h + 1) * HEAD_DIM)
        qh = jnp.broadcast_to(q[:, lanes], (SUBLANES, HEAD_DIM))
        kt = jnp.concatenate([buf_ref[0, h * n_sel + s] for s in range(n_sel)], axis=1).astype(BF16)
        vt = jnp.concatenate([buf_ref[1, h * n_sel + s] for s in range(n_sel)], axis=1).astype(BF16)
        bias = []
        for t in range(MOBA_TOPK):
            blk = top_ref[(b * N_HEADS_A + h) * MOBA_TOPK + t]
            bias.append(jnp.where(blk == n_blk - 1, tnear_ref[h:h + 1, :], bfar_ref[h]))
        s = jnp.dot(qh, kt, preferred_element_type=F32)[:1] + jnp.concatenate(bias, axis=1)
        s_new = jnp.sum(s_new_all[:, lanes], axis=1, keepdims=True) + bias0_ref[h]
        m = jnp.maximum(jnp.max(s, axis=1, keepdims=True), s_new)
        p = jnp.exp(s - m)
        p_new = jnp.exp(s_new - m)
        l = jnp.sum(p, axis=1, keepdims=True) + p_new
        pb = jnp.broadcast_to(p.astype(BF16), (SUBLANES, p.shape[1]))
        o = lax.dot_general(pb, vt, _NT, preferred_element_type=F32)[:1]
        o = o + p_new.astype(BF16).astype(F32) * vn[:, lanes]
        outs.append(o / l)
    o_ref[0] = jnp.concatenate(outs, axis=1)


def _sample_attn_call(top, page_table, q, kn, vn, tnear, bias0, bfar, kpool_t, vpool_t, n_blk):
    n_dec = q.shape[0]
    row = pl.BlockSpec((1, 1, D_A), lambda b, *_: (b, 0, 0))
    smem = pl.BlockSpec(memory_space=pltpu.SMEM)
    any_ = pl.BlockSpec(memory_space=pl.ANY)
    n_slots = N_HEADS_A * MOBA_TOPK * PAGES_PER_BLOCK
    return pl.pallas_call(
        functools.partial(_sample_attn_kernel, n_blk=n_blk),
        grid_spec=pltpu.PrefetchScalarGridSpec(
            num_scalar_prefetch=2,
            grid=(n_dec,),
            in_specs=[row, row, row, pl.BlockSpec(tnear.shape, lambda b, *_: (0, 0)), smem, smem, any_, any_],
            out_specs=row,
            scratch_shapes=[pltpu.VMEM((2, n_slots, HEAD_DIM, PAGE_SIZE), F32),
                            pltpu.SemaphoreType.DMA((2, n_slots))]),
        out_shape=jax.ShapeDtypeStruct((n_dec, 1, D_A), F32),
        compiler_params=_params(1),
        name="attn_sample",
    )(top, page_table, q, kn, vn, tnear, bias0, bfar, kpool_t, vpool_t)


def _mix_sample_kernel(x1_ref, a_ref, u_ref, st_ref, pw_ref, ps_ref, wout_ref, g1_ref, b1_ref, wxq_ref,
                       x2_ref, qx_ref, *, alpha, pos):
    u = u_ref[...]

    def window_sum(g, w):
        lanes = slice(g * POOL_GROUP, (g + 1) * POOL_GROUP)
        s = u[:, lanes]
        for back in range(1, w):
            s = s + st_ref[POOL_STATE - back, :, lanes]
        return s

    p = _pool_project(window_sum, u, lambda w: float(min(w, pos + 1)), pw_ref, ps_ref)
    x2 = _out_project(x1_ref[...], a_ref[...].astype(BF16), p, wout_ref, g1_ref, b1_ref, alpha)
    x2_ref[...] = x2
    qx = jnp.dot(x2.astype(BF16), wxq_ref[...], preferred_element_type=F32) * HEAD_DIM_X ** -0.5
    qx_ref[...] = qx.astype(BF16)


def _mix_sample_call(x1, a, u, state_t, pw, ps, wout, g1, b1, wxq, *, alpha, pos):
    n = x1.shape[0]
    args = (x1, a, u, state_t, pw, ps, wout, g1, b1, wxq)
    return pl.pallas_call(
        functools.partial(_mix_sample_kernel, alpha=alpha, pos=pos),
        grid=(1,),
        in_specs=[_resident(v.shape) for v in args],
        out_specs=[_resident((n, D_MODEL)), _resident((n, D_MODEL))],
        out_shape=[jax.ShapeDtypeStruct((n, D_MODEL), F32), jax.ShapeDtypeStruct((n, D_MODEL), BF16)],
        compiler_params=_params(1),
        name="mix_sample",
    )(*args)


def _xattn_sample_kernel(qx_ref, mk_ref, mv_ref, o_ref):
    qm = _head_rows_x(qx_ref[0])
    logits = lax.dot_general(qm, mk_ref[0].astype(BF16), _NT, preferred_element_type=F32)
    e = jnp.exp(logits - jnp.max(logits, axis=-1, keepdims=True))
    o = jnp.dot(e.astype(BF16), mv_ref[0].astype(BF16), preferred_element_type=F32)
    o = o / jnp.sum(e, axis=-1, keepdims=True)
    row = lax.broadcasted_iota(jnp.int32, o.shape, 0)
    lane_head = lax.broadcasted_iota(jnp.int32, o.shape, 1) // HEAD_DIM_X
    o_ref[0] = jnp.sum(jnp.where(lane_head == row, o, 0.0), axis=0, keepdims=True)


def _head_rows_x(q_row):
    return _mask_heads(q_row, (SUBLANES, D_MODEL), HEAD_DIM_X)


def _xattn_sample_call(qx, mk, mv):
    n_dec, n_mem, _ = mk.shape
    row = pl.BlockSpec((1, 1, D_MODEL), lambda b: (b, 0, 0))
    mem = pl.BlockSpec((1, n_mem, D_MODEL), lambda b: (b, 0, 0))
    return pl.pallas_call(
        _xattn_sample_kernel,
        grid=(n_dec,),
        in_specs=[row, mem, mem],
        out_specs=row,
        out_shape=jax.ShapeDtypeStruct((n_dec, 1, D_MODEL), F32),
        compiler_params=_params(1),
        name="xattn_sample",
    )(qx, mk, mv)


def _proj_ln_kernel(x_ref, o_ref, w_ref, g_ref, b_ref, y_ref, *, alpha):
    proj = jnp.dot(o_ref[...].astype(BF16), w_ref[...], preferred_element_type=F32)
    y_ref[...] = _layer_norm(alpha * x_ref[...] + proj, g_ref[...], b_ref[...])


def _proj_ln_call(x, o, w, g, b, *, alpha):
    args = (x, o, w, g, b)
    return pl.pallas_call(
        functools.partial(_proj_ln_kernel, alpha=alpha),
        grid=(1,),
        in_specs=[_resident(v.shape) for v in args],
        out_specs=_resident(x.shape),
        out_shape=jax.ShapeDtypeStruct(x.shape, F32),
        compiler_params=_params(1),
        name="proj_ln",
    )(*args)


def _t5_bucket(dist):
    n = jnp.maximum(dist, 0)
    max_exact = N_BUCKETS // 2
    nf = jnp.maximum(n, 1).astype(F32)
    large = max_exact + (jnp.log(nf / max_exact) / math.log(MAX_DISTANCE / max_exact)
                         * (N_BUCKETS - max_exact)).astype(jnp.int32)
    return jnp.where(n < max_exact, n, jnp.minimum(large, N_BUCKETS - 1))


def kernel(x_prompt, x_sample, cache_k, cache_v, cache_mem_k, cache_mem_v, state_pool, page_table, mem_prompt, rel_bias, ln_g, ln_b, w_ff1_gate, w_ff1_up, w_ff1_down, w_in, pool_w, pool_scale, w_out, w_xq, w_xk, w_xv, w_xo, w_ff2_gate, w_ff2_up, w_ff2_down):
    n_batch, seq, _ = x_prompt.shape
    n_dec, dec_seq, _ = x_sample.shape
    depth = ln_g.shape[0]
    n_pool = cache_k.shape[1]
    n_pages = page_table.shape[1]
    past_len = n_pages * PAGE_SIZE
    n_mem = mem_prompt.shape[1]
    assert depth == 1 and dec_seq == 1
    assert seq % MOBA_BLOCK == 0 and past_len % MOBA_BLOCK == 0
    assert past_len // MOBA_BLOCK >= MOBA_TOPK
    alpha = (2 * depth) ** 0.25
    tm = min(512, seq)
    n_blk = seq // MOBA_BLOCK

    bf = lambda w: w[0].astype(BF16)
    ff1 = (bf(w_ff1_gate), bf(w_ff1_up), bf(w_ff1_down))
    ff2 = (bf(w_ff2_gate), bf(w_ff2_up), bf(w_ff2_down))
    win = bf(w_in)
    ln = lambda i: (ln_g[0, i:i + 1], ln_b[0, i:i + 1])
    (g0, b0), (g1, b1), (g2, b2), (g3, b3) = ln(0), ln(1), ln(2), ln(3)
    pw, ps = pool_w[0].astype(BF16), pool_scale
    wout, wxq, wxo = bf(w_out), bf(w_xq), bf(w_xo)

    bias_by_dist = rel_bias[_t5_bucket(jnp.arange(2 * MOBA_BLOCK, dtype=jnp.int32))].T
    bfar = rel_bias[_far_bucket()]

    xp = x_prompt.reshape(n_batch * seq, D_MODEL)
    x1, q, kt, vt, u, kb, vtb, km = _ffn_proj_call(xp, *ff1, g0, b0, win, alpha=alpha, tm=tm, moba_batch=n_batch)
    a = _moba_call(rel_bias.reshape(-1), q.reshape(n_batch, seq, D_A), kb.reshape(n_batch, seq, D_A), vtb,
                   km.reshape(n_batch, n_blk, D_A))
    mk, mv, mkb, mvb = _memkv_call(mem_prompt.reshape(n_batch * n_mem, D_MODEL), bf(w_xk), bf(w_xv),
                                   tm=min(512, n_batch * n_mem))
    x3 = _mix_call(x1, a.reshape(n_batch * seq, D_A), u, pw, ps, wout, g1, b1, wxq, wxo,
                   mkb.reshape(n_batch, n_mem, D_MODEL), mvb.reshape(n_batch, n_mem, D_MODEL), g2, b2,
                   alpha=alpha, n_batch=n_batch, tm=tm)
    y_prompt = _ffn_call(x3, *ff2, g3, b3, alpha=alpha, tm=tm).reshape(n_batch, seq, D_MODEL)

    to_heads = lambda t: t.reshape(t.shape[0], N_HEADS_A, HEAD_DIM, t.shape[2]).transpose(0, 3, 1, 2)[None]
    k_prompt, v_prompt = to_heads(kt), to_heads(vt)
    pool_prompt = u.reshape(n_batch, seq, D_POOL)[None, :, seq - POOL_STATE:]
    memk_prompt = mk.reshape(1, n_batch, n_mem, N_HEADS_X, HEAD_DIM_X)
    memv_prompt = mv.reshape(1, n_batch, n_mem, N_HEADS_X, HEAD_DIM_X)

    xs = x_sample.reshape(n_dec, D_MODEL)
    x1s, qs, kn, vn, us = _ffn_proj_call(xs, *ff1, g0, b0, win, alpha=alpha, tm=n_dec)
    kpool_t = cache_k[0].transpose(0, 2, 3, 1).reshape(n_pool, D_A, PAGE_SIZE)
    vpool_t = cache_v[0].transpose(0, 2, 3, 1).reshape(n_pool, D_A, PAGE_SIZE)
    n_blk_s = past_len // MOBA_BLOCK
    kmean_t = _kmean_call(page_table, kpool_t)
    qs3 = qs.reshape(n_dec, 1, D_A)
    top = _gate_call(qs3, kmean_t, n_blk_s)[:, :, :MOBA_TOPK].reshape(-1)
    tnear = bias_by_dist[:, MOBA_BLOCK - jnp.arange(MOBA_BLOCK)]
    a_s = _sample_attn_call(top, page_table, qs3, kn.reshape(n_dec, 1, D_A), vn.reshape(n_dec, 1, D_A),
                            tnear, bias_by_dist[:, 0], bfar, kpool_t, vpool_t, n_blk_s)
    state_t = state_pool[0].transpose(1, 0, 2)
    x2s, qxs = _mix_sample_call(x1s, a_s.reshape(n_dec, D_A), us, state_t, pw, ps, wout, g1, b1, wxq,
                                alpha=alpha, pos=past_len)
    o_s = _xattn_sample_call(qxs.reshape(n_dec, 1, D_MODEL),
                             cache_mem_k[0].reshape(n_dec, n_mem, D_MODEL),
                             cache_mem_v[0].reshape(n_dec, n_mem, D_MODEL))
    x3s = _proj_ln_call(x2s, o_s.reshape(n_dec, D_MODEL), wxo, g2, b2, alpha=alpha)
    y_sample = _ffn_call(x3s, *ff2, g3, b3, alpha=alpha, tm=n_dec).reshape(n_dec, 1, D_MODEL)

    k_sample = kn.reshape(1, n_dec, 1, N_HEADS_A, HEAD_DIM)
    v_sample = vn.reshape(1, n_dec, 1, N_HEADS_A, HEAD_DIM)
    pool_sample = jnp.concatenate([state_t[1:], us[None]], axis=0).transpose(1, 0, 2)[None]
    return (y_prompt, y_sample, k_prompt, v_prompt, pool_prompt, memk_prompt, memv_prompt,
            k_sample, v_sample, pool_sample)
```

```python
import functools
import math

import jax
import jax.numpy as jnp
import numpy as np
from jax import lax
from jax.experimental import pallas as pl
from jax.experimental.pallas import tpu as pltpu

F32 = jnp.float32
BF16 = jnp.bfloat16

D_MODEL = 1024
HEAD_DIM = 64
N_HEADS_A = 8
D_A = N_HEADS_A * HEAD_DIM
MOBA_BLOCK = 256
MOBA_TOPK = 3
D_POOL = D_MODEL - D_A
POOL_WINDOWS = (2, 4, 8, 16)
POOL_GROUP = D_POOL // len(POOL_WINDOWS)
POOL_STATE = max(POOL_WINDOWS) - 1
N_BUCKETS = 32
MAX_DISTANCE = 128
N_HEADS_X = 4
HEAD_DIM_X = D_MODEL // N_HEADS_X
D_FF = 2816
LN_EPS = 1e-5
PAGE_SIZE = 128
NEG = -1e30

LANES = 128
SUBLANES = 8
VMEM_LIMIT_BYTES = 56 * 1024 * 1024

BF16_SUBLANES = 16
HEADS_PER_TILE = LANES // HEAD_DIM
V_ROWS = HEAD_DIM + BF16_SUBLANES
LOG2E = math.log2(math.e)
POOL_HALO = POOL_STATE + 1
PAGES_PER_BLOCK = MOBA_BLOCK // PAGE_SIZE
FFN_CHUNK = 256
KMEAN_RING = 32
SCORE_CHUNK = 32

_NT = (((1,), (1,)), ((), ()))


def _params(n_grid_dims):
    return pltpu.CompilerParams(
        dimension_semantics=("arbitrary",) * n_grid_dims, vmem_limit_bytes=VMEM_LIMIT_BYTES)


def _resident(shape):
    return pl.BlockSpec(shape, lambda *_: (0,) * len(shape), pipeline_mode=pl.Buffered(1))


def _layer_norm(z, g, b):
    mu = jnp.mean(z, axis=-1, keepdims=True)
    zc = z - mu
    var = jnp.mean(zc * zc, axis=-1, keepdims=True)
    return zc * lax.rsqrt(var + LN_EPS) * g + b


def _swiglu(x, wg_ref, wu_ref, wd_ref):
    xb = x.astype(BF16)
    acc = None
    for c in range(D_FF // FFN_CHUNK):
        sl = slice(c * FFN_CHUNK, (c + 1) * FFN_CHUNK)
        g = jnp.dot(xb, wg_ref[:, sl], preferred_element_type=F32)
        u = jnp.dot(xb, wu_ref[:, sl], preferred_element_type=F32)
        h = (g * jax.nn.sigmoid(g) * u).astype(BF16)
        part = jnp.dot(h, wd_ref[sl, :], preferred_element_type=F32)
        acc = part if acc is None else acc + part
    return acc


def _ffn_kernel(x_ref, wg_ref, wu_ref, wd_ref, g_ref, b_ref, y_ref, *, alpha):
    x = x_ref[...]
    y_ref[...] = _layer_norm(alpha * x + 0.5 * _swiglu(x, wg_ref, wu_ref, wd_ref), g_ref[...], b_ref[...])


def _ffn_proj_kernel(x_ref, wg_ref, wu_ref, wd_ref, g_ref, b_ref, win_ref,
                     x1_ref, q_ref, k_ref, v_ref, u_ref, *moba_refs, alpha, q_scale):
    x = x_ref[...]
    x1 = _layer_norm(alpha * x + 0.5 * _swiglu(x, wg_ref, wu_ref, wd_ref), g_ref[...], b_ref[...])
    x1_ref[...] = x1
    h = jnp.dot(x1.astype(BF16), win_ref[...], preferred_element_type=F32)
    q_ref[...] = (h[:, :D_A] * q_scale).astype(BF16)
    k = h[:, D_A:2 * D_A]
    v = h[:, 2 * D_A:3 * D_A]
    u_ref[...] = h[:, 3 * D_A:]
    if not moba_refs:
        k_ref[...] = k
        v_ref[...] = v
        return
    kb_ref, vtb_ref, km_ref = moba_refs
    vt = v.T
    k_ref[0] = k.T
    v_ref[0] = vt
    kb_ref[...] = k.astype(BF16)
    ones = jnp.ones((BF16_SUBLANES, MOBA_BLOCK), BF16)
    for blk in range(k.shape[0] // MOBA_BLOCK):
        rows = slice(blk * MOBA_BLOCK, (blk + 1) * MOBA_BLOCK)
        for hd in range(N_HEADS_A):
            vtb_ref[0, blk, hd * V_ROWS:hd * V_ROWS + HEAD_DIM, :] = (
                vt[hd * HEAD_DIM:(hd + 1) * HEAD_DIM, rows].astype(BF16))
            vtb_ref[0, blk, hd * V_ROWS + HEAD_DIM:(hd + 1) * V_ROWS, :] = ones
        km_ref[blk] = jnp.mean(k[rows], axis=0, keepdims=True)


def _ffn_call(x, wg, wu, wd, g, b, *, alpha, tm):
    n = x.shape[0]
    row = pl.BlockSpec((tm, D_MODEL), lambda i: (i, 0))
    return pl.pallas_call(
        functools.partial(_ffn_kernel, alpha=alpha),
        grid=(n // tm,),
        in_specs=[row, _resident(wg.shape), _resident(wu.shape), _resident(wd.shape),
                  _resident(g.shape), _resident(b.shape)],
        out_specs=row,
        out_shape=jax.ShapeDtypeStruct((n, D_MODEL), F32),
        compiler_params=_params(1),
        name="ffn",
    )(x, wg, wu, wd, g, b)


def _ffn_proj_call(x, wg, wu, wd, g, b, win, *, alpha, q_scale, tm, moba_batch=None):
    n = x.shape[0]
    row = lambda w: pl.BlockSpec((tm, w), lambda i: (i, 0))
    sds = jax.ShapeDtypeStruct
    out_specs = [row(D_MODEL), row(D_A)]
    out_shape = [sds((n, D_MODEL), F32), sds((n, D_A), BF16)]
    if moba_batch is None:
        out_specs += [row(D_A), row(D_A), row(D_POOL)]
        out_shape += [sds((n, D_A), F32), sds((n, D_A), F32), sds((n, D_POOL), F32)]
    else:
        seq = n // moba_batch
        tpb = seq // tm
        bpt = tm // MOBA_BLOCK
        tcol = pl.BlockSpec((1, D_A, tm), lambda i: (i // tpb, 0, i % tpb))
        out_specs += [tcol, tcol, row(D_POOL), row(D_A),
                      pl.BlockSpec((1, bpt, N_HEADS_A * V_ROWS, MOBA_BLOCK), lambda i: (i // tpb, i % tpb, 0, 0)),
                      pl.BlockSpec((bpt, 1, D_A), lambda i: (i, 0, 0))]
        out_shape += [sds((moba_batch, D_A, seq), F32), sds((moba_batch, D_A, seq), F32), sds((n, D_POOL), F32),
                      sds((n, D_A), BF16),
                      sds((moba_batch, seq // MOBA_BLOCK, N_HEADS_A * V_ROWS, MOBA_BLOCK), BF16),
                      sds((n // MOBA_BLOCK, 1, D_A), F32)]
    return pl.pallas_call(
        functools.partial(_ffn_proj_kernel, alpha=alpha, q_scale=q_scale),
        grid=(n // tm,),
        in_specs=[row(D_MODEL), _resident(wg.shape), _resident(wu.shape), _resident(wd.shape),
                  _resident(g.shape), _resident(b.shape), _resident(win.shape)],
        out_specs=out_specs,
        out_shape=out_shape,
        compiler_params=_params(1),
        name="ffn_proj",
    )(x, wg, wu, wd, g, b, win)


def _memkv_kernel(m_ref, wk_ref, wv_ref, k_ref, v_ref, kb_ref, vb_ref):
    mb = m_ref[...].astype(BF16)
    k = jnp.dot(mb, wk_ref[...], preferred_element_type=F32)
    v = jnp.dot(mb, wv_ref[...], preferred_element_type=F32)
    k_ref[...] = k
    v_ref[...] = v
    kb_ref[...] = k.astype(BF16)
    vb_ref[...] = v.astype(BF16)


def _memkv_call(mem, wk, wv, *, tm):
    n = mem.shape[0]
    row = pl.BlockSpec((tm, D_MODEL), lambda i: (i, 0))
    sds = lambda dt: jax.ShapeDtypeStruct((n, D_MODEL), dt)
    return pl.pallas_call(
        _memkv_kernel,
        grid=(n // tm,),
        in_specs=[row, _resident(wk.shape), _resident(wv.shape)],
        out_specs=[row, row, row, row],
        out_shape=[sds(F32), sds(F32), sds(BF16), sds(BF16)],
        compiler_params=_params(1),
        name="memkv",
    )(mem, wk, wv)


def _top3_rows(gate, n_valid, n_rows):
    rid = lax.broadcasted_iota(jnp.int32, gate.shape, 0).astype(F32)
    g = jnp.where(rid < n_valid, gate, NEG)
    sel = jnp.zeros(gate.shape, jnp.bool_)
    for _ in range(MOBA_TOPK):
        m = jnp.max(g, axis=0, keepdims=True)
        idx = jnp.min(jnp.where(g == m, rid, float(n_rows)), axis=0, keepdims=True)
        pick = rid == idx
        sel = jnp.logical_or(sel, jnp.logical_and(pick, idx < n_valid))
        g = jnp.where(pick, -jnp.inf, g)
    return sel


def _moba_kernel(rb_ref, q_ref, kb_ref, vtb_ref, km_ref, bkt_ref, o_ref,
                 tbl_ref, rbt_ref, qh_ref, s_ref, p_ref, acc_ref, m_ref, smx_ref, al_ref, *, n_blk, far_bucket):
    b = pl.program_id(0)
    i = pl.program_id(1)
    row_zero, row_prev = n_blk, n_blk + 1
    n_chunk = MOBA_BLOCK // SCORE_CHUNK
    chunks = [slice(c * SCORE_CHUNK, (c + 1) * SCORE_CHUNK) for c in range(n_chunk)]
    head_lanes = lambda h: slice((h // HEADS_PER_TILE) * LANES, (h // HEADS_PER_TILE + 1) * LANES)

    @pl.when(jnp.logical_and(b == 0, i == 0))
    def _build_bias_tiles():
        def per_head(h, _):
            for slot in range(2):
                bk = bkt_ref[slot]
                t = jnp.zeros(bk.shape, F32)
                for bucket in range(N_BUCKETS):
                    t = jnp.where(bk == bucket, rb_ref[bucket * N_HEADS_A + h], t)
                tbl_ref[h, slot] = jnp.where(bk < 0, NEG, t)
            return 0
        lax.fori_loop(0, N_HEADS_A, per_head, 0)

    q = q_ref[0]
    km = km_ref[0]
    i_f = i.astype(F32)
    jp_f = jnp.maximum(i - 1, 0).astype(F32)
    lane_head = lax.broadcasted_iota(jnp.int32, (MOBA_BLOCK, LANES), 1) // HEAD_DIM
    rid = lax.broadcasted_iota(jnp.int32, (n_blk, MOBA_BLOCK), 0).astype(F32)
    for h in range(N_HEADS_A):
        q2 = q[:, head_lanes(h)].astype(F32)
        qh = jnp.where(lane_head == h % HEADS_PER_TILE, q2, 0.0).astype(BF16)
        qh_ref[h] = qh
        gate = lax.dot_general(km[:, head_lanes(h)].astype(BF16), qh, _NT, preferred_element_type=F32)
        sel = _top3_rows(gate, i_f, n_blk)
        rbt_ref[h, :n_blk] = jnp.where(sel, rb_ref[far_bucket * N_HEADS_A + h], NEG)
        prev_sel = jnp.max(jnp.where(jnp.logical_and(sel, rid == jp_f), 1.0, 0.0), axis=0, keepdims=True)
        rbt_ref[h, row_zero:row_zero + 1] = jnp.zeros((1, MOBA_BLOCK), F32)
        rbt_ref[h, row_prev:row_prev + 1] = jnp.where(prev_sel > 0.5, 0.0, NEG)
        m_ref[h] = jnp.full((1, MOBA_BLOCK), NEG, F32)
        al_ref[h] = jnp.ones((1, MOBA_BLOCK), F32)
        acc_ref[h] = jnp.zeros((V_ROWS, MOBA_BLOCK), F32)
        p_ref[h] = jnp.zeros((MOBA_BLOCK, MOBA_BLOCK), BF16)

    def list_block(t):
        return jnp.clip(jnp.where(t == 0, i, jnp.where(t == 1, i - 1, t - 2)), 0, n_blk - 1)

    def stage_a(t):
        rows = pl.ds(pl.multiple_of(list_block(t) * MOBA_BLOCK, MOBA_BLOCK), MOBA_BLOCK)
        for h in range(N_HEADS_A):
            s = lax.dot_general(kb_ref[0, rows, head_lanes(h)], qh_ref[h], _NT, preferred_element_type=F32)
            s_ref[h] = s
            smx_ref[h] = jnp.max(s, axis=0, keepdims=True)

    def add_bias_tile(slot):
        for h in range(N_HEADS_A):
            s = s_ref[h] + tbl_ref[h, slot]
            s_ref[h] = s
            smx_ref[h] = jnp.max(s, axis=0, keepdims=True)

    def stage_b(t):
        row = jnp.where(t == 0, row_zero, jnp.where(t == 1, row_prev, t - 2))
        for h in range(N_HEADS_A):
            rb = rbt_ref[h, pl.ds(row, 1), :]
            m_old = m_ref[h]
            m_new = jnp.maximum(m_old, smx_ref[h] + rb)
            shift = m_new - rb
            for c in chunks:
                p_ref[h, c, :] = jnp.exp2(s_ref[h, c, :] - shift).astype(BF16)
            m_ref[h] = m_new
            al_ref[h] = jnp.exp2(m_old - m_new)

    def stage_c(t):
        j = list_block(t)
        for h in range(N_HEADS_A):
            vt = vtb_ref[0, j, h * V_ROWS:(h + 1) * V_ROWS, :]
            acc_ref[h] = al_ref[h] * acc_ref[h] + jnp.dot(vt, p_ref[h], preferred_element_type=F32)

    stage_a(0)
    add_bias_tile(0)

    def step(t, carry):
        stage_c(jnp.maximum(t - 2, 0))
        stage_b(t - 1)
        stage_a(t)

        @pl.when(t == 1)
        def _():
            add_bias_tile(1)

        return carry

    lax.fori_loop(1, i + 2, step, 0)
    stage_c(i)
    out_t = jnp.concatenate([acc_ref[h, :HEAD_DIM] / acc_ref[h, HEAD_DIM:HEAD_DIM + 1]
                             for h in range(N_HEADS_A)], axis=0)
    o_ref[0] = out_t.T.astype(BF16)


def _bucket_np(dist):
    n = np.maximum(dist, 0)
    max_exact = N_BUCKETS // 2
    nf = np.maximum(n, 1).astype(np.float32)
    large = max_exact + (np.log(nf / np.float32(max_exact)) / np.float32(math.log(MAX_DISTANCE / max_exact))
                         * np.float32(N_BUCKETS - max_exact)).astype(np.int32)
    return np.where(n < max_exact, n, np.minimum(large, N_BUCKETS - 1)).astype(np.int32)


def _bucket_tiles():
    key = np.arange(MOBA_BLOCK)[:, None]
    qry = np.arange(MOBA_BLOCK)[None, :]
    own = np.where(qry >= key, _bucket_np(qry - key), -1)
    prev = _bucket_np(qry - key + MOBA_BLOCK)
    return np.stack([own, prev]).astype(np.int32)


def _far_bucket():
    far = _bucket_np(np.array([MOBA_BLOCK + 1, 1 << 30]))
    assert far[0] == far[1]
    return int(far[0])


def _moba_call(rel_bias_flat, q, kb, vtb, km):
    n_batch, seq, _ = q.shape
    n_blk = seq // MOBA_BLOCK
    qo = pl.BlockSpec((1, MOBA_BLOCK, D_A), lambda b, i, *_: (b, i, 0))
    per_batch = lambda shape: pl.BlockSpec((1,) + shape, lambda b, i, *_: (b,) + (0,) * len(shape))
    tile = (MOBA_BLOCK, MOBA_BLOCK)
    stat = pltpu.VMEM((N_HEADS_A, 1, MOBA_BLOCK), F32)
    return pl.pallas_call(
        functools.partial(_moba_kernel, n_blk=n_blk, far_bucket=_far_bucket()),
        grid_spec=pltpu.PrefetchScalarGridSpec(
            num_scalar_prefetch=1,
            grid=(n_batch, n_blk),
            in_specs=[qo, per_batch((seq, D_A)), per_batch((n_blk, N_HEADS_A * V_ROWS, MOBA_BLOCK)),
                      per_batch((n_blk, D_A)),
                      _resident((2,) + tile)],
            out_specs=qo,
            scratch_shapes=[pltpu.VMEM((N_HEADS_A, 2) + tile, F32),
                            pltpu.VMEM((N_HEADS_A, n_blk + SUBLANES, MOBA_BLOCK), F32),
                            pltpu.VMEM((N_HEADS_A, MOBA_BLOCK, LANES), BF16),
                            pltpu.VMEM((N_HEADS_A,) + tile, F32),
                            pltpu.VMEM((N_HEADS_A,) + tile, BF16),
                            pltpu.VMEM((N_HEADS_A, V_ROWS, MOBA_BLOCK), F32),
                            stat, stat, stat]),
        out_shape=jax.ShapeDtypeStruct((n_batch, seq, D_A), BF16),
        compiler_params=_params(2),
        name="moba_prompt",
    )(rel_bias_flat, q, kb, vtb, km, jnp.asarray(_bucket_tiles()))


def _pool_project(window_sum, u_new, cnt, pw_ref, ps_ref):
    ys = []
    for g, w in enumerate(POOL_WINDOWS):
        lanes = slice(g * POOL_GROUP, (g + 1) * POOL_GROUP)
        d = window_sum(g, w) / cnt(w) - u_new[:, lanes]
        ys.append(jnp.dot(d.astype(BF16), pw_ref[g], preferred_element_type=F32))
    return jnp.concatenate(ys, axis=1) * ps_ref[...]


def _out_project(x1, a, p, wout_ref, g_ref, b_ref, alpha):
    proj = (jnp.dot(a, wout_ref[:D_A, :], preferred_element_type=F32)
            + jnp.dot(p.astype(BF16), wout_ref[D_A:, :], preferred_element_type=F32))
    return _layer_norm(alpha * x1 + proj, g_ref[...], b_ref[...])


def _mix_kernel(x1_ref, a_ref, u_ref, uh_ref, pw_ref, ps_ref, wout_ref, g1_ref, b1_ref,
                wxq_ref, wxo_ref, mk_ref, mv_ref, g2_ref, b2_ref, x3_ref, ext_ref, *, alpha, tm, tpb):
    t_in_b = pl.program_id(0) % tpb
    ext_ref[:POOL_HALO, :] = jnp.where(t_in_b == 0, 0.0, uh_ref[...])
    ext_ref[POOL_HALO:, :] = u_ref[...]
    pos = t_in_b * tm + lax.broadcasted_iota(jnp.int32, (tm, 1), 0)

    def window_sum(g, w):
        lanes = slice(g * POOL_GROUP, (g + 1) * POOL_GROUP)
        s = ext_ref[POOL_HALO:POOL_HALO + tm, lanes]
        for back in range(1, w):
            s = s + ext_ref[POOL_HALO - back:POOL_HALO - back + tm, lanes]
        return s

    p = _pool_project(window_sum, u_ref[...], lambda w: jnp.minimum(w, pos + 1).astype(F32), pw_ref, ps_ref)
    x2 = _out_project(x1_ref[...], a_ref[...], p, wout_ref, g1_ref, b1_ref, alpha)

    qx = jnp.dot(x2.astype(BF16), wxq_ref[...], preferred_element_type=F32) * HEAD_DIM_X ** -0.5
    outs = []
    for h in range(N_HEADS_X):
        lanes = slice(h * HEAD_DIM_X, (h + 1) * HEAD_DIM_X)
        logits = lax.dot_general(qx[:, lanes].astype(BF16), mk_ref[0, :, lanes], _NT, preferred_element_type=F32)
        e = jnp.exp(logits - jnp.max(logits, axis=-1, keepdims=True))
        o = jnp.dot(e.astype(BF16), mv_ref[0, :, lanes], preferred_element_type=F32)
        outs.append(o / jnp.sum(e, axis=-1, keepdims=True))
    o = jnp.concatenate(outs, axis=1).astype(BF16)
    x3_ref[...] = _layer_norm(alpha * x2 + jnp.dot(o, wxo_ref[...], preferred_element_type=F32),
                              g2_ref[...], b2_ref[...])


def _mix_call(x1, a, u, pw, ps, wout, g1, b1, wxq, wxo, mk, mv, g2, b2, *, alpha, n_batch, tm):
    n = x1.shape[0]
    tpb = n // n_batch // tm
    n_mem = mk.shape[1]
    row = lambda w: pl.BlockSpec((tm, w), lambda i: (i, 0))
    halo = pl.BlockSpec((POOL_HALO, D_POOL), lambda i: (jnp.maximum(i * (tm // POOL_HALO) - 1, 0), 0))
    mem = pl.BlockSpec((1, n_mem, D_MODEL), lambda i: (i // tpb, 0, 0))
    return pl.pallas_call(
        functools.partial(_mix_kernel, alpha=alpha, tm=tm, tpb=tpb),
        grid=(n // tm,),
        in_specs=[row(D_MODEL), row(D_A), row(D_POOL), halo, _resident(pw.shape), _resident(ps.shape),
                  _resident(wout.shape), _resident(g1.shape), _resident(b1.shape),
                  _resident(wxq.shape), _resident(wxo.shape), mem, mem,
                  _resident(g2.shape), _resident(b2.shape)],
        out_specs=row(D_MODEL),
        out_shape=jax.ShapeDtypeStruct((n, D_MODEL), F32),
        scratch_shapes=[pltpu.VMEM((tm + POOL_HALO, D_POOL), F32)],
        compiler_params=_params(1),
        name="mix_prompt",
    )(x1, a, u, u, pw, ps, wout, g1, b1, wxq, wxo, mk, mv, g2, b2)


def _page_copy(pool_ref, page, buf_ref, sem_ref, slot):
    return pltpu.make_async_copy(pool_ref.at[page], buf_ref.at[slot], sem_ref.at[slot])


def _kmean_kernel(pt_ref, kpool_ref, o_ref, buf_ref, sem_ref, *, n_dec, n_pages):
    n_blk = n_pages // PAGES_PER_BLOCK
    n_total = n_dec * n_pages
    ring = min(KMEAN_RING, n_total)
    lane = lax.broadcasted_iota(jnp.int32, (D_A, LANES), 1)

    for s in range(ring):
        _page_copy(kpool_ref, pt_ref[s], buf_ref, sem_ref, s).start()

    def batch(b, carry):
        o_ref[b] = jnp.zeros((D_A, LANES), F32)

        def block(j, carry):
            total = None
            for half in range(PAGES_PER_BLOCK):
                page_no = (b * n_blk + j) * PAGES_PER_BLOCK + half
                slot = page_no % ring
                _page_copy(kpool_ref, pt_ref[page_no], buf_ref, sem_ref, slot).wait()
                x = buf_ref[slot]
                total = x if total is None else total + x

                @pl.when(page_no + ring < n_total)
                def _refill():
                    _page_copy(kpool_ref, pt_ref[page_no + ring], buf_ref, sem_ref, slot).start()

            col = jnp.sum(total, axis=1, keepdims=True) * (1.0 / MOBA_BLOCK)
            o_ref[b] = jnp.where(lane == j, col, o_ref[b])
            return carry

        return lax.fori_loop(0, n_blk, block, carry)

    lax.fori_loop(0, n_dec, batch, 0)


def _kmean_call(page_table, kpool_t):
    n_dec, n_pages = page_table.shape
    assert n_pages % PAGES_PER_BLOCK == 0 and n_pages // PAGES_PER_BLOCK <= LANES
    assert KMEAN_RING % PAGES_PER_BLOCK == 0
    out = (n_dec, D_A, LANES)
    return pl.pallas_call(
        functools.partial(_kmean_kernel, n_dec=n_dec, n_pages=n_pages),
        grid_spec=pltpu.PrefetchScalarGridSpec(
            num_scalar_prefetch=1,
            grid=(1,),
            in_specs=[pl.BlockSpec(memory_space=pl.ANY)],
            out_specs=pl.BlockSpec(out, lambda i, *_: (0, 0, 0)),
            scratch_shapes=[pltpu.VMEM((KMEAN_RING, D_A, PAGE_SIZE), F32),
                            pltpu.SemaphoreType.DMA((KMEAN_RING,))]),
        out_shape=jax.ShapeDtypeStruct(out, F32),
        compiler_params=_params(1),
        name="kmean_paged",
    )(page_table.reshape(-1), kpool_t)


def _head_rows(q_row, n_rows):
    return _mask_heads(q_row, (n_rows, D_A), HEAD_DIM)


def _mask_heads(q_row, shape, head_dim):
    row = lax.broadcasted_iota(jnp.int32, shape, 0)
    lane_head = lax.broadcasted_iota(jnp.int32, shape, 1) // head_dim
    qb = jnp.broadcast_to(q_row.astype(F32), shape)
    return jnp.where(lane_head == row, qb, 0.0).astype(q_row.dtype)


def _gate_kernel(q_ref, km_ref, o_ref, *, n_blk):
    qm = _head_rows(q_ref[0], N_HEADS_A)
    gate = jnp.dot(qm, km_ref[0].astype(BF16), preferred_element_type=F32)
    lane = lax.broadcasted_iota(jnp.int32, gate.shape, 1)
    lane_f = lane.astype(F32)
    g = jnp.where(lane < n_blk, gate, NEG)
    out = jnp.zeros(gate.shape, F32)
    for t in range(MOBA_TOPK):
        m = jnp.max(g, axis=1, keepdims=True)
        idx = jnp.min(jnp.where(g == m, lane_f, float(LANES)), axis=1, keepdims=True)
        out = jnp.where(lane == t, idx, out)
        g = jnp.where(lane_f == idx, -jnp.inf, g)
    o_ref[0] = out.astype(jnp.int32)


def _gate_call(q, kmean_t, n_blk):
    n_dec = q.shape[0]
    return pl.pallas_call(
        functools.partial(_gate_kernel, n_blk=n_blk),
        grid=(n_dec,),
        in_specs=[pl.BlockSpec((1, 1, D_A), lambda b: (b, 0, 0)),
                  pl.BlockSpec((1, D_A, LANES), lambda b: (b, 0, 0))],
        out_specs=pl.BlockSpec((1, N_HEADS_A, LANES), lambda b: (b, 0, 0)),
        out_shape=jax.ShapeDtypeStruct((n_dec, N_HEADS_A, LANES), jnp.int32),
        compiler_params=_params(1),
        name="gate_sample",
    )(q, kmean_t)


def _head_page_copy(pool_ref, page, h, buf_ref, sem_ref, which, slot):
    rows = pl.ds(h * HEAD_DIM, HEAD_DIM)
    return pltpu.make_async_copy(pool_ref.at[page, rows, :], buf_ref.at[which, slot], sem_ref.at[which, slot])


def _sample_attn_kernel(top_ref, pt_ref, q_ref, kn_ref, vn_ref, tnear_ref, bias0_ref, bfar_ref,
                        kpool_ref, vpool_ref, o_ref, buf_ref, sem_ref, *, n_blk):
    b = pl.program_id(0)
    n_sel = MOBA_TOPK * PAGES_PER_BLOCK

    def copies(h):
        out = []
        for t in range(MOBA_TOPK):
            blk = top_ref[(b * N_HEADS_A + h) * MOBA_TOPK + t]
            for half in range(PAGES_PER_BLOCK):
                page = pt_ref[b, blk * PAGES_PER_BLOCK + half]
                slot = h * n_sel + t * PAGES_PER_BLOCK + half
                out.append(_head_page_copy(kpool_ref, page, h, buf_ref, sem_ref, 0, slot))
                out.append(_head_page_copy(vpool_ref, page, h, buf_ref, sem_ref, 1, slot))
        return out

    for h in range(N_HEADS_A):
        for c in copies(h):
            c.start()

    q = q_ref[0]
    kn = kn_ref[0].astype(BF16).astype(F32)
    vn = vn_ref[0].astype(BF16).astype(F32)
    s_new_all = q.astype(F32) * kn
    outs = []
    for h in range(N_HEADS_A):
        for c in copies(h):
            c.wait()
        lanes = slice(h * HEAD_DIM, (h + 1) * HEAD_DIM)
        qh = jnp.broadcast_to(q[:, lanes], (SUBLANES, HEAD_DIM))
        kt = jnp.concatenate([buf_ref[0, h * n_sel + s] for s in range(n_sel)], axis=1).astype(BF16)
        vt = jnp.concatenate([buf_ref[1, h * n_sel + s] for s in range(n_sel)], axis=1).astype(BF16)
        bias = []
        for t in range(MOBA_TOPK):
            blk = top_ref[(b * N_HEADS_A + h) * MOBA_TOPK + t]
            bias.append(jnp.where(blk == n_blk - 1, tnear_ref[h:h + 1, :], bfar_ref[h]))
        s = jnp.dot(qh, kt, preferred_element_type=F32)[:1] + jnp.concatenate(bias, axis=1)
        s_new = jnp.sum(s_new_all[:, lanes], axis=1, keepdims=True) + bias0_ref[h]
        m = jnp.maximum(jnp.max(s, axis=1, keepdims=True), s_new)
        p = jnp.exp(s - m)
        p_new = jnp.exp(s_new - m)
        l = jnp.sum(p, axis=1, keepdims=True) + p_new
        pb = jnp.broadcast_to(p.astype(BF16), (SUBLANES, p.shape[1]))
        o = lax.dot_general(pb, vt, _NT, preferred_element_type=F32)[:1]
        o = o + p_new.astype(BF16).astype(F32) * vn[:, lanes]
        outs.append(o / l)
    o_ref[0] = jnp.concatenate(outs, axis=1)


def _sample_attn_call(top, page_table, q, kn, vn, tnear, bias0, bfar, kpool_t, vpool_t, n_blk):
    n_dec = q.shape[0]
    row = pl.BlockSpec((1, 1, D_A), lambda b, *_: (b, 0, 0))
    smem = pl.BlockSpec(memory_space=pltpu.SMEM)
    any_ = pl.BlockSpec(memory_space=pl.ANY)
    n_slots = N_HEADS_A * MOBA_TOPK * PAGES_PER_BLOCK
    return pl.pallas_call(
        functools.partial(_sample_attn_kernel, n_blk=n_blk),
        grid_spec=pltpu.PrefetchScalarGridSpec(
            num_scalar_prefetch=2,
            grid=(n_dec,),
            in_specs=[row, row, row, pl.BlockSpec(tnear.shape, lambda b, *_: (0, 0)), smem, smem, any_, any_],
            out_specs=row,
            scratch_shapes=[pltpu.VMEM((2, n_slots, HEAD_DIM, PAGE_SIZE), F32),
                            pltpu.SemaphoreType.DMA((2, n_slots))]),
        out_shape=jax.ShapeDtypeStruct((n_dec, 1, D_A), F32),
        compiler_params=_params(1),
        name="attn_sample",
    )(top, page_table, q, kn, vn, tnear, bias0, bfar, kpool_t, vpool_t)


def _mix_sample_kernel(x1_ref, a_ref, u_ref, st_ref, pw_ref, ps_ref, wout_ref, g1_ref, b1_ref, wxq_ref,
                       x2_ref, qx_ref, *, alpha, pos):
    u = u_ref[...]

    def window_sum(g, w):
        lanes = slice(g * POOL_GROUP, (g + 1) * POOL_GROUP)
        s = u[:, lanes]
        for back in range(1, w):
            s = s + st_ref[POOL_STATE - back, :, lanes]
        return s

    p = _pool_project(window_sum, u, lambda w: float(min(w, pos + 1)), pw_ref, ps_ref)
    x2 = _out_project(x1_ref[...], a_ref[...].astype(BF16), p, wout_ref, g1_ref, b1_ref, alpha)
    x2_ref[...] = x2
    qx = jnp.dot(x2.astype(BF16), wxq_ref[...], preferred_element_type=F32) * HEAD_DIM_X ** -0.5
    qx_ref[...] = qx.astype(BF16)


def _mix_sample_call(x1, a, u, state_t, pw, ps, wout, g1, b1, wxq, *, alpha, pos):
    n = x1.shape[0]
    args = (x1, a, u, state_t, pw, ps, wout, g1, b1, wxq)
    return pl.pallas_call(
        functools.partial(_mix_sample_kernel, alpha=alpha, pos=pos),
        grid=(1,),
        in_specs=[_resident(v.shape) for v in args],
        out_specs=[_resident((n, D_MODEL)), _resident((n, D_MODEL))],
        out_shape=[jax.ShapeDtypeStruct((n, D_MODEL), F32), jax.ShapeDtypeStruct((n, D_MODEL), BF16)],
        compiler_params=_params(1),
        name="mix_sample",
    )(*args)


def _xattn_sample_kernel(qx_ref, mk_ref, mv_ref, o_ref):
    qm = _head_rows_x(qx_ref[0])
    logits = lax.dot_general(qm, mk_ref[0].astype(BF16), _NT, preferred_element_type=F32)
    e = jnp.exp(logits - jnp.max(logits, axis=-1, keepdims=True))
    o = jnp.dot(e.astype(BF16), mv_ref[0].astype(BF16), preferred_element_type=F32)
    o = o / jnp.sum(e, axis=-1, keepdims=True)
    row = lax.broadcasted_iota(jnp.int32, o.shape, 0)
    lane_head = lax.broadcasted_iota(jnp.int32, o.shape, 1) // HEAD_DIM_X
    o_ref[0] = jnp.sum(jnp.where(lane_head == row, o, 0.0), axis=0, keepdims=True)


def _head_rows_x(q_row):
    return _mask_heads(q_row, (SUBLANES, D_MODEL), HEAD_DIM_X)


def _xattn_sample_call(qx, mk, mv):
    n_dec, n_mem, _ = mk.shape
    row = pl.BlockSpec((1, 1, D_MODEL), lambda b: (b, 0, 0))
    mem = pl.BlockSpec((1, n_mem, D_MODEL), lambda b: (b, 0, 0))
    return pl.pallas_call(
        _xattn_sample_kernel,
        grid=(n_dec,),
        in_specs=[row, mem, mem],
        out_specs=row,
        out_shape=jax.ShapeDtypeStruct((n_dec, 1, D_MODEL), F32),
        compiler_params=_params(1),
        name="xattn_sample",
    )(qx, mk, mv)


def _proj_ln_kernel(x_ref, o_ref, w_ref, g_ref, b_ref, y_ref, *, alpha):
    proj = jnp.dot(o_ref[...].astype(BF16), w_ref[...], preferred_element_type=F32)
    y_ref[...] = _layer_norm(alpha * x_ref[...] + proj, g_ref[...], b_ref[...])


def _proj_ln_call(x, o, w, g, b, *, alpha):
    args = (x, o, w, g, b)
    return pl.pallas_call(
        functools.partial(_proj_ln_kernel, alpha=alpha),
        grid=(1,),
        in_specs=[_resident(v.shape) for v in args],
        out_specs=_resident(x.shape),
        out_shape=jax.ShapeDtypeStruct(x.shape, F32),
        compiler_params=_params(1),
        name="proj_ln",
    )(*args)


def _t5_bucket(dist):
    n = jnp.maximum(dist, 0)
    max_exact = N_BUCKETS // 2
    nf = jnp.maximum(n, 1).astype(F32)
    large = max_exact + (jnp.log(nf / max_exact) / math.log(MAX_DISTANCE / max_exact)
                         * (N_BUCKETS - max_exact)).astype(jnp.int32)
    return jnp.where(n < max_exact, n, jnp.minimum(large, N_BUCKETS - 1))


def kernel(x_prompt, x_sample, cache_k, cache_v, cache_mem_k, cache_mem_v, state_pool, page_table, mem_prompt, rel_bias, ln_g, ln_b, w_ff1_gate, w_ff1_up, w_ff1_down, w_in, pool_w, pool_scale, w_out, w_xq, w_xk, w_xv, w_xo, w_ff2_gate, w_ff2_up, w_ff2_down):
    n_batch, seq, _ = x_prompt.shape
    n_dec, dec_seq, _ = x_sample.shape
    depth = ln_g.shape[0]
    n_pool = cache_k.shape[1]
    n_pages = page_table.shape[1]
    past_len = n_pages * PAGE_SIZE
    n_mem = mem_prompt.shape[1]
    assert depth == 1 and dec_seq == 1
    assert seq % MOBA_BLOCK == 0 and past_len % MOBA_BLOCK == 0
    assert past_len // MOBA_BLOCK >= MOBA_TOPK
    alpha = (2 * depth) ** 0.25
    tm = min(512, seq)
    n_blk = seq // MOBA_BLOCK

    bf = lambda w: w[0].astype(BF16)
    ff1 = (bf(w_ff1_gate), bf(w_ff1_up), bf(w_ff1_down))
    ff2 = (bf(w_ff2_gate), bf(w_ff2_up), bf(w_ff2_down))
    win = bf(w_in)
    ln = lambda i: (ln_g[0, i:i + 1], ln_b[0, i:i + 1])
    (g0, b0), (g1, b1), (g2, b2), (g3, b3) = ln(0), ln(1), ln(2), ln(3)
    pw, ps = pool_w[0].astype(BF16), pool_scale
    wout, wxq, wxo = bf(w_out), bf(w_xq), bf(w_xo)

    bias_by_dist = rel_bias[_t5_bucket(jnp.arange(2 * MOBA_BLOCK, dtype=jnp.int32))].T
    bfar = rel_bias[_far_bucket()]

    xp = x_prompt.reshape(n_batch * seq, D_MODEL)
    attn_scale = HEAD_DIM ** -0.5
    x1, q, kt, vt, u, kb, vtb, km = _ffn_proj_call(xp, *ff1, g0, b0, win, alpha=alpha, q_scale=attn_scale * LOG2E,
                                                   tm=tm, moba_batch=n_batch)
    a = _moba_call(rel_bias.reshape(-1) * LOG2E, q.reshape(n_batch, seq, D_A), kb.reshape(n_batch, seq, D_A), vtb,
                   km.reshape(n_batch, n_blk, D_A))
    mk, mv, mkb, mvb = _memkv_call(mem_prompt.reshape(n_batch * n_mem, D_MODEL), bf(w_xk), bf(w_xv),
                                   tm=min(512, n_batch * n_mem))
    x3 = _mix_call(x1, a.reshape(n_batch * seq, D_A), u, pw, ps, wout, g1, b1, wxq, wxo,
                   mkb.reshape(n_batch, n_mem, D_MODEL), mvb.reshape(n_batch, n_mem, D_MODEL), g2, b2,
                   alpha=alpha, n_batch=n_batch, tm=tm)
    y_prompt = _ffn_call(x3, *ff2, g3, b3, alpha=alpha, tm=tm).reshape(n_batch, seq, D_MODEL)

    to_heads = lambda t: t.reshape(t.shape[0], N_HEADS_A, HEAD_DIM, t.shape[2]).transpose(0, 3, 1, 2)[None]
    k_prompt, v_prompt = to_heads(kt), to_heads(vt)
    pool_prompt = u.reshape(n_batch, seq, D_POOL)[None, :, seq - POOL_STATE:]
    memk_prompt = mk.reshape(1, n_batch, n_mem, N_HEADS_X, HEAD_DIM_X)
    memv_prompt = mv.reshape(1, n_batch, n_mem, N_HEADS_X, HEAD_DIM_X)

    xs = x_sample.reshape(n_dec, D_MODEL)
    x1s, qs, kn, vn, us = _ffn_proj_call(xs, *ff1, g0, b0, win, alpha=alpha, q_scale=attn_scale, tm=n_dec)
    kpool_t = cache_k[0].transpose(0, 2, 3, 1).reshape(n_pool, D_A, PAGE_SIZE)
    vpool_t = cache_v[0].transpose(0, 2, 3, 1).reshape(n_pool, D_A, PAGE_SIZE)
    n_blk_s = past_len // MOBA_BLOCK
    kmean_t = _kmean_call(page_table, kpool_t)
    qs3 = qs.reshape(n_dec, 1, D_A)
    top = _gate_call(qs3, kmean_t, n_blk_s)[:, :, :MOBA_TOPK].reshape(-1)
    tnear = bias_by_dist[:, MOBA_BLOCK - jnp.arange(MOBA_BLOCK)]
    a_s = _sample_attn_call(top, page_table, qs3, kn.reshape(n_dec, 1, D_A), vn.reshape(n_dec, 1, D_A),
                            tnear, bias_by_dist[:, 0], bfar, kpool_t, vpool_t, n_blk_s)
    state_t = state_pool[0].transpose(1, 0, 2)
    x2s, qxs = _mix_sample_call(x1s, a_s.reshape(n_dec, D_A), us, state_t, pw, ps, wout, g1, b1, wxq,
                                alpha=alpha, pos=past_len)
    o_s = _xattn_sample_call(qxs.reshape(n_dec, 1, D_MODEL),
                             cache_mem_k[0].reshape(n_dec, n_mem, D_MODEL),
                             cache_mem_v[0].reshape(n_dec, n_mem, D_MODEL))
    x3s = _proj_ln_call(x2s, o_s.reshape(n_dec, D_MODEL), wxo, g2, b2, alpha=alpha)
    y_sample = _ffn_call(x3s, *ff2, g3, b3, alpha=alpha, tm=n_dec).reshape(n_dec, 1, D_MODEL)

    k_sample = kn.reshape(1, n_dec, 1, N_HEADS_A, HEAD_DIM)
    v_sample = vn.reshape(1, n_dec, 1, N_HEADS_A, HEAD_DIM)
    pool_sample = jnp.concatenate([state_t[1:], us[None]], axis=0).transpose(1, 0, 2)[None]
    return (y_prompt, y_sample, k_prompt, v_prompt, pool_prompt, memk_prompt, memv_prompt,
            k_sample, v_sample, pool_sample)
```

```python
import functools
import math

import jax
import jax.numpy as jnp
import numpy as np
from jax import lax
from jax.experimental import pallas as pl
from jax.experimental.pallas import tpu as pltpu

F32 = jnp.float32
BF16 = jnp.bfloat16

D_MODEL = 1024
HEAD_DIM = 64
N_HEADS_A = 8
D_A = N_HEADS_A * HEAD_DIM
MOBA_BLOCK = 256
MOBA_TOPK = 3
D_POOL = D_MODEL - D_A
POOL_WINDOWS = (2, 4, 8, 16)
POOL_GROUP = D_POOL // len(POOL_WINDOWS)
POOL_STATE = max(POOL_WINDOWS) - 1
N_BUCKETS = 32
MAX_DISTANCE = 128
N_HEADS_X = 4
HEAD_DIM_X = D_MODEL // N_HEADS_X
D_FF = 2816
LN_EPS = 1e-5
PAGE_SIZE = 128
NEG = -1e30

LANES = 128
SUBLANES = 8
VMEM_LIMIT_BYTES = 56 * 1024 * 1024

BF16_SUBLANES = 16
HEADS_PER_TILE = LANES // HEAD_DIM
V_ROWS = HEAD_DIM + BF16_SUBLANES
LOG2E = math.log2(math.e)
POOL_HALO = POOL_STATE + 1
PAGES_PER_BLOCK = MOBA_BLOCK // PAGE_SIZE
FFN_CHUNK = 256
KMEAN_RING = 16
SCORE_CHUNK = 32

_NT = (((1,), (1,)), ((), ()))


def _params(n_grid_dims):
    return pltpu.CompilerParams(
        dimension_semantics=("arbitrary",) * n_grid_dims, vmem_limit_bytes=VMEM_LIMIT_BYTES)


def _resident(shape):
    return pl.BlockSpec(shape, lambda *_: (0,) * len(shape), pipeline_mode=pl.Buffered(1))


def _layer_norm(z, g, b):
    mu = jnp.mean(z, axis=-1, keepdims=True)
    zc = z - mu
    var = jnp.mean(zc * zc, axis=-1, keepdims=True)
    return zc * lax.rsqrt(var + LN_EPS) * g + b


def _swiglu(x, wg_ref, wu_ref, wd_ref):
    xb = x.astype(BF16)
    acc = None
    for c in range(D_FF // FFN_CHUNK):
        sl = slice(c * FFN_CHUNK, (c + 1) * FFN_CHUNK)
        g = jnp.dot(xb, wg_ref[:, sl], preferred_element_type=F32)
        u = jnp.dot(xb, wu_ref[:, sl], preferred_element_type=F32)
        h = (g * jax.nn.sigmoid(g) * u).astype(BF16)
        part = jnp.dot(h, wd_ref[sl, :], preferred_element_type=F32)
        acc = part if acc is None else acc + part
    return acc


def _ffn_kernel(x_ref, wg_ref, wu_ref, wd_ref, g_ref, b_ref, y_ref, *, alpha):
    x = x_ref[...]
    y_ref[...] = _layer_norm(alpha * x + 0.5 * _swiglu(x, wg_ref, wu_ref, wd_ref), g_ref[...], b_ref[...])


def _ffn_proj_kernel(x_ref, wg_ref, wu_ref, wd_ref, g_ref, b_ref, win_ref,
                     x1_ref, q_ref, k_ref, v_ref, u_ref, *moba_refs, alpha, q_scale):
    x = x_ref[...]
    x1 = _layer_norm(alpha * x + 0.5 * _swiglu(x, wg_ref, wu_ref, wd_ref), g_ref[...], b_ref[...])
    x1_ref[...] = x1
    h = jnp.dot(x1.astype(BF16), win_ref[...], preferred_element_type=F32)
    q_ref[...] = (h[:, :D_A] * q_scale).astype(BF16)
    k = h[:, D_A:2 * D_A]
    v = h[:, 2 * D_A:3 * D_A]
    u_ref[...] = h[:, 3 * D_A:]
    if not moba_refs:
        k_ref[...] = k
        v_ref[...] = v
        return
    kb_ref, vtb_ref, km_ref = moba_refs
    vt = v.T
    k_ref[0] = k.T
    v_ref[0] = vt
    kb_ref[...] = k.astype(BF16)
    ones = jnp.ones((BF16_SUBLANES, MOBA_BLOCK), BF16)
    for blk in range(k.shape[0] // MOBA_BLOCK):
        rows = slice(blk * MOBA_BLOCK, (blk + 1) * MOBA_BLOCK)
        for hd in range(N_HEADS_A):
            vtb_ref[0, blk, hd * V_ROWS:hd * V_ROWS + HEAD_DIM, :] = (
                vt[hd * HEAD_DIM:(hd + 1) * HEAD_DIM, rows].astype(BF16))
            vtb_ref[0, blk, hd * V_ROWS + HEAD_DIM:(hd + 1) * V_ROWS, :] = ones
        km_ref[blk] = jnp.mean(k[rows], axis=0, keepdims=True)


def _ffn_call(x, wg, wu, wd, g, b, *, alpha, tm):
    n = x.shape[0]
    row = pl.BlockSpec((tm, D_MODEL), lambda i: (i, 0))
    return pl.pallas_call(
        functools.partial(_ffn_kernel, alpha=alpha),
        grid=(n // tm,),
        in_specs=[row, _resident(wg.shape), _resident(wu.shape), _resident(wd.shape),
                  _resident(g.shape), _resident(b.shape)],
        out_specs=row,
        out_shape=jax.ShapeDtypeStruct((n, D_MODEL), F32),
        compiler_params=_params(1),
        name="ffn",
    )(x, wg, wu, wd, g, b)


def _ffn_proj_call(x, wg, wu, wd, g, b, win, *, alpha, q_scale, tm, moba_batch=None):
    n = x.shape[0]
    row = lambda w: pl.BlockSpec((tm, w), lambda i: (i, 0))
    sds = jax.ShapeDtypeStruct
    out_specs = [row(D_MODEL), row(D_A)]
    out_shape = [sds((n, D_MODEL), F32), sds((n, D_A), BF16)]
    if moba_batch is None:
        out_specs += [row(D_A), row(D_A), row(D_POOL)]
        out_shape += [sds((n, D_A), F32), sds((n, D_A), F32), sds((n, D_POOL), F32)]
    else:
        seq = n // moba_batch
        tpb = seq // tm
        bpt = tm // MOBA_BLOCK
        tcol = pl.BlockSpec((1, D_A, tm), lambda i: (i // tpb, 0, i % tpb))
        out_specs += [tcol, tcol, row(D_POOL), row(D_A),
                      pl.BlockSpec((1, bpt, N_HEADS_A * V_ROWS, MOBA_BLOCK), lambda i: (i // tpb, i % tpb, 0, 0)),
                      pl.BlockSpec((bpt, 1, D_A), lambda i: (i, 0, 0))]
        out_shape += [sds((moba_batch, D_A, seq), F32), sds((moba_batch, D_A, seq), F32), sds((n, D_POOL), F32),
                      sds((n, D_A), BF16),
                      sds((moba_batch, seq // MOBA_BLOCK, N_HEADS_A * V_ROWS, MOBA_BLOCK), BF16),
                      sds((n // MOBA_BLOCK, 1, D_A), F32)]
    return pl.pallas_call(
        functools.partial(_ffn_proj_kernel, alpha=alpha, q_scale=q_scale),
        grid=(n // tm,),
        in_specs=[row(D_MODEL), _resident(wg.shape), _resident(wu.shape), _resident(wd.shape),
                  _resident(g.shape), _resident(b.shape), _resident(win.shape)],
        out_specs=out_specs,
        out_shape=out_shape,
        compiler_params=_params(1),
        name="ffn_proj",
    )(x, wg, wu, wd, g, b, win)


def _memkv_kernel(m_ref, wk_ref, wv_ref, k_ref, v_ref, kb_ref, vb_ref):
    mb = m_ref[...].astype(BF16)
    k = jnp.dot(mb, wk_ref[...], preferred_element_type=F32)
    v = jnp.dot(mb, wv_ref[...], preferred_element_type=F32)
    k_ref[...] = k
    v_ref[...] = v
    kb_ref[...] = k.astype(BF16)
    vb_ref[...] = v.astype(BF16)


def _memkv_call(mem, wk, wv, *, tm):
    n = mem.shape[0]
    row = pl.BlockSpec((tm, D_MODEL), lambda i: (i, 0))
    sds = lambda dt: jax.ShapeDtypeStruct((n, D_MODEL), dt)
    return pl.pallas_call(
        _memkv_kernel,
        grid=(n // tm,),
        in_specs=[row, _resident(wk.shape), _resident(wv.shape)],
        out_specs=[row, row, row, row],
        out_shape=[sds(F32), sds(F32), sds(BF16), sds(BF16)],
        compiler_params=_params(1),
        name="memkv",
    )(mem, wk, wv)


def _top3_rows(gate, n_valid, n_rows):
    rid = lax.broadcasted_iota(jnp.int32, gate.shape, 0).astype(F32)
    g = jnp.where(rid < n_valid, gate, NEG)
    sel = jnp.zeros(gate.shape, jnp.bool_)
    for _ in range(MOBA_TOPK):
        m = jnp.max(g, axis=0, keepdims=True)
        idx = jnp.min(jnp.where(g == m, rid, float(n_rows)), axis=0, keepdims=True)
        pick = rid == idx
        sel = jnp.logical_or(sel, jnp.logical_and(pick, idx < n_valid))
        g = jnp.where(pick, -jnp.inf, g)
    return sel


def _moba_kernel(rb_ref, pt_ref, q_ref, kb_ref, vtb_ref, km_ref, bkt_ref, kpool_ref, o_ref, kmo_ref,
                 tbl_ref, rbt_ref, qh_ref, s_ref, p_ref, acc_ref, m_ref, smx_ref, al_ref,
                 ring_ref, ring_sem, kacc_ref, out_sem, *, n_blk, far_bucket, n_dec, n_pool_blk_per_dec):
    b = pl.program_id(0)
    i = pl.program_id(1)
    row_zero, row_prev = n_blk, n_blk + 1
    n_chunk = MOBA_BLOCK // SCORE_CHUNK
    chunks = [slice(c * SCORE_CHUNK, (c + 1) * SCORE_CHUNK) for c in range(n_chunk)]
    head_lanes = lambda h: slice((h // HEADS_PER_TILE) * LANES, (h // HEADS_PER_TILE + 1) * LANES)

    @pl.when(jnp.logical_and(b == 0, i == 0))
    def _build_bias_tiles():
        def per_head(h, _):
            for slot in range(2):
                bk = bkt_ref[slot]
                t = jnp.zeros(bk.shape, F32)
                for bucket in range(N_BUCKETS):
                    t = jnp.where(bk == bucket, rb_ref[bucket * N_HEADS_A + h], t)
                tbl_ref[h, slot] = jnp.where(bk < 0, NEG, t)
            return 0
        lax.fori_loop(0, N_HEADS_A, per_head, 0)

    q = q_ref[0]
    km = km_ref[0]
    i_f = i.astype(F32)
    jp_f = jnp.maximum(i - 1, 0).astype(F32)
    lane_head = lax.broadcasted_iota(jnp.int32, (MOBA_BLOCK, LANES), 1) // HEAD_DIM
    rid = lax.broadcasted_iota(jnp.int32, (n_blk, MOBA_BLOCK), 0).astype(F32)
    for h in range(N_HEADS_A):
        q2 = q[:, head_lanes(h)].astype(F32)
        qh = jnp.where(lane_head == h % HEADS_PER_TILE, q2, 0.0).astype(BF16)
        qh_ref[h] = qh
        gate = lax.dot_general(km[:, head_lanes(h)].astype(BF16), qh, _NT, preferred_element_type=F32)
        sel = _top3_rows(gate, i_f, n_blk)
        rbt_ref[h, :n_blk] = jnp.where(sel, rb_ref[far_bucket * N_HEADS_A + h], NEG)
        prev_sel = jnp.max(jnp.where(jnp.logical_and(sel, rid == jp_f), 1.0, 0.0), axis=0, keepdims=True)
        rbt_ref[h, row_zero:row_zero + 1] = jnp.zeros((1, MOBA_BLOCK), F32)
        rbt_ref[h, row_prev:row_prev + 1] = jnp.where(prev_sel > 0.5, 0.0, NEG)
        m_ref[h] = jnp.full((1, MOBA_BLOCK), NEG, F32)
        al_ref[h] = jnp.ones((1, MOBA_BLOCK), F32)
        acc_ref[h] = jnp.zeros((V_ROWS, MOBA_BLOCK), F32)
        p_ref[h] = jnp.zeros((MOBA_BLOCK, MOBA_BLOCK), BF16)

    def list_block(t):
        return jnp.clip(jnp.where(t == 0, i, jnp.where(t == 1, i - 1, t - 2)), 0, n_blk - 1)

    def stage_a(t):
        rows = pl.ds(pl.multiple_of(list_block(t) * MOBA_BLOCK, MOBA_BLOCK), MOBA_BLOCK)
        for h in range(N_HEADS_A):
            s = lax.dot_general(kb_ref[0, rows, head_lanes(h)], qh_ref[h], _NT, preferred_element_type=F32)
            s_ref[h] = s
            smx_ref[h] = jnp.max(s, axis=0, keepdims=True)

    def add_bias_tile(slot):
        for h in range(N_HEADS_A):
            s = s_ref[h] + tbl_ref[h, slot]
            s_ref[h] = s
            smx_ref[h] = jnp.max(s, axis=0, keepdims=True)

    def stage_b(t):
        row = jnp.where(t == 0, row_zero, jnp.where(t == 1, row_prev, t - 2))
        for h in range(N_HEADS_A):
            rb = rbt_ref[h, pl.ds(row, 1), :]
            m_old = m_ref[h]
            m_new = jnp.maximum(m_old, smx_ref[h] + rb)
            shift = m_new - rb
            for c in chunks:
                p_ref[h, c, :] = jnp.exp2(s_ref[h, c, :] - shift).astype(BF16)
            m_ref[h] = m_new
            al_ref[h] = jnp.exp2(m_old - m_new)

    def stage_c(t):
        j = list_block(t)
        for h in range(N_HEADS_A):
            vt = vtb_ref[0, j, h * V_ROWS:(h + 1) * V_ROWS, :]
            acc_ref[h] = al_ref[h] * acc_ref[h] + jnp.dot(vt, p_ref[h], preferred_element_type=F32)

    n_pool_blk = n_dec * n_pool_blk_per_dec
    n_pool_pages = n_pool_blk * PAGES_PER_BLOCK
    ring = min(KMEAN_RING, n_pool_pages)
    iters_per_batch = n_blk * (n_blk + 1) // 2
    lane = lax.broadcasted_iota(jnp.int32, (D_A, LANES), 1)

    def page_copy(page_no, slot):
        return pltpu.make_async_copy(kpool_ref.at[pt_ref[page_no]], ring_ref.at[slot], ring_sem.at[slot])

    def pool_block_mean(g):
        g = jnp.minimum(g, n_pool_blk - 1)
        total = None
        for half in range(PAGES_PER_BLOCK):
            page_no = g * PAGES_PER_BLOCK + half
            slot = page_no % ring
            page_copy(page_no, slot).wait()
            x = ring_ref[slot]
            total = x if total is None else total + x
            nxt = jnp.where(page_no + ring < n_pool_pages, page_no + ring, page_no)
            page_copy(nxt, slot).start()
        col = jnp.sum(total, axis=1, keepdims=True) * (1.0 / MOBA_BLOCK)
        dec, j = g // n_pool_blk_per_dec, g % n_pool_blk_per_dec
        kacc_ref[dec] = jnp.where(lane == j, col, kacc_ref[dec])

    @pl.when(jnp.logical_and(b == 0, i == 0))
    def _start_ring():
        for slot in range(ring):
            page_copy(slot, slot).start()
        kacc_ref[...] = jnp.zeros(kacc_ref.shape, F32)

    stage_a(0)
    add_bias_tile(0)
    g0 = b * iters_per_batch + i * (i + 1) // 2

    def step(t, carry):
        pool_block_mean(g0 + t - 1)
        stage_c(jnp.maximum(t - 2, 0))
        stage_b(t - 1)
        stage_a(t)

        @pl.when(t == 1)
        def _():
            add_bias_tile(1)

        return carry

    lax.fori_loop(1, i + 2, step, 0)
    stage_c(i)
    out_t = jnp.concatenate([acc_ref[h, :HEAD_DIM] / acc_ref[h, HEAD_DIM:HEAD_DIM + 1]
                             for h in range(N_HEADS_A)], axis=0)
    o_ref[0] = out_t.T.astype(BF16)

    @pl.when(jnp.logical_and(b == pl.num_programs(0) - 1, i == n_blk - 1))
    def _finish_pool_means():
        n_done = pl.num_programs(0) * iters_per_batch

        def rest(g, carry):
            pool_block_mean(g)
            return carry

        lax.fori_loop(jnp.minimum(n_done, n_pool_blk), n_pool_blk, rest, 0)
        for slot in range(ring):
            page_copy(slot, slot).wait()
        out_copy = pltpu.make_async_copy(kacc_ref, kmo_ref, out_sem.at[0])
        out_copy.start()
        out_copy.wait()


def _bucket_np(dist):
    n = np.maximum(dist, 0)
    max_exact = N_BUCKETS // 2
    nf = np.maximum(n, 1).astype(np.float32)
    large = max_exact + (np.log(nf / np.float32(max_exact)) / np.float32(math.log(MAX_DISTANCE / max_exact))
                         * np.float32(N_BUCKETS - max_exact)).astype(np.int32)
    return np.where(n < max_exact, n, np.minimum(large, N_BUCKETS - 1)).astype(np.int32)


def _bucket_tiles():
    key = np.arange(MOBA_BLOCK)[:, None]
    qry = np.arange(MOBA_BLOCK)[None, :]
    own = np.where(qry >= key, _bucket_np(qry - key), -1)
    prev = _bucket_np(qry - key + MOBA_BLOCK)
    return np.stack([own, prev]).astype(np.int32)


def _far_bucket():
    far = _bucket_np(np.array([MOBA_BLOCK + 1, 1 << 30]))
    assert far[0] == far[1]
    return int(far[0])


def _moba_call(rel_bias_flat, page_table, q, kb, vtb, km, kpool_t):
    n_batch, seq, _ = q.shape
    n_blk = seq // MOBA_BLOCK
    n_dec, n_pages = page_table.shape
    assert n_pages % PAGES_PER_BLOCK == 0 and n_pages // PAGES_PER_BLOCK <= LANES
    qo = pl.BlockSpec((1, MOBA_BLOCK, D_A), lambda b, i, *_: (b, i, 0))

    def per_batch(shape, **kw):
        return pl.BlockSpec((1,) + shape, lambda b, i, *_: (b,) + (0,) * len(shape), **kw)

    once = dict(pipeline_mode=pl.Buffered(1))
    any_ = pl.BlockSpec(memory_space=pl.ANY)
    tile = (MOBA_BLOCK, MOBA_BLOCK)
    stat = pltpu.VMEM((N_HEADS_A, 1, MOBA_BLOCK), F32)
    return pl.pallas_call(
        functools.partial(_moba_kernel, n_blk=n_blk, far_bucket=_far_bucket(), n_dec=n_dec,
                          n_pool_blk_per_dec=n_pages // PAGES_PER_BLOCK),
        grid_spec=pltpu.PrefetchScalarGridSpec(
            num_scalar_prefetch=2,
            grid=(n_batch, n_blk),
            in_specs=[qo, per_batch((seq, D_A), **once),
                      per_batch((n_blk, N_HEADS_A * V_ROWS, MOBA_BLOCK), **once),
                      per_batch((n_blk, D_A)), _resident((2,) + tile), any_],
            out_specs=[qo, any_],
            scratch_shapes=[pltpu.VMEM((N_HEADS_A, 2) + tile, F32),
                            pltpu.VMEM((N_HEADS_A, n_blk + SUBLANES, MOBA_BLOCK), F32),
                            pltpu.VMEM((N_HEADS_A, MOBA_BLOCK, LANES), BF16),
                            pltpu.VMEM((N_HEADS_A,) + tile, F32),
                            pltpu.VMEM((N_HEADS_A,) + tile, BF16),
                            pltpu.VMEM((N_HEADS_A, V_ROWS, MOBA_BLOCK), F32),
                            stat, stat, stat,
                            pltpu.VMEM((KMEAN_RING, D_A, PAGE_SIZE), F32),
                            pltpu.SemaphoreType.DMA((KMEAN_RING,)),
                            pltpu.VMEM((n_dec, D_A, LANES), F32),
                            pltpu.SemaphoreType.DMA((1,))]),
        out_shape=[jax.ShapeDtypeStruct((n_batch, seq, D_A), BF16),
                   jax.ShapeDtypeStruct((n_dec, D_A, LANES), F32)],
        compiler_params=_params(2),
        name="moba_prompt",
    )(rel_bias_flat, page_table.reshape(-1), q, kb, vtb, km, jnp.asarray(_bucket_tiles()), kpool_t)


def _pool_project(window_sum, u_new, cnt, pw_ref, ps_ref):
    ys = []
    for g, w in enumerate(POOL_WINDOWS):
        lanes = slice(g * POOL_GROUP, (g + 1) * POOL_GROUP)
        d = window_sum(g, w) / cnt(w) - u_new[:, lanes]
        ys.append(jnp.dot(d.astype(BF16), pw_ref[g], preferred_element_type=F32))
    return jnp.concatenate(ys, axis=1) * ps_ref[...]


def _out_project(x1, a, p, wout_ref, g_ref, b_ref, alpha):
    proj = (jnp.dot(a, wout_ref[:D_A, :], preferred_element_type=F32)
            + jnp.dot(p.astype(BF16), wout_ref[D_A:, :], preferred_element_type=F32))
    return _layer_norm(alpha * x1 + proj, g_ref[...], b_ref[...])


def _mix_kernel(x1_ref, a_ref, u_ref, uh_ref, pw_ref, ps_ref, wout_ref, g1_ref, b1_ref,
                wxq_ref, wxo_ref, mk_ref, mv_ref, g2_ref, b2_ref, x3_ref, ext_ref, *, alpha, tm, tpb):
    t_in_b = pl.program_id(0) % tpb
    ext_ref[:POOL_HALO, :] = jnp.where(t_in_b == 0, 0.0, uh_ref[...])
    ext_ref[POOL_HALO:, :] = u_ref[...]
    pos = t_in_b * tm + lax.broadcasted_iota(jnp.int32, (tm, 1), 0)

    def window_sum(g, w):
        lanes = slice(g * POOL_GROUP, (g + 1) * POOL_GROUP)
        s = ext_ref[POOL_HALO:POOL_HALO + tm, lanes]
        for back in range(1, w):
            s = s + ext_ref[POOL_HALO - back:POOL_HALO - back + tm, lanes]
        return s

    p = _pool_project(window_sum, u_ref[...], lambda w: jnp.minimum(w, pos + 1).astype(F32), pw_ref, ps_ref)
    x2 = _out_project(x1_ref[...], a_ref[...], p, wout_ref, g1_ref, b1_ref, alpha)

    qx = jnp.dot(x2.astype(BF16), wxq_ref[...], preferred_element_type=F32) * HEAD_DIM_X ** -0.5
    outs = []
    for h in range(N_HEADS_X):
        lanes = slice(h * HEAD_DIM_X, (h + 1) * HEAD_DIM_X)
        logits = lax.dot_general(qx[:, lanes].astype(BF16), mk_ref[0, :, lanes], _NT, preferred_element_type=F32)
        e = jnp.exp(logits - jnp.max(logits, axis=-1, keepdims=True))
        o = jnp.dot(e.astype(BF16), mv_ref[0, :, lanes], preferred_element_type=F32)
        outs.append(o / jnp.sum(e, axis=-1, keepdims=True))
    o = jnp.concatenate(outs, axis=1).astype(BF16)
    x3_ref[...] = _layer_norm(alpha * x2 + jnp.dot(o, wxo_ref[...], preferred_element_type=F32),
                              g2_ref[...], b2_ref[...])


def _mix_call(x1, a, u, pw, ps, wout, g1, b1, wxq, wxo, mk, mv, g2, b2, *, alpha, n_batch, tm):
    n = x1.shape[0]
    tpb = n // n_batch // tm
    n_mem = mk.shape[1]
    row = lambda w: pl.BlockSpec((tm, w), lambda i: (i, 0))
    halo = pl.BlockSpec((POOL_HALO, D_POOL), lambda i: (jnp.maximum(i * (tm // POOL_HALO) - 1, 0), 0))
    mem = pl.BlockSpec((1, n_mem, D_MODEL), lambda i: (i // tpb, 0, 0))
    return pl.pallas_call(
        functools.partial(_mix_kernel, alpha=alpha, tm=tm, tpb=tpb),
        grid=(n // tm,),
        in_specs=[row(D_MODEL), row(D_A), row(D_POOL), halo, _resident(pw.shape), _resident(ps.shape),
                  _resident(wout.shape), _resident(g1.shape), _resident(b1.shape),
                  _resident(wxq.shape), _resident(wxo.shape), mem, mem,
                  _resident(g2.shape), _resident(b2.shape)],
        out_specs=row(D_MODEL),
        out_shape=jax.ShapeDtypeStruct((n, D_MODEL), F32),
        scratch_shapes=[pltpu.VMEM((tm + POOL_HALO, D_POOL), F32)],
        compiler_params=_params(1),
        name="mix_prompt",
    )(x1, a, u, u, pw, ps, wout, g1, b1, wxq, wxo, mk, mv, g2, b2)


def _head_rows(q_row, n_rows):
    return _mask_heads(q_row, (n_rows, D_A), HEAD_DIM)


def _mask_heads(q_row, shape, head_dim):
    row = lax.broadcasted_iota(jnp.int32, shape, 0)
    lane_head = lax.broadcasted_iota(jnp.int32, shape, 1) // head_dim
    qb = jnp.broadcast_to(q_row.astype(F32), shape)
    return jnp.where(lane_head == row, qb, 0.0).astype(q_row.dtype)


def _gate_kernel(q_ref, km_ref, o_ref, *, n_blk):
    qm = _head_rows(q_ref[0], N_HEADS_A)
    gate = jnp.dot(qm, km_ref[0].astype(BF16), preferred_element_type=F32)
    lane = lax.broadcasted_iota(jnp.int32, gate.shape, 1)
    lane_f = lane.astype(F32)
    g = jnp.where(lane < n_blk, gate, NEG)
    out = jnp.zeros(gate.shape, F32)
    for t in range(MOBA_TOPK):
        m = jnp.max(g, axis=1, keepdims=True)
        idx = jnp.min(jnp.where(g == m, lane_f, float(LANES)), axis=1, keepdims=True)
        out = jnp.where(lane == t, idx, out)
        g = jnp.where(lane_f == idx, -jnp.inf, g)
    o_ref[0] = out.astype(jnp.int32)


def _gate_call(q, kmean_t, n_blk):
    n_dec = q.shape[0]
    return pl.pallas_call(
        functools.partial(_gate_kernel, n_blk=n_blk),
        grid=(n_dec,),
        in_specs=[pl.BlockSpec((1, 1, D_A), lambda b: (b, 0, 0)),
                  pl.BlockSpec((1, D_A, LANES), lambda b: (b, 0, 0))],
        out_specs=pl.BlockSpec((1, N_HEADS_A, LANES), lambda b: (b, 0, 0)),
        out_shape=jax.ShapeDtypeStruct((n_dec, N_HEADS_A, LANES), jnp.int32),
        compiler_params=_params(1),
        name="gate_sample",
    )(q, kmean_t)


def _head_page_copy(pool_ref, page, h, buf_ref, sem_ref, which, slot):
    rows = pl.ds(h * HEAD_DIM, HEAD_DIM)
    return pltpu.make_async_copy(pool_ref.at[page, rows, :], buf_ref.at[which, slot], sem_ref.at[which, slot])


def _sample_attn_kernel(top_ref, pt_ref, q_ref, kn_ref, vn_ref, tnear_ref, bias0_ref, bfar_ref,
                        kpool_ref, vpool_ref, o_ref, buf_ref, sem_ref, *, n_blk):
    b = pl.program_id(0)
    n_sel = MOBA_TOPK * PAGES_PER_BLOCK

    def copies(h):
        out = []
        for t in range(MOBA_TOPK):
            blk = top_ref[(b * N_HEADS_A + h) * MOBA_TOPK + t]
            for half in range(PAGES_PER_BLOCK):
                page = pt_ref[b, blk * PAGES_PER_BLOCK + half]
                slot = h * n_sel + t * PAGES_PER_BLOCK + half
                out.append(_head_page_copy(kpool_ref, page, h, buf_ref, sem_ref, 0, slot))
                out.append(_head_page_copy(vpool_ref, page, h, buf_ref, sem_ref, 1, slot))
        return out

    for h in range(N_HEADS_A):
        for c in copies(h):
            c.start()

    q = q_ref[0]
    kn = kn_ref[0].astype(BF16).astype(F32)
    vn = vn_ref[0].astype(BF16).astype(F32)
    s_new_all = q.astype(F32) * kn
    outs = []
    for h in range(N_HEADS_A):
        for c in copies(h):
            c.wait()
        lanes = slice(h * HEAD_DIM, (h + 1) * HEAD_DIM)
        qh = jnp.broadcast_to(q[:, lanes], (SUBLANES, HEAD_DIM))
        kt = jnp.concatenate([buf_ref[0, h * n_sel + s] for s in range(n_sel)], axis=1).astype(BF16)
        vt = jnp.concatenate([buf_ref[1, h * n_sel + s] for s in range(n_sel)], axis=1).astype(BF16)
        bias = []
        for t in range(MOBA_TOPK):
            blk = top_ref[(b * N_HEADS_A + h) * MOBA_TOPK + t]
            bias.append(jnp.where(blk == n_blk - 1, tnear_ref[h:h + 1, :], bfar_ref[h]))
        s = jnp.dot(qh, kt, preferred_element_type=F32)[:1] + jnp.concatenate(bias, axis=1)
        s_new = jnp.sum(s_new_all[:, lanes], axis=1, keepdims=True) + bias0_ref[h]
        m = jnp.maximum(jnp.max(s, axis=1, keepdims=True), s_new)
        p = jnp.exp(s - m)
        p_new = jnp.exp(s_new - m)
        l = jnp.sum(p, axis=1, keepdims=True) + p_new
        pb = jnp.broadcast_to(p.astype(BF16), (SUBLANES, p.shape[1]))
        o = lax.dot_general(pb, vt, _NT, preferred_element_type=F32)[:1]
        o = o + p_new.astype(BF16).astype(F32) * vn[:, lanes]
        outs.append(o / l)
    o_ref[0] = jnp.concatenate(outs, axis=1)


def _sample_attn_call(top, page_table, q, kn, vn, tnear, bias0, bfar, kpool_t, vpool_t, n_blk):
    n_dec = q.shape[0]
    row = pl.BlockSpec((1, 1, D_A), lambda b, *_: (b, 0, 0))
    smem = pl.BlockSpec(memory_space=pltpu.SMEM)
    any_ = pl.BlockSpec(memory_space=pl.ANY)
    n_slots = N_HEADS_A * MOBA_TOPK * PAGES_PER_BLOCK
    return pl.pallas_call(
        functools.partial(_sample_attn_kernel, n_blk=n_blk),
        grid_spec=pltpu.PrefetchScalarGridSpec(
            num_scalar_prefetch=2,
            grid=(n_dec,),
            in_specs=[row, row, row, pl.BlockSpec(tnear.shape, lambda b, *_: (0, 0)), smem, smem, any_, any_],
            out_specs=row,
            scratch_shapes=[pltpu.VMEM((2, n_slots, HEAD_DIM, PAGE_SIZE), F32),
                            pltpu.SemaphoreType.DMA((2, n_slots))]),
        out_shape=jax.ShapeDtypeStruct((n_dec, 1, D_A), F32),
        compiler_params=_params(1),
        name="attn_sample",
    )(top, page_table, q, kn, vn, tnear, bias0, bfar, kpool_t, vpool_t)


def _mix_sample_kernel(x1_ref, a_ref, u_ref, st_ref, pw_ref, ps_ref, wout_ref, g1_ref, b1_ref, wxq_ref,
                       x2_ref, qx_ref, *, alpha, pos):
    u = u_ref[...]

    def window_sum(g, w):
        lanes = slice(g * POOL_GROUP, (g + 1) * POOL_GROUP)
        s = u[:, lanes]
        for back in range(1, w):
            s = s + st_ref[POOL_STATE - back, :, lanes]
        return s

    p = _pool_project(window_sum, u, lambda w: float(min(w, pos + 1)), pw_ref, ps_ref)
    x2 = _out_project(x1_ref[...], a_ref[...].astype(BF16), p, wout_ref, g1_ref, b1_ref, alpha)
    x2_ref[...] = x2
    qx = jnp.dot(x2.astype(BF16), wxq_ref[...], preferred_element_type=F32) * HEAD_DIM_X ** -0.5
    qx_ref[...] = qx.astype(BF16)


def _mix_sample_call(x1, a, u, state_t, pw, ps, wout, g1, b1, wxq, *, alpha, pos):
    n = x1.shape[0]
    args = (x1, a, u, state_t, pw, ps, wout, g1, b1, wxq)
    return pl.pallas_call(
        functools.partial(_mix_sample_kernel, alpha=alpha, pos=pos),
        grid=(1,),
        in_specs=[_resident(v.shape) for v in args],
        out_specs=[_resident((n, D_MODEL)), _resident((n, D_MODEL))],
        out_shape=[jax.ShapeDtypeStruct((n, D_MODEL), F32), jax.ShapeDtypeStruct((n, D_MODEL), BF16)],
        compiler_params=_params(1),
        name="mix_sample",
    )(*args)


def _xattn_sample_kernel(qx_ref, mk_ref, mv_ref, o_ref):
    qm = _head_rows_x(qx_ref[0])
    logits = lax.dot_general(qm, mk_ref[0].astype(BF16), _NT, preferred_element_type=F32)
    e = jnp.exp(logits - jnp.max(logits, axis=-1, keepdims=True))
    o = jnp.dot(e.astype(BF16), mv_ref[0].astype(BF16), preferred_element_type=F32)
    o = o / jnp.sum(e, axis=-1, keepdims=True)
    row = lax.broadcasted_iota(jnp.int32, o.shape, 0)
    lane_head = lax.broadcasted_iota(jnp.int32, o.shape, 1) // HEAD_DIM_X
    o_ref[0] = jnp.sum(jnp.where(lane_head == row, o, 0.0), axis=0, keepdims=True)


def _head_rows_x(q_row):
    return _mask_heads(q_row, (SUBLANES, D_MODEL), HEAD_DIM_X)


def _xattn_sample_call(qx, mk, mv):
    n_dec, n_mem, _ = mk.shape
    row = pl.BlockSpec((1, 1, D_MODEL), lambda b: (b, 0, 0))
    mem = pl.BlockSpec((1, n_mem, D_MODEL), lambda b: (b, 0, 0))
    return pl.pallas_call(
        _xattn_sample_kernel,
        grid=(n_dec,),
        in_specs=[row, mem, mem],
        out_specs=row,
        out_shape=jax.ShapeDtypeStruct((n_dec, 1, D_MODEL), F32),
        compiler_params=_params(1),
        name="xattn_sample",
    )(qx, mk, mv)


def _proj_ln_kernel(x_ref, o_ref, w_ref, g_ref, b_ref, y_ref, *, alpha):
    proj = jnp.dot(o_ref[...].astype(BF16), w_ref[...], preferred_element_type=F32)
    y_ref[...] = _layer_norm(alpha * x_ref[...] + proj, g_ref[...], b_ref[...])


def _proj_ln_call(x, o, w, g, b, *, alpha):
    args = (x, o, w, g, b)
    return pl.pallas_call(
        functools.partial(_proj_ln_kernel, alpha=alpha),
        grid=(1,),
        in_specs=[_resident(v.shape) for v in args],
        out_specs=_resident(x.shape),
        out_shape=jax.ShapeDtypeStruct(x.shape, F32),
        compiler_params=_params(1),
        name="proj_ln",
    )(*args)


def _t5_bucket(dist):
    n = jnp.maximum(dist, 0)
    max_exact = N_BUCKETS // 2
    nf = jnp.maximum(n, 1).astype(F32)
    large = max_exact + (jnp.log(nf / max_exact) / math.log(MAX_DISTANCE / max_exact)
                         * (N_BUCKETS - max_exact)).astype(jnp.int32)
    return jnp.where(n < max_exact, n, jnp.minimum(large, N_BUCKETS - 1))


def kernel(x_prompt, x_sample, cache_k, cache_v, cache_mem_k, cache_mem_v, state_pool, page_table, mem_prompt, rel_bias, ln_g, ln_b, w_ff1_gate, w_ff1_up, w_ff1_down, w_in, pool_w, pool_scale, w_out, w_xq, w_xk, w_xv, w_xo, w_ff2_gate, w_ff2_up, w_ff2_down):
    n_batch, seq, _ = x_prompt.shape
    n_dec, dec_seq, _ = x_sample.shape
    depth = ln_g.shape[0]
    n_pool = cache_k.shape[1]
    n_pages = page_table.shape[1]
    past_len = n_pages * PAGE_SIZE
    n_mem = mem_prompt.shape[1]
    assert depth == 1 and dec_seq == 1
    assert seq % MOBA_BLOCK == 0 and past_len % MOBA_BLOCK == 0
    assert past_len // MOBA_BLOCK >= MOBA_TOPK
    alpha = (2 * depth) ** 0.25
    tm = min(512, seq)
    n_blk = seq // MOBA_BLOCK

    bf = lambda w: w[0].astype(BF16)
    ff1 = (bf(w_ff1_gate), bf(w_ff1_up), bf(w_ff1_down))
    ff2 = (bf(w_ff2_gate), bf(w_ff2_up), bf(w_ff2_down))
    win = bf(w_in)
    ln = lambda i: (ln_g[0, i:i + 1], ln_b[0, i:i + 1])
    (g0, b0), (g1, b1), (g2, b2), (g3, b3) = ln(0), ln(1), ln(2), ln(3)
    pw, ps = pool_w[0].astype(BF16), pool_scale
    wout, wxq, wxo = bf(w_out), bf(w_xq), bf(w_xo)

    bias_by_dist = rel_bias[_t5_bucket(jnp.arange(2 * MOBA_BLOCK, dtype=jnp.int32))].T
    bfar = rel_bias[_far_bucket()]

    xp = x_prompt.reshape(n_batch * seq, D_MODEL)
    attn_scale = HEAD_DIM ** -0.5
    x1, q, kt, vt, u, kb, vtb, km = _ffn_proj_call(xp, *ff1, g0, b0, win, alpha=alpha, q_scale=attn_scale * LOG2E,
                                                   tm=tm, moba_batch=n_batch)
    kpool_t = cache_k[0].transpose(0, 2, 3, 1).reshape(n_pool, D_A, PAGE_SIZE)
    vpool_t = cache_v[0].transpose(0, 2, 3, 1).reshape(n_pool, D_A, PAGE_SIZE)
    a, kmean_t = _moba_call(rel_bias.reshape(-1) * LOG2E, page_table, q.reshape(n_batch, seq, D_A),
                            kb.reshape(n_batch, seq, D_A), vtb, km.reshape(n_batch, n_blk, D_A), kpool_t)
    mk, mv, mkb, mvb = _memkv_call(mem_prompt.reshape(n_batch * n_mem, D_MODEL), bf(w_xk), bf(w_xv),
                                   tm=min(512, n_batch * n_mem))
    x3 = _mix_call(x1, a.reshape(n_batch * seq, D_A), u, pw, ps, wout, g1, b1, wxq, wxo,
                   mkb.reshape(n_batch, n_mem, D_MODEL), mvb.reshape(n_batch, n_mem, D_MODEL), g2, b2,
                   alpha=alpha, n_batch=n_batch, tm=tm)
    y_prompt = _ffn_call(x3, *ff2, g3, b3, alpha=alpha, tm=tm).reshape(n_batch, seq, D_MODEL)

    to_heads = lambda t: t.reshape(t.shape[0], N_HEADS_A, HEAD_DIM, t.shape[2]).transpose(0, 3, 1, 2)[None]
    k_prompt, v_prompt = to_heads(kt), to_heads(vt)
    pool_prompt = u.reshape(n_batch, seq, D_POOL)[None, :, seq - POOL_STATE:]
    memk_prompt = mk.reshape(1, n_batch, n_mem, N_HEADS_X, HEAD_DIM_X)
    memv_prompt = mv.reshape(1, n_batch, n_mem, N_HEADS_X, HEAD_DIM_X)

    xs = x_sample.reshape(n_dec, D_MODEL)
    x1s, qs, kn, vn, us = _ffn_proj_call(xs, *ff1, g0, b0, win, alpha=alpha, q_scale=attn_scale, tm=n_dec)
    n_blk_s = past_len // MOBA_BLOCK
    qs3 = qs.reshape(n_dec, 1, D_A)
    top = _gate_call(qs3, kmean_t, n_blk_s)[:, :, :MOBA_TOPK].reshape(-1)
    tnear = bias_by_dist[:, MOBA_BLOCK - jnp.arange(MOBA_BLOCK)]
    a_s = _sample_attn_call(top, page_table, qs3, kn.reshape(n_dec, 1, D_A), vn.reshape(n_dec, 1, D_A),
                            tnear, bias_by_dist[:, 0], bfar, kpool_t, vpool_t, n_blk_s)
    state_t = state_pool[0].transpose(1, 0, 2)
    x2s, qxs = _mix_sample_call(x1s, a_s.reshape(n_dec, D_A), us, state_t, pw, ps, wout, g1, b1, wxq,
                                alpha=alpha, pos=past_len)
    o_s = _xattn_sample_call(qxs.reshape(n_dec, 1, D_MODEL),
                             cache_mem_k[0].reshape(n_dec, n_mem, D_MODEL),
                             cache_mem_v[0].reshape(n_dec, n_mem, D_MODEL))
    x3s = _proj_ln_call(x2s, o_s.reshape(n_dec, D_MODEL), wxo, g2, b2, alpha=alpha)
    y_sample = _ffn_call(x3s, *ff2, g3, b3, alpha=alpha, tm=n_dec).reshape(n_dec, 1, D_MODEL)

    k_sample = kn.reshape(1, n_dec, 1, N_HEADS_A, HEAD_DIM)
    v_sample = vn.reshape(1, n_dec, 1, N_HEADS_A, HEAD_DIM)
    pool_sample = jnp.concatenate([state_t[1:], us[None]], axis=0).transpose(1, 0, 2)[None]
    return (y_prompt, y_sample, k_prompt, v_prompt, pool_prompt, memk_prompt, memv_prompt,
            k_sample, v_sample, pool_sample)
```

```python
import functools
import math

import jax
import jax.numpy as jnp
import numpy as np
from jax import lax
from jax.experimental import pallas as pl
from jax.experimental.pallas import tpu as pltpu

F32 = jnp.float32
BF16 = jnp.bfloat16

D_MODEL = 1024
HEAD_DIM = 64
N_HEADS_A = 8
D_A = N_HEADS_A * HEAD_DIM
MOBA_BLOCK = 256
MOBA_TOPK = 3
D_POOL = D_MODEL - D_A
POOL_WINDOWS = (2, 4, 8, 16)
POOL_GROUP = D_POOL // len(POOL_WINDOWS)
POOL_STATE = max(POOL_WINDOWS) - 1
N_BUCKETS = 32
MAX_DISTANCE = 128
N_HEADS_X = 4
HEAD_DIM_X = D_MODEL // N_HEADS_X
D_FF = 2816
LN_EPS = 1e-5
PAGE_SIZE = 128
NEG = -1e30

LANES = 128
SUBLANES = 8
VMEM_LIMIT_BYTES = 56 * 1024 * 1024

BF16_SUBLANES = 16
HEADS_PER_TILE = LANES // HEAD_DIM
V_ROWS = HEAD_DIM + BF16_SUBLANES
LOG2E = math.log2(math.e)
POOL_HALO = POOL_STATE + 1
PAGES_PER_BLOCK = MOBA_BLOCK // PAGE_SIZE
FFN_CHUNK = 256
KMEAN_RING = 16
SCORE_CHUNK = 32

_NT = (((1,), (1,)), ((), ()))


def _params(n_grid_dims):
    return pltpu.CompilerParams(
        dimension_semantics=("arbitrary",) * n_grid_dims, vmem_limit_bytes=VMEM_LIMIT_BYTES)


def _resident(shape):
    return pl.BlockSpec(shape, lambda *_: (0,) * len(shape), pipeline_mode=pl.Buffered(1))


def _layer_norm(z, g, b):
    mu = jnp.mean(z, axis=-1, keepdims=True)
    zc = z - mu
    var = jnp.mean(zc * zc, axis=-1, keepdims=True)
    return zc * lax.rsqrt(var + LN_EPS) * g + b


def _swiglu(x, wg_ref, wu_ref, wd_ref):
    xb = x.astype(BF16)
    acc = None
    for c in range(D_FF // FFN_CHUNK):
        sl = slice(c * FFN_CHUNK, (c + 1) * FFN_CHUNK)
        g = jnp.dot(xb, wg_ref[:, sl], preferred_element_type=F32)
        u = jnp.dot(xb, wu_ref[:, sl], preferred_element_type=F32)
        h = (g * jax.nn.sigmoid(g) * u).astype(BF16)
        part = jnp.dot(h, wd_ref[sl, :], preferred_element_type=F32)
        acc = part if acc is None else acc + part
    return acc


def _ffn_kernel(x_ref, wg_ref, wu_ref, wd_ref, g_ref, b_ref, y_ref, *, alpha):
    x = x_ref[...]
    y_ref[...] = _layer_norm(alpha * x + 0.5 * _swiglu(x, wg_ref, wu_ref, wd_ref), g_ref[...], b_ref[...])


def _ffn_proj_kernel(x_ref, wg_ref, wu_ref, wd_ref, g_ref, b_ref, win_ref,
                     x1_ref, q_ref, k_ref, v_ref, u_ref, *moba_refs, alpha, q_scale):
    x = x_ref[...]
    x1 = _layer_norm(alpha * x + 0.5 * _swiglu(x, wg_ref, wu_ref, wd_ref), g_ref[...], b_ref[...])
    x1_ref[...] = x1
    h = jnp.dot(x1.astype(BF16), win_ref[...], preferred_element_type=F32)
    q_ref[...] = (h[:, :D_A] * q_scale).astype(BF16)
    k = h[:, D_A:2 * D_A]
    v = h[:, 2 * D_A:3 * D_A]
    u_ref[...] = h[:, 3 * D_A:]
    if not moba_refs:
        k_ref[...] = k
        v_ref[...] = v
        return
    kb_ref, vtb_ref, km_ref = moba_refs
    vt = v.T
    k_ref[0] = k.T
    v_ref[0] = vt
    kb_ref[...] = k.astype(BF16)
    ones = jnp.ones((BF16_SUBLANES, MOBA_BLOCK), BF16)
    for blk in range(k.shape[0] // MOBA_BLOCK):
        rows = slice(blk * MOBA_BLOCK, (blk + 1) * MOBA_BLOCK)
        for hd in range(N_HEADS_A):
            vtb_ref[0, blk, hd * V_ROWS:hd * V_ROWS + HEAD_DIM, :] = (
                vt[hd * HEAD_DIM:(hd + 1) * HEAD_DIM, rows].astype(BF16))
            vtb_ref[0, blk, hd * V_ROWS + HEAD_DIM:(hd + 1) * V_ROWS, :] = ones
        km_ref[blk] = jnp.mean(k[rows], axis=0, keepdims=True)


def _ffn_call(x, wg, wu, wd, g, b, *, alpha, tm):
    n = x.shape[0]
    row = pl.BlockSpec((tm, D_MODEL), lambda i: (i, 0))
    return pl.pallas_call(
        functools.partial(_ffn_kernel, alpha=alpha),
        grid=(n // tm,),
        in_specs=[row, _resident(wg.shape), _resident(wu.shape), _resident(wd.shape),
                  _resident(g.shape), _resident(b.shape)],
        out_specs=row,
        out_shape=jax.ShapeDtypeStruct((n, D_MODEL), F32),
        compiler_params=_params(1),
        name="ffn",
    )(x, wg, wu, wd, g, b)


def _ffn_proj_call(x, wg, wu, wd, g, b, win, *, alpha, q_scale, tm, moba_batch=None):
    n = x.shape[0]
    row = lambda w: pl.BlockSpec((tm, w), lambda i: (i, 0))
    sds = jax.ShapeDtypeStruct
    out_specs = [row(D_MODEL), row(D_A)]
    out_shape = [sds((n, D_MODEL), F32), sds((n, D_A), BF16)]
    if moba_batch is None:
        out_specs += [row(D_A), row(D_A), row(D_POOL)]
        out_shape += [sds((n, D_A), F32), sds((n, D_A), F32), sds((n, D_POOL), F32)]
    else:
        seq = n // moba_batch
        tpb = seq // tm
        bpt = tm // MOBA_BLOCK
        tcol = pl.BlockSpec((1, D_A, tm), lambda i: (i // tpb, 0, i % tpb))
        out_specs += [tcol, tcol, row(D_POOL), row(D_A),
                      pl.BlockSpec((1, bpt, N_HEADS_A * V_ROWS, MOBA_BLOCK), lambda i: (i // tpb, i % tpb, 0, 0)),
                      pl.BlockSpec((bpt, 1, D_A), lambda i: (i, 0, 0))]
        out_shape += [sds((moba_batch, D_A, seq), F32), sds((moba_batch, D_A, seq), F32), sds((n, D_POOL), F32),
                      sds((n, D_A), BF16),
                      sds((moba_batch, seq // MOBA_BLOCK, N_HEADS_A * V_ROWS, MOBA_BLOCK), BF16),
                      sds((n // MOBA_BLOCK, 1, D_A), F32)]
    return pl.pallas_call(
        functools.partial(_ffn_proj_kernel, alpha=alpha, q_scale=q_scale),
        grid=(n // tm,),
        in_specs=[row(D_MODEL), _resident(wg.shape), _resident(wu.shape), _resident(wd.shape),
                  _resident(g.shape), _resident(b.shape), _resident(win.shape)],
        out_specs=out_specs,
        out_shape=out_shape,
        compiler_params=_params(1),
        name="ffn_proj",
    )(x, wg, wu, wd, g, b, win)


def _memkv_kernel(m_ref, wk_ref, wv_ref, k_ref, v_ref, kb_ref, vb_ref):
    mb = m_ref[...].astype(BF16)
    k = jnp.dot(mb, wk_ref[...], preferred_element_type=F32)
    v = jnp.dot(mb, wv_ref[...], preferred_element_type=F32)
    k_ref[...] = k
    v_ref[...] = v
    kb_ref[...] = k.astype(BF16)
    vb_ref[...] = v.astype(BF16)


def _memkv_call(mem, wk, wv, *, tm):
    n = mem.shape[0]
    row = pl.BlockSpec((tm, D_MODEL), lambda i: (i, 0))
    sds = lambda dt: jax.ShapeDtypeStruct((n, D_MODEL), dt)
    return pl.pallas_call(
        _memkv_kernel,
        grid=(n // tm,),
        in_specs=[row, _resident(wk.shape), _resident(wv.shape)],
        out_specs=[row, row, row, row],
        out_shape=[sds(F32), sds(F32), sds(BF16), sds(BF16)],
        compiler_params=_params(1),
        name="memkv",
    )(mem, wk, wv)


def _top3_rows(gate, n_valid, n_rows):
    rid = lax.broadcasted_iota(jnp.int32, gate.shape, 0).astype(F32)
    g = jnp.where(rid < n_valid, gate, NEG)
    sel = jnp.zeros(gate.shape, jnp.bool_)
    for _ in range(MOBA_TOPK):
        m = jnp.max(g, axis=0, keepdims=True)
        idx = jnp.min(jnp.where(g == m, rid, float(n_rows)), axis=0, keepdims=True)
        pick = rid == idx
        sel = jnp.logical_or(sel, jnp.logical_and(pick, idx < n_valid))
        g = jnp.where(pick, -jnp.inf, g)
    return sel


def _moba_kernel(rb_ref, pt_ref, q_ref, kb_ref, vtb_ref, km_ref, bkt_ref, kpool_ref, o_ref, kmo_ref,
                 tbl_ref, rbt_ref, qh_ref, s_ref, p_ref, acc_ref, m_ref, smx_ref, al_ref,
                 ring_ref, ring_sem, kacc_ref, out_sem, *, n_blk, far_bucket, n_dec, n_pool_blk_per_dec):
    b = pl.program_id(0)
    i = pl.program_id(1)
    row_zero, row_prev = n_blk, n_blk + 1
    n_chunk = MOBA_BLOCK // SCORE_CHUNK
    chunks = [slice(c * SCORE_CHUNK, (c + 1) * SCORE_CHUNK) for c in range(n_chunk)]
    head_lanes = lambda h: slice((h // HEADS_PER_TILE) * LANES, (h // HEADS_PER_TILE + 1) * LANES)

    @pl.when(jnp.logical_and(b == 0, i == 0))
    def _build_bias_tiles():
        def per_head(h, _):
            for slot in range(2):
                bk = bkt_ref[slot]
                t = jnp.zeros(bk.shape, F32)
                for bucket in range(N_BUCKETS):
                    t = jnp.where(bk == bucket, rb_ref[bucket * N_HEADS_A + h], t)
                tbl_ref[h, slot] = jnp.where(bk < 0, NEG, t)
            return 0
        lax.fori_loop(0, N_HEADS_A, per_head, 0)

    q = q_ref[0]
    km = km_ref[0]
    i_f = i.astype(F32)
    jp_f = jnp.maximum(i - 1, 0).astype(F32)
    lane_head = lax.broadcasted_iota(jnp.int32, (MOBA_BLOCK, LANES), 1) // HEAD_DIM
    rid = lax.broadcasted_iota(jnp.int32, (n_blk, MOBA_BLOCK), 0).astype(F32)
    for h in range(N_HEADS_A):
        q2 = q[:, head_lanes(h)].astype(F32)
        qh = jnp.where(lane_head == h % HEADS_PER_TILE, q2, 0.0).astype(BF16)
        qh_ref[h] = qh
        gate = lax.dot_general(km[:, head_lanes(h)].astype(BF16), qh, _NT, preferred_element_type=F32)
        sel = _top3_rows(gate, i_f, n_blk)
        rbt_ref[h, :n_blk] = jnp.where(sel, rb_ref[far_bucket * N_HEADS_A + h], NEG)
        prev_sel = jnp.max(jnp.where(jnp.logical_and(sel, rid == jp_f), 1.0, 0.0), axis=0, keepdims=True)
        rbt_ref[h, row_zero:row_zero + 1] = jnp.zeros((1, MOBA_BLOCK), F32)
        rbt_ref[h, row_prev:row_prev + 1] = jnp.where(prev_sel > 0.5, 0.0, NEG)
        m_ref[h] = jnp.full((1, MOBA_BLOCK), NEG, F32)
        al_ref[h] = jnp.ones((1, MOBA_BLOCK), F32)
        acc_ref[h] = jnp.zeros((V_ROWS, MOBA_BLOCK), F32)
        p_ref[h] = jnp.zeros((MOBA_BLOCK, MOBA_BLOCK), BF16)

    def list_block(t):
        return jnp.clip(jnp.where(t == 0, i, jnp.where(t == 1, i - 1, t - 2)), 0, n_blk - 1)

    def stage_a(t):
        rows = pl.ds(pl.multiple_of(list_block(t) * MOBA_BLOCK, MOBA_BLOCK), MOBA_BLOCK)
        for h in range(N_HEADS_A):
            s = lax.dot_general(kb_ref[0, rows, head_lanes(h)], qh_ref[h], _NT, preferred_element_type=F32)
            s_ref[h] = s
            smx_ref[h] = jnp.max(s, axis=0, keepdims=True)

    def add_bias_tile(slot):
        for h in range(N_HEADS_A):
            s = s_ref[h] + tbl_ref[h, slot]
            s_ref[h] = s
            smx_ref[h] = jnp.max(s, axis=0, keepdims=True)

    def stage_b(t):
        row = jnp.where(t == 0, row_zero, jnp.where(t == 1, row_prev, t - 2))
        for h in range(N_HEADS_A):
            rb = rbt_ref[h, pl.ds(row, 1), :]
            m_old = m_ref[h]
            m_new = jnp.maximum(m_old, smx_ref[h] + rb)
            shift = m_new - rb
            for c in chunks:
                p_ref[h, c, :] = jnp.exp2(s_ref[h, c, :] - shift).astype(BF16)
            m_ref[h] = m_new
            al_ref[h] = jnp.exp2(m_old - m_new)

    def stage_c(t):
        j = list_block(t)
        for h in range(N_HEADS_A):
            vt = vtb_ref[0, j, h * V_ROWS:(h + 1) * V_ROWS, :]
            acc_ref[h] = al_ref[h] * acc_ref[h] + jnp.dot(vt, p_ref[h], preferred_element_type=F32)

    n_pool_blk = n_dec * n_pool_blk_per_dec
    n_pool_pages = n_pool_blk * PAGES_PER_BLOCK
    ring = min(KMEAN_RING, n_pool_pages)
    iters_per_batch = n_blk * (n_blk + 1) // 2
    lane = lax.broadcasted_iota(jnp.int32, (D_A, LANES), 1)

    def page_copy(page_no, slot):
        return pltpu.make_async_copy(kpool_ref.at[pt_ref[page_no]], ring_ref.at[slot], ring_sem.at[slot])

    def pool_block_mean(g):
        g = jnp.minimum(g, n_pool_blk - 1)
        total = None
        for half in range(PAGES_PER_BLOCK):
            page_no = g * PAGES_PER_BLOCK + half
            slot = page_no % ring
            page_copy(page_no, slot).wait()
            x = ring_ref[slot]
            total = x if total is None else total + x
            nxt = jnp.where(page_no + ring < n_pool_pages, page_no + ring, page_no)
            page_copy(nxt, slot).start()
        col = jnp.sum(total, axis=1, keepdims=True) * (1.0 / MOBA_BLOCK)
        dec, j = g // n_pool_blk_per_dec, g % n_pool_blk_per_dec
        kacc_ref[dec] = jnp.where(lane == j, col, kacc_ref[dec])

    @pl.when(jnp.logical_and(b == 0, i == 0))
    def _start_ring():
        for slot in range(ring):
            page_copy(slot, slot).start()
        kacc_ref[...] = jnp.zeros(kacc_ref.shape, F32)

    stage_a(0)
    add_bias_tile(0)
    g0 = b * iters_per_batch + i * (i + 1) // 2

    def step(t, carry):
        pool_block_mean(g0 + t - 1)
        stage_c(jnp.maximum(t - 2, 0))
        stage_b(t - 1)
        stage_a(t)

        @pl.when(t == 1)
        def _():
            add_bias_tile(1)

        return carry

    lax.fori_loop(1, i + 1, step, 0)
    pool_block_mean(g0 + i)
    stage_c(jnp.maximum(i - 1, 0))
    stage_b(i)
    stage_c(i)
    out_t = jnp.concatenate([acc_ref[h, :HEAD_DIM] / acc_ref[h, HEAD_DIM:HEAD_DIM + 1]
                             for h in range(N_HEADS_A)], axis=0)
    o_ref[0] = out_t.T.astype(BF16)

    @pl.when(jnp.logical_and(b == pl.num_programs(0) - 1, i == n_blk - 1))
    def _finish_pool_means():
        n_done = pl.num_programs(0) * iters_per_batch

        def rest(g, carry):
            pool_block_mean(g)
            return carry

        lax.fori_loop(jnp.minimum(n_done, n_pool_blk), n_pool_blk, rest, 0)
        for slot in range(ring):
            page_copy(slot, slot).wait()
        out_copy = pltpu.make_async_copy(kacc_ref, kmo_ref, out_sem.at[0])
        out_copy.start()
        out_copy.wait()


def _bucket_np(dist):
    n = np.maximum(dist, 0)
    max_exact = N_BUCKETS // 2
    nf = np.maximum(n, 1).astype(np.float32)
    large = max_exact + (np.log(nf / np.float32(max_exact)) / np.float32(math.log(MAX_DISTANCE / max_exact))
                         * np.float32(N_BUCKETS - max_exact)).astype(np.int32)
    return np.where(n < max_exact, n, np.minimum(large, N_BUCKETS - 1)).astype(np.int32)


def _bucket_tiles():
    key = np.arange(MOBA_BLOCK)[:, None]
    qry = np.arange(MOBA_BLOCK)[None, :]
    own = np.where(qry >= key, _bucket_np(qry - key), -1)
    prev = _bucket_np(qry - key + MOBA_BLOCK)
    return np.stack([own, prev]).astype(np.int32)


def _far_bucket():
    far = _bucket_np(np.array([MOBA_BLOCK + 1, 1 << 30]))
    assert far[0] == far[1]
    return int(far[0])


def _moba_call(rel_bias_flat, page_table, q, kb, vtb, km, kpool_t):
    n_batch, seq, _ = q.shape
    n_blk = seq // MOBA_BLOCK
    n_dec, n_pages = page_table.shape
    assert n_pages % PAGES_PER_BLOCK == 0 and n_pages // PAGES_PER_BLOCK <= LANES
    qo = pl.BlockSpec((1, MOBA_BLOCK, D_A), lambda b, i, *_: (b, i, 0))

    def per_batch(shape, **kw):
        return pl.BlockSpec((1,) + shape, lambda b, i, *_: (b,) + (0,) * len(shape), **kw)

    once = dict(pipeline_mode=pl.Buffered(1))
    any_ = pl.BlockSpec(memory_space=pl.ANY)
    tile = (MOBA_BLOCK, MOBA_BLOCK)
    stat = pltpu.VMEM((N_HEADS_A, 1, MOBA_BLOCK), F32)
    return pl.pallas_call(
        functools.partial(_moba_kernel, n_blk=n_blk, far_bucket=_far_bucket(), n_dec=n_dec,
                          n_pool_blk_per_dec=n_pages // PAGES_PER_BLOCK),
        grid_spec=pltpu.PrefetchScalarGridSpec(
            num_scalar_prefetch=2,
            grid=(n_batch, n_blk),
            in_specs=[qo, per_batch((seq, D_A), **once),
                      per_batch((n_blk, N_HEADS_A * V_ROWS, MOBA_BLOCK), **once),
                      per_batch((n_blk, D_A)), _resident((2,) + tile), any_],
            out_specs=[qo, any_],
            scratch_shapes=[pltpu.VMEM((N_HEADS_A, 2) + tile, F32),
                            pltpu.VMEM((N_HEADS_A, n_blk + SUBLANES, MOBA_BLOCK), F32),
                            pltpu.VMEM((N_HEADS_A, MOBA_BLOCK, LANES), BF16),
                            pltpu.VMEM((N_HEADS_A,) + tile, F32),
                            pltpu.VMEM((N_HEADS_A,) + tile, BF16),
                            pltpu.VMEM((N_HEADS_A, V_ROWS, MOBA_BLOCK), F32),
                            stat, stat, stat,
                            pltpu.VMEM((KMEAN_RING, D_A, PAGE_SIZE), F32),
                            pltpu.SemaphoreType.DMA((KMEAN_RING,)),
                            pltpu.VMEM((n_dec, D_A, LANES), F32),
                            pltpu.SemaphoreType.DMA((1,))]),
        out_shape=[jax.ShapeDtypeStruct((n_batch, seq, D_A), BF16),
                   jax.ShapeDtypeStruct((n_dec, D_A, LANES), F32)],
        compiler_params=_params(2),
        name="moba_prompt",
    )(rel_bias_flat, page_table.reshape(-1), q, kb, vtb, km, jnp.asarray(_bucket_tiles()), kpool_t)


def _pool_project(window_sum, u_new, cnt, pw_ref, ps_ref):
    ys = []
    for g, w in enumerate(POOL_WINDOWS):
        lanes = slice(g * POOL_GROUP, (g + 1) * POOL_GROUP)
        d = window_sum(g, w) / cnt(w) - u_new[:, lanes]
        ys.append(jnp.dot(d.astype(BF16), pw_ref[g], preferred_element_type=F32))
    return jnp.concatenate(ys, axis=1) * ps_ref[...]


def _out_project(x1, a, p, wout_ref, g_ref, b_ref, alpha):
    proj = (jnp.dot(a, wout_ref[:D_A, :], preferred_element_type=F32)
            + jnp.dot(p.astype(BF16), wout_ref[D_A:, :], preferred_element_type=F32))
    return _layer_norm(alpha * x1 + proj, g_ref[...], b_ref[...])


def _mix_kernel(x1_ref, a_ref, u_ref, uh_ref, pw_ref, ps_ref, wout_ref, g1_ref, b1_ref,
                wxq_ref, wxo_ref, mk_ref, mv_ref, g2_ref, b2_ref, x3_ref, ext_ref, *, alpha, tm, tpb):
    t_in_b = pl.program_id(0) % tpb
    ext_ref[:POOL_HALO, :] = jnp.where(t_in_b == 0, 0.0, uh_ref[...])
    ext_ref[POOL_HALO:, :] = u_ref[...]
    pos = t_in_b * tm + lax.broadcasted_iota(jnp.int32, (tm, 1), 0)

    def window_sum(g, w):
        lanes = slice(g * POOL_GROUP, (g + 1) * POOL_GROUP)
        s = ext_ref[POOL_HALO:POOL_HALO + tm, lanes]
        for back in range(1, w):
            s = s + ext_ref[POOL_HALO - back:POOL_HALO - back + tm, lanes]
        return s

    p = _pool_project(window_sum, u_ref[...], lambda w: jnp.minimum(w, pos + 1).astype(F32), pw_ref, ps_ref)
    x2 = _out_project(x1_ref[...], a_ref[...], p, wout_ref, g1_ref, b1_ref, alpha)

    qx = jnp.dot(x2.astype(BF16), wxq_ref[...], preferred_element_type=F32) * HEAD_DIM_X ** -0.5
    outs = []
    for h in range(N_HEADS_X):
        lanes = slice(h * HEAD_DIM_X, (h + 1) * HEAD_DIM_X)
        logits = lax.dot_general(qx[:, lanes].astype(BF16), mk_ref[0, :, lanes], _NT, preferred_element_type=F32)
        e = jnp.exp(logits - jnp.max(logits, axis=-1, keepdims=True))
        o = jnp.dot(e.astype(BF16), mv_ref[0, :, lanes], preferred_element_type=F32)
        outs.append(o / jnp.sum(e, axis=-1, keepdims=True))
    o = jnp.concatenate(outs, axis=1).astype(BF16)
    x3_ref[...] = _layer_norm(alpha * x2 + jnp.dot(o, wxo_ref[...], preferred_element_type=F32),
                              g2_ref[...], b2_ref[...])


def _mix_call(x1, a, u, pw, ps, wout, g1, b1, wxq, wxo, mk, mv, g2, b2, *, alpha, n_batch, tm):
    n = x1.shape[0]
    tpb = n // n_batch // tm
    n_mem = mk.shape[1]
    row = lambda w: pl.BlockSpec((tm, w), lambda i: (i, 0))
    halo = pl.BlockSpec((POOL_HALO, D_POOL), lambda i: (jnp.maximum(i * (tm // POOL_HALO) - 1, 0), 0))
    mem = pl.BlockSpec((1, n_mem, D_MODEL), lambda i: (i // tpb, 0, 0))
    return pl.pallas_call(
        functools.partial(_mix_kernel, alpha=alpha, tm=tm, tpb=tpb),
        grid=(n // tm,),
        in_specs=[row(D_MODEL), row(D_A), row(D_POOL), halo, _resident(pw.shape), _resident(ps.shape),
                  _resident(wout.shape), _resident(g1.shape), _resident(b1.shape),
                  _resident(wxq.shape), _resident(wxo.shape), mem, mem,
                  _resident(g2.shape), _resident(b2.shape)],
        out_specs=row(D_MODEL),
        out_shape=jax.ShapeDtypeStruct((n, D_MODEL), F32),
        scratch_shapes=[pltpu.VMEM((tm + POOL_HALO, D_POOL), F32)],
        compiler_params=_params(1),
        name="mix_prompt",
    )(x1, a, u, u, pw, ps, wout, g1, b1, wxq, wxo, mk, mv, g2, b2)


def _head_rows(q_row, n_rows):
    return _mask_heads(q_row, (n_rows, D_A), HEAD_DIM)


def _mask_heads(q_row, shape, head_dim):
    row = lax.broadcasted_iota(jnp.int32, shape, 0)
    lane_head = lax.broadcasted_iota(jnp.int32, shape, 1) // head_dim
    qb = jnp.broadcast_to(q_row.astype(F32), shape)
    return jnp.where(lane_head == row, qb, 0.0).astype(q_row.dtype)


def _gate_kernel(q_ref, km_ref, o_ref, *, n_blk):
    qm = _head_rows(q_ref[0], N_HEADS_A)
    gate = jnp.dot(qm, km_ref[0].astype(BF16), preferred_element_type=F32)
    lane = lax.broadcasted_iota(jnp.int32, gate.shape, 1)
    lane_f = lane.astype(F32)
    g = jnp.where(lane < n_blk, gate, NEG)
    out = jnp.zeros(gate.shape, F32)
    for t in range(MOBA_TOPK):
        m = jnp.max(g, axis=1, keepdims=True)
        idx = jnp.min(jnp.where(g == m, lane_f, float(LANES)), axis=1, keepdims=True)
        out = jnp.where(lane == t, idx, out)
        g = jnp.where(lane_f == idx, -jnp.inf, g)
    o_ref[0] = out.astype(jnp.int32)


def _gate_call(q, kmean_t, n_blk):
    n_dec = q.shape[0]
    return pl.pallas_call(
        functools.partial(_gate_kernel, n_blk=n_blk),
        grid=(n_dec,),
        in_specs=[pl.BlockSpec((1, 1, D_A), lambda b: (b, 0, 0)),
                  pl.BlockSpec((1, D_A, LANES), lambda b: (b, 0, 0))],
        out_specs=pl.BlockSpec((1, N_HEADS_A, LANES), lambda b: (b, 0, 0)),
        out_shape=jax.ShapeDtypeStruct((n_dec, N_HEADS_A, LANES), jnp.int32),
        compiler_params=_params(1),
        name="gate_sample",
    )(q, kmean_t)


def _head_page_copy(pool_ref, page, h, buf_ref, sem_ref, par, which, slot):
    rows = pl.ds(h * HEAD_DIM, HEAD_DIM)
    return pltpu.make_async_copy(pool_ref.at[page, rows, :], buf_ref.at[par, which, slot],
                                 sem_ref.at[par, which, slot])


def _sample_attn_kernel(top_ref, pt_ref, q_ref, kn_ref, vn_ref, tnear_ref, bias0_ref, bfar_ref,
                        kpool_ref, vpool_ref, o_ref, buf_ref, sem_ref, *, n_blk):
    b = pl.program_id(0)
    n_sel = MOBA_TOPK * PAGES_PER_BLOCK
    par = b % 2

    def copies(row, half_buf, h):
        out = []
        for t in range(MOBA_TOPK):
            blk = top_ref[(row * N_HEADS_A + h) * MOBA_TOPK + t]
            for half in range(PAGES_PER_BLOCK):
                page = pt_ref[row, blk * PAGES_PER_BLOCK + half]
                slot = h * n_sel + t * PAGES_PER_BLOCK + half
                out.append(_head_page_copy(kpool_ref, page, h, buf_ref, sem_ref, half_buf, 0, slot))
                out.append(_head_page_copy(vpool_ref, page, h, buf_ref, sem_ref, half_buf, 1, slot))
        return out

    def start_row(row, half_buf):
        for h in range(N_HEADS_A):
            for c in copies(row, half_buf, h):
                c.start()

    @pl.when(b == 0)
    def _():
        start_row(0, 0)

    @pl.when(b + 1 < pl.num_programs(0))
    def _():
        start_row(b + 1, 1 - par)

    q = q_ref[0]
    kn = kn_ref[0].astype(BF16).astype(F32)
    vn = vn_ref[0].astype(BF16).astype(F32)
    s_new_all = q.astype(F32) * kn
    outs = []
    for h in range(N_HEADS_A):
        for c in copies(b, par, h):
            c.wait()
        lanes = slice(h * HEAD_DIM, (h + 1) * HEAD_DIM)
        qh = jnp.broadcast_to(q[:, lanes], (SUBLANES, HEAD_DIM))
        kt = jnp.concatenate([buf_ref[par, 0, h * n_sel + s] for s in range(n_sel)], axis=1).astype(BF16)
        vt = jnp.concatenate([buf_ref[par, 1, h * n_sel + s] for s in range(n_sel)], axis=1).astype(BF16)
        bias = []
        for t in range(MOBA_TOPK):
            blk = top_ref[(b * N_HEADS_A + h) * MOBA_TOPK + t]
            bias.append(jnp.where(blk == n_blk - 1, tnear_ref[h:h + 1, :], bfar_ref[h]))
        s = jnp.dot(qh, kt, preferred_element_type=F32)[:1] + jnp.concatenate(bias, axis=1)
        s_new = jnp.sum(s_new_all[:, lanes], axis=1, keepdims=True) + bias0_ref[h]
        m = jnp.maximum(jnp.max(s, axis=1, keepdims=True), s_new)
        p = jnp.exp(s - m)
        p_new = jnp.exp(s_new - m)
        l = jnp.sum(p, axis=1, keepdims=True) + p_new
        pb = jnp.broadcast_to(p.astype(BF16), (SUBLANES, p.shape[1]))
        o = lax.dot_general(pb, vt, _NT, preferred_element_type=F32)[:1]
        o = o + p_new.astype(BF16).astype(F32) * vn[:, lanes]
        outs.append(o / l)
    o_ref[0] = jnp.concatenate(outs, axis=1)


def _sample_attn_call(top, page_table, q, kn, vn, tnear, bias0, bfar, kpool_t, vpool_t, n_blk):
    n_dec = q.shape[0]
    row = pl.BlockSpec((1, 1, D_A), lambda b, *_: (b, 0, 0))
    smem = pl.BlockSpec(memory_space=pltpu.SMEM)
    any_ = pl.BlockSpec(memory_space=pl.ANY)
    n_slots = N_HEADS_A * MOBA_TOPK * PAGES_PER_BLOCK
    return pl.pallas_call(
        functools.partial(_sample_attn_kernel, n_blk=n_blk),
        grid_spec=pltpu.PrefetchScalarGridSpec(
            num_scalar_prefetch=2,
            grid=(n_dec,),
            in_specs=[row, row, row, pl.BlockSpec(tnear.shape, lambda b, *_: (0, 0)), smem, smem, any_, any_],
            out_specs=row,
            scratch_shapes=[pltpu.VMEM((2, 2, n_slots, HEAD_DIM, PAGE_SIZE), F32),
                            pltpu.SemaphoreType.DMA((2, 2, n_slots))]),
        out_shape=jax.ShapeDtypeStruct((n_dec, 1, D_A), F32),
        compiler_params=_params(1),
        name="attn_sample",
    )(top, page_table, q, kn, vn, tnear, bias0, bfar, kpool_t, vpool_t)


def _mix_sample_kernel(x1_ref, a_ref, u_ref, st_ref, pw_ref, ps_ref, wout_ref, g1_ref, b1_ref, wxq_ref,
                       x2_ref, qx_ref, *, alpha, pos):
    u = u_ref[...]

    def window_sum(g, w):
        lanes = slice(g * POOL_GROUP, (g + 1) * POOL_GROUP)
        s = u[:, lanes]
        for back in range(1, w):
            s = s + st_ref[POOL_STATE - back, :, lanes]
        return s

    p = _pool_project(window_sum, u, lambda w: float(min(w, pos + 1)), pw_ref, ps_ref)
    x2 = _out_project(x1_ref[...], a_ref[...].astype(BF16), p, wout_ref, g1_ref, b1_ref, alpha)
    x2_ref[...] = x2
    qx = jnp.dot(x2.astype(BF16), wxq_ref[...], preferred_element_type=F32) * HEAD_DIM_X ** -0.5
    qx_ref[...] = qx.astype(BF16)


def _mix_sample_call(x1, a, u, state_t, pw, ps, wout, g1, b1, wxq, *, alpha, pos):
    n = x1.shape[0]
    args = (x1, a, u, state_t, pw, ps, wout, g1, b1, wxq)
    return pl.pallas_call(
        functools.partial(_mix_sample_kernel, alpha=alpha, pos=pos),
        grid=(1,),
        in_specs=[_resident(v.shape) for v in args],
        out_specs=[_resident((n, D_MODEL)), _resident((n, D_MODEL))],
        out_shape=[jax.ShapeDtypeStruct((n, D_MODEL), F32), jax.ShapeDtypeStruct((n, D_MODEL), BF16)],
        compiler_params=_params(1),
        name="mix_sample",
    )(*args)


def _xattn_sample_kernel(qx_ref, mk_ref, mv_ref, o_ref):
    qm = _head_rows_x(qx_ref[0])
    logits = lax.dot_general(qm, mk_ref[0].astype(BF16), _NT, preferred_element_type=F32)
    e = jnp.exp(logits - jnp.max(logits, axis=-1, keepdims=True))
    o = jnp.dot(e.astype(BF16), mv_ref[0].astype(BF16), preferred_element_type=F32)
    o = o / jnp.sum(e, axis=-1, keepdims=True)
    row = lax.broadcasted_iota(jnp.int32, o.shape, 0)
    lane_head = lax.broadcasted_iota(jnp.int32, o.shape, 1) // HEAD_DIM_X
    o_ref[0] = jnp.sum(jnp.where(lane_head == row, o, 0.0), axis=0, keepdims=True)


def _head_rows_x(q_row):
    return _mask_heads(q_row, (SUBLANES, D_MODEL), HEAD_DIM_X)


def _xattn_sample_call(qx, mk, mv):
    n_dec, n_mem, _ = mk.shape
    row = pl.BlockSpec((1, 1, D_MODEL), lambda b: (b, 0, 0))
    mem = pl.BlockSpec((1, n_mem, D_MODEL), lambda b: (b, 0, 0))
    return pl.pallas_call(
        _xattn_sample_kernel,
        grid=(n_dec,),
        in_specs=[row, mem, mem],
        out_specs=row,
        out_shape=jax.ShapeDtypeStruct((n_dec, 1, D_MODEL), F32),
        compiler_params=_params(1),
        name="xattn_sample",
    )(qx, mk, mv)


def _proj_ln_kernel(x_ref, o_ref, w_ref, g_ref, b_ref, y_ref, *, alpha):
    proj = jnp.dot(o_ref[...].astype(BF16), w_ref[...], preferred_element_type=F32)
    y_ref[...] = _layer_norm(alpha * x_ref[...] + proj, g_ref[...], b_ref[...])


def _proj_ln_call(x, o, w, g, b, *, alpha):
    args = (x, o, w, g, b)
    return pl.pallas_call(
        functools.partial(_proj_ln_kernel, alpha=alpha),
        grid=(1,),
        in_specs=[_resident(v.shape) for v in args],
        out_specs=_resident(x.shape),
        out_shape=jax.ShapeDtypeStruct(x.shape, F32),
        compiler_params=_params(1),
        name="proj_ln",
    )(*args)


def _t5_bucket(dist):
    n = jnp.maximum(dist, 0)
    max_exact = N_BUCKETS // 2
    nf = jnp.maximum(n, 1).astype(F32)
    large = max_exact + (jnp.log(nf / max_exact) / math.log(MAX_DISTANCE / max_exact)
                         * (N_BUCKETS - max_exact)).astype(jnp.int32)
    return jnp.where(n < max_exact, n, jnp.minimum(large, N_BUCKETS - 1))


def kernel(x_prompt, x_sample, cache_k, cache_v, cache_mem_k, cache_mem_v, state_pool, page_table, mem_prompt, rel_bias, ln_g, ln_b, w_ff1_gate, w_ff1_up, w_ff1_down, w_in, pool_w, pool_scale, w_out, w_xq, w_xk, w_xv, w_xo, w_ff2_gate, w_ff2_up, w_ff2_down):
    n_batch, seq, _ = x_prompt.shape
    n_dec, dec_seq, _ = x_sample.shape
    depth = ln_g.shape[0]
    n_pool = cache_k.shape[1]
    n_pages = page_table.shape[1]
    past_len = n_pages * PAGE_SIZE
    n_mem = mem_prompt.shape[1]
    assert depth == 1 and dec_seq == 1
    assert seq % MOBA_BLOCK == 0 and past_len % MOBA_BLOCK == 0
    assert past_len // MOBA_BLOCK >= MOBA_TOPK
    alpha = (2 * depth) ** 0.25
    tm = min(512, seq)
    n_blk = seq // MOBA_BLOCK

    bf = lambda w: w[0].astype(BF16)
    ff1 = (bf(w_ff1_gate), bf(w_ff1_up), bf(w_ff1_down))
    ff2 = (bf(w_ff2_gate), bf(w_ff2_up), bf(w_ff2_down))
    win = bf(w_in)
    ln = lambda i: (ln_g[0, i:i + 1], ln_b[0, i:i + 1])
    (g0, b0), (g1, b1), (g2, b2), (g3, b3) = ln(0), ln(1), ln(2), ln(3)
    pw, ps = pool_w[0].astype(BF16), pool_scale
    wout, wxq, wxo = bf(w_out), bf(w_xq), bf(w_xo)

    bias_by_dist = rel_bias[_t5_bucket(jnp.arange(2 * MOBA_BLOCK, dtype=jnp.int32))].T
    bfar = rel_bias[_far_bucket()]

    xp = x_prompt.reshape(n_batch * seq, D_MODEL)
    attn_scale = HEAD_DIM ** -0.5
    x1, q, kt, vt, u, kb, vtb, km = _ffn_proj_call(xp, *ff1, g0, b0, win, alpha=alpha, q_scale=attn_scale * LOG2E,
                                                   tm=tm, moba_batch=n_batch)
    kpool_t = cache_k[0].transpose(0, 2, 3, 1).reshape(n_pool, D_A, PAGE_SIZE)
    vpool_t = cache_v[0].transpose(0, 2, 3, 1).reshape(n_pool, D_A, PAGE_SIZE)
    a, kmean_t = _moba_call(rel_bias.reshape(-1) * LOG2E, page_table, q.reshape(n_batch, seq, D_A),
                            kb.reshape(n_batch, seq, D_A), vtb, km.reshape(n_batch, n_blk, D_A), kpool_t)
    mk, mv, mkb, mvb = _memkv_call(mem_prompt.reshape(n_batch * n_mem, D_MODEL), bf(w_xk), bf(w_xv),
                                   tm=min(512, n_batch * n_mem))
    x3 = _mix_call(x1, a.reshape(n_batch * seq, D_A), u, pw, ps, wout, g1, b1, wxq, wxo,
                   mkb.reshape(n_batch, n_mem, D_MODEL), mvb.reshape(n_batch, n_mem, D_MODEL), g2, b2,
                   alpha=alpha, n_batch=n_batch, tm=tm)
    y_prompt = _ffn_call(x3, *ff2, g3, b3, alpha=alpha, tm=tm).reshape(n_batch, seq, D_MODEL)

    to_heads = lambda t: t.reshape(t.shape[0], N_HEADS_A, HEAD_DIM, t.shape[2]).transpose(0, 3, 1, 2)[None]
    k_prompt, v_prompt = to_heads(kt), to_heads(vt)
    pool_prompt = u.reshape(n_batch, seq, D_POOL)[None, :, seq - POOL_STATE:]
    memk_prompt = mk.reshape(1, n_batch, n_mem, N_HEADS_X, HEAD_DIM_X)
    memv_prompt = mv.reshape(1, n_batch, n_mem, N_HEADS_X, HEAD_DIM_X)

    xs = x_sample.reshape(n_dec, D_MODEL)
    x1s, qs, kn, vn, us = _ffn_proj_call(xs, *ff1, g0, b0, win, alpha=alpha, q_scale=attn_scale, tm=n_dec)
    n_blk_s = past_len // MOBA_BLOCK
    qs3 = qs.reshape(n_dec, 1, D_A)
    top = _gate_call(qs3, kmean_t, n_blk_s)[:, :, :MOBA_TOPK].reshape(-1)
    tnear = bias_by_dist[:, MOBA_BLOCK - jnp.arange(MOBA_BLOCK)]
    a_s = _sample_attn_call(top, page_table, qs3, kn.reshape(n_dec, 1, D_A), vn.reshape(n_dec, 1, D_A),
                            tnear, bias_by_dist[:, 0], bfar, kpool_t, vpool_t, n_blk_s)
    state_t = state_pool[0].transpose(1, 0, 2)
    x2s, qxs = _mix_sample_call(x1s, a_s.reshape(n_dec, D_A), us, state_t, pw, ps, wout, g1, b1, wxq,
                                alpha=alpha, pos=past_len)
    o_s = _xattn_sample_call(qxs.reshape(n_dec, 1, D_MODEL),
                             cache_mem_k[0].reshape(n_dec, n_mem, D_MODEL),
                             cache_mem_v[0].reshape(n_dec, n_mem, D_MODEL))
    x3s = _proj_ln_call(x2s, o_s.reshape(n_dec, D_MODEL), wxo, g2, b2, alpha=alpha)
    y_sample = _ffn_call(x3s, *ff2, g3, b3, alpha=alpha, tm=n_dec).reshape(n_dec, 1, D_MODEL)

    k_sample = kn.reshape(1, n_dec, 1, N_HEADS_A, HEAD_DIM)
    v_sample = vn.reshape(1, n_dec, 1, N_HEADS_A, HEAD_DIM)
    pool_sample = jnp.concatenate([state_t[1:], us[None]], axis=0).transpose(1, 0, 2)[None]
    return (y_prompt, y_sample, k_prompt, v_prompt, pool_prompt, memk_prompt, memv_prompt,
            k_sample, v_sample, pool_sample)
```

```python
import functools
import math

import jax
import jax.numpy as jnp
import numpy as np
from jax import lax
from jax.experimental import pallas as pl
from jax.experimental.pallas import tpu as pltpu

F32 = jnp.float32
BF16 = jnp.bfloat16

D_MODEL = 1024
HEAD_DIM = 64
N_HEADS_A = 8
D_A = N_HEADS_A * HEAD_DIM
MOBA_BLOCK = 256
MOBA_TOPK = 3
D_POOL = D_MODEL - D_A
POOL_WINDOWS = (2, 4, 8, 16)
POOL_GROUP = D_POOL // len(POOL_WINDOWS)
POOL_STATE = max(POOL_WINDOWS) - 1
N_BUCKETS = 32
MAX_DISTANCE = 128
N_HEADS_X = 4
HEAD_DIM_X = D_MODEL // N_HEADS_X
D_FF = 2816
LN_EPS = 1e-5
PAGE_SIZE = 128
NEG = -1e30

LANES = 128
SUBLANES = 8
VMEM_LIMIT_BYTES = 56 * 1024 * 1024

BF16_SUBLANES = 16
HEADS_PER_TILE = LANES // HEAD_DIM
V_ROWS = HEAD_DIM + BF16_SUBLANES
LOG2E = math.log2(math.e)
POOL_HALO = POOL_STATE + 1
PAGES_PER_BLOCK = MOBA_BLOCK // PAGE_SIZE
FFN_CHUNK = 256
KMEAN_RING = 16
SCORE_CHUNK = 32

_NT = (((1,), (1,)), ((), ()))


def _params(n_grid_dims):
    return pltpu.CompilerParams(
        dimension_semantics=("arbitrary",) * n_grid_dims, vmem_limit_bytes=VMEM_LIMIT_BYTES)


def _resident(shape):
    return pl.BlockSpec(shape, lambda *_: (0,) * len(shape), pipeline_mode=pl.Buffered(1))


def _layer_norm(z, g, b):
    mu = jnp.mean(z, axis=-1, keepdims=True)
    zc = z - mu
    var = jnp.mean(zc * zc, axis=-1, keepdims=True)
    return zc * lax.rsqrt(var + LN_EPS) * g + b


def _swiglu(x, wg_ref, wu_ref, wd_ref):
    xb = x.astype(BF16)
    acc = None
    for c in range(D_FF // FFN_CHUNK):
        sl = slice(c * FFN_CHUNK, (c + 1) * FFN_CHUNK)
        g = jnp.dot(xb, wg_ref[:, sl], preferred_element_type=F32)
        u = jnp.dot(xb, wu_ref[:, sl], preferred_element_type=F32)
        h = (g * jax.nn.sigmoid(g) * u).astype(BF16)
        part = jnp.dot(h, wd_ref[sl, :], preferred_element_type=F32)
        acc = part if acc is None else acc + part
    return acc


def _ffn_kernel(x_ref, wg_ref, wu_ref, wd_ref, g_ref, b_ref, y_ref, *, alpha):
    x = x_ref[...]
    y_ref[...] = _layer_norm(alpha * x + 0.5 * _swiglu(x, wg_ref, wu_ref, wd_ref), g_ref[...], b_ref[...])


def _ffn_proj_kernel(x_ref, wg_ref, wu_ref, wd_ref, g_ref, b_ref, win_ref,
                     x1_ref, q_ref, k_ref, v_ref, u_ref, *moba_refs, alpha, q_scale):
    x = x_ref[...]
    x1 = _layer_norm(alpha * x + 0.5 * _swiglu(x, wg_ref, wu_ref, wd_ref), g_ref[...], b_ref[...])
    x1_ref[...] = x1
    h = jnp.dot(x1.astype(BF16), win_ref[...], preferred_element_type=F32)
    q = h[:, :D_A] * q_scale
    k = h[:, D_A:2 * D_A]
    v = h[:, 2 * D_A:3 * D_A]
    u_ref[...] = h[:, 3 * D_A:]
    if not moba_refs:
        q_ref[...] = q.astype(BF16)
        k_ref[...] = k
        v_ref[...] = v
        return
    kb_ref, vtb_ref, km_ref = moba_refs
    vt = v.T
    q_ref[0] = q.T.astype(BF16)
    k_ref[0] = k.T
    v_ref[0] = vt
    kb_ref[...] = k.astype(BF16)
    ones = jnp.ones((BF16_SUBLANES, MOBA_BLOCK), BF16)
    for blk in range(k.shape[0] // MOBA_BLOCK):
        rows = slice(blk * MOBA_BLOCK, (blk + 1) * MOBA_BLOCK)
        for hd in range(N_HEADS_A):
            vtb_ref[0, blk, hd * V_ROWS:hd * V_ROWS + HEAD_DIM, :] = (
                vt[hd * HEAD_DIM:(hd + 1) * HEAD_DIM, rows].astype(BF16))
            vtb_ref[0, blk, hd * V_ROWS + HEAD_DIM:(hd + 1) * V_ROWS, :] = ones
        km_ref[blk] = jnp.mean(k[rows], axis=0, keepdims=True)


def _ffn_call(x, wg, wu, wd, g, b, *, alpha, tm):
    n = x.shape[0]
    row = pl.BlockSpec((tm, D_MODEL), lambda i: (i, 0))
    return pl.pallas_call(
        functools.partial(_ffn_kernel, alpha=alpha),
        grid=(n // tm,),
        in_specs=[row, _resident(wg.shape), _resident(wu.shape), _resident(wd.shape),
                  _resident(g.shape), _resident(b.shape)],
        out_specs=row,
        out_shape=jax.ShapeDtypeStruct((n, D_MODEL), F32),
        compiler_params=_params(1),
        name="ffn",
    )(x, wg, wu, wd, g, b)


def _ffn_proj_call(x, wg, wu, wd, g, b, win, *, alpha, q_scale, tm, moba_batch=None):
    n = x.shape[0]
    row = lambda w: pl.BlockSpec((tm, w), lambda i: (i, 0))
    sds = jax.ShapeDtypeStruct
    out_specs = [row(D_MODEL)]
    out_shape = [sds((n, D_MODEL), F32)]
    if moba_batch is None:
        out_specs += [row(D_A), row(D_A), row(D_A), row(D_POOL)]
        out_shape += [sds((n, D_A), BF16), sds((n, D_A), F32), sds((n, D_A), F32), sds((n, D_POOL), F32)]
    else:
        seq = n // moba_batch
        tpb = seq // tm
        bpt = tm // MOBA_BLOCK
        tcol = pl.BlockSpec((1, D_A, tm), lambda i: (i // tpb, 0, i % tpb))
        out_specs += [tcol, tcol, tcol, row(D_POOL), row(D_A),
                      pl.BlockSpec((1, bpt, N_HEADS_A * V_ROWS, MOBA_BLOCK), lambda i: (i // tpb, i % tpb, 0, 0)),
                      pl.BlockSpec((bpt, 1, D_A), lambda i: (i, 0, 0))]
        out_shape += [sds((moba_batch, D_A, seq), BF16),
                      sds((moba_batch, D_A, seq), F32), sds((moba_batch, D_A, seq), F32), sds((n, D_POOL), F32),
                      sds((n, D_A), BF16),
                      sds((moba_batch, seq // MOBA_BLOCK, N_HEADS_A * V_ROWS, MOBA_BLOCK), BF16),
                      sds((n // MOBA_BLOCK, 1, D_A), F32)]
    return pl.pallas_call(
        functools.partial(_ffn_proj_kernel, alpha=alpha, q_scale=q_scale),
        grid=(n // tm,),
        in_specs=[row(D_MODEL), _resident(wg.shape), _resident(wu.shape), _resident(wd.shape),
                  _resident(g.shape), _resident(b.shape), _resident(win.shape)],
        out_specs=out_specs,
        out_shape=out_shape,
        compiler_params=_params(1),
        name="ffn_proj",
    )(x, wg, wu, wd, g, b, win)


def _memkv_kernel(m_ref, wk_ref, wv_ref, k_ref, v_ref, kb_ref, vb_ref):
    mb = m_ref[...].astype(BF16)
    k = jnp.dot(mb, wk_ref[...], preferred_element_type=F32)
    v = jnp.dot(mb, wv_ref[...], preferred_element_type=F32)
    k_ref[...] = k
    v_ref[...] = v
    kb_ref[...] = k.astype(BF16)
    vb_ref[...] = v.astype(BF16)


def _memkv_call(mem, wk, wv, *, tm):
    n = mem.shape[0]
    row = pl.BlockSpec((tm, D_MODEL), lambda i: (i, 0))
    sds = lambda dt: jax.ShapeDtypeStruct((n, D_MODEL), dt)
    return pl.pallas_call(
        _memkv_kernel,
        grid=(n // tm,),
        in_specs=[row, _resident(wk.shape), _resident(wv.shape)],
        out_specs=[row, row, row, row],
        out_shape=[sds(F32), sds(F32), sds(BF16), sds(BF16)],
        compiler_params=_params(1),
        name="memkv",
    )(mem, wk, wv)


def _top3_rows(gate, n_valid, n_rows):
    rid = lax.broadcasted_iota(jnp.int32, gate.shape, 0).astype(F32)
    g = jnp.where(rid < n_valid, gate, NEG)
    sel = jnp.zeros(gate.shape, jnp.bool_)
    for _ in range(MOBA_TOPK):
        m = jnp.max(g, axis=0, keepdims=True)
        idx = jnp.min(jnp.where(g == m, rid, float(n_rows)), axis=0, keepdims=True)
        pick = rid == idx
        sel = jnp.logical_or(sel, jnp.logical_and(pick, idx < n_valid))
        g = jnp.where(pick, -jnp.inf, g)
    return sel


def _moba_kernel(rb_ref, pt_ref, q_ref, kb_ref, vtb_ref, km_ref, bkt_ref, kpool_ref, o_ref, kmo_ref,
                 tbl_ref, rbt_ref, qh_ref, s_ref, p_ref, acc_ref, m_ref, smx_ref, al_ref,
                 ring_ref, ring_sem, kacc_ref, out_sem, *, n_blk, far_bucket, n_dec, n_pool_blk_per_dec):
    b = pl.program_id(0)
    i = pl.program_id(1)
    row_zero, row_prev = n_blk, n_blk + 1
    n_chunk = MOBA_BLOCK // SCORE_CHUNK
    chunks = [slice(c * SCORE_CHUNK, (c + 1) * SCORE_CHUNK) for c in range(n_chunk)]
    head_lanes = lambda h: slice((h // HEADS_PER_TILE) * LANES, (h // HEADS_PER_TILE + 1) * LANES)

    @pl.when(jnp.logical_and(b == 0, i == 0))
    def _build_bias_tiles():
        def per_head(h, _):
            for slot in range(2):
                bk = bkt_ref[slot]
                t = jnp.zeros(bk.shape, F32)
                for bucket in range(N_BUCKETS):
                    t = jnp.where(bk == bucket, rb_ref[bucket * N_HEADS_A + h], t)
                tbl_ref[h, slot] = jnp.where(bk < 0, NEG, t)
            return 0
        lax.fori_loop(0, N_HEADS_A, per_head, 0)

    qt = q_ref[0]
    km = km_ref[0]
    i_f = i.astype(F32)
    jp_f = jnp.maximum(i - 1, 0).astype(F32)
    row_head = lax.broadcasted_iota(jnp.int32, (LANES, MOBA_BLOCK), 0) // HEAD_DIM
    rid = lax.broadcasted_iota(jnp.int32, (n_blk, MOBA_BLOCK), 0).astype(F32)
    for h in range(N_HEADS_A):
        q2 = qt[head_lanes(h), :].astype(F32)
        qh = jnp.where(row_head == h % HEADS_PER_TILE, q2, 0.0).astype(BF16)
        qh_ref[h] = qh
        gate = jnp.dot(km[:, head_lanes(h)].astype(BF16), qh, preferred_element_type=F32)
        sel = _top3_rows(gate, i_f, n_blk)
        rbt_ref[h, :n_blk] = jnp.where(sel, rb_ref[far_bucket * N_HEADS_A + h], NEG)
        prev_sel = jnp.max(jnp.where(jnp.logical_and(sel, rid == jp_f), 1.0, 0.0), axis=0, keepdims=True)
        rbt_ref[h, row_zero:row_zero + 1] = jnp.zeros((1, MOBA_BLOCK), F32)
        rbt_ref[h, row_prev:row_prev + 1] = jnp.where(prev_sel > 0.5, 0.0, NEG)
        m_ref[h] = jnp.full((1, MOBA_BLOCK), NEG, F32)
        al_ref[h] = jnp.ones((1, MOBA_BLOCK), F32)
        acc_ref[h] = jnp.zeros((V_ROWS, MOBA_BLOCK), F32)
        p_ref[h] = jnp.zeros((MOBA_BLOCK, MOBA_BLOCK), BF16)

    def list_block(t):
        return jnp.clip(jnp.where(t == 0, i, jnp.where(t == 1, i - 1, t - 2)), 0, n_blk - 1)

    def stage_a(t):
        rows = pl.ds(pl.multiple_of(list_block(t) * MOBA_BLOCK, MOBA_BLOCK), MOBA_BLOCK)
        for h in range(N_HEADS_A):
            s = jnp.dot(kb_ref[0, rows, head_lanes(h)], qh_ref[h], preferred_element_type=F32)
            s_ref[h] = s
            smx_ref[h] = jnp.max(s, axis=0, keepdims=True)

    def add_bias_tile(slot):
        for h in range(N_HEADS_A):
            s = s_ref[h] + tbl_ref[h, slot]
            s_ref[h] = s
            smx_ref[h] = jnp.max(s, axis=0, keepdims=True)

    def stage_b(t):
        row = jnp.where(t == 0, row_zero, jnp.where(t == 1, row_prev, t - 2))
        for h in range(N_HEADS_A):
            rb = rbt_ref[h, pl.ds(row, 1), :]
            m_old = m_ref[h]
            m_new = jnp.maximum(m_old, smx_ref[h] + rb)
            shift = m_new - rb
            for c in chunks:
                p_ref[h, c, :] = jnp.exp2(s_ref[h, c, :] - shift).astype(BF16)
            m_ref[h] = m_new
            al_ref[h] = jnp.exp2(m_old - m_new)

    def stage_c(t):
        j = list_block(t)
        for h in range(N_HEADS_A):
            vt = vtb_ref[0, j, h * V_ROWS:(h + 1) * V_ROWS, :]
            acc_ref[h] = al_ref[h] * acc_ref[h] + jnp.dot(vt, p_ref[h], preferred_element_type=F32)

    n_pool_blk = n_dec * n_pool_blk_per_dec
    n_pool_pages = n_pool_blk * PAGES_PER_BLOCK
    ring = min(KMEAN_RING, n_pool_pages)
    iters_per_batch = n_blk * (n_blk + 1) // 2
    lane = lax.broadcasted_iota(jnp.int32, (D_A, LANES), 1)

    def page_copy(page_no, slot):
        return pltpu.make_async_copy(kpool_ref.at[pt_ref[page_no]], ring_ref.at[slot], ring_sem.at[slot])

    def pool_block_mean(g):
        g = jnp.minimum(g, n_pool_blk - 1)
        total = None
        for half in range(PAGES_PER_BLOCK):
            page_no = g * PAGES_PER_BLOCK + half
            slot = page_no % ring
            page_copy(page_no, slot).wait()
            x = ring_ref[slot]
            total = x if total is None else total + x
            nxt = jnp.where(page_no + ring < n_pool_pages, page_no + ring, page_no)
            page_copy(nxt, slot).start()
        col = jnp.sum(total, axis=1, keepdims=True) * (1.0 / MOBA_BLOCK)
        dec, j = g // n_pool_blk_per_dec, g % n_pool_blk_per_dec
        kacc_ref[dec] = jnp.where(lane == j, col, kacc_ref[dec])

    @pl.when(jnp.logical_and(b == 0, i == 0))
    def _start_ring():
        for slot in range(ring):
            page_copy(slot, slot).start()
        kacc_ref[...] = jnp.zeros(kacc_ref.shape, F32)

    stage_a(0)
    add_bias_tile(0)
    g0 = b * iters_per_batch + i * (i + 1) // 2

    def step(t, carry):
        pool_block_mean(g0 + t - 1)
        stage_c(jnp.maximum(t - 2, 0))
        stage_b(t - 1)
        stage_a(t)

        @pl.when(t == 1)
        def _():
            add_bias_tile(1)

        return carry

    lax.fori_loop(1, i + 1, step, 0)
    pool_block_mean(g0 + i)
    stage_c(jnp.maximum(i - 1, 0))
    stage_b(i)
    stage_c(i)
    out_t = jnp.concatenate([acc_ref[h, :HEAD_DIM] / acc_ref[h, HEAD_DIM:HEAD_DIM + 1]
                             for h in range(N_HEADS_A)], axis=0)
    o_ref[0] = out_t.T.astype(BF16)

    @pl.when(jnp.logical_and(b == pl.num_programs(0) - 1, i == n_blk - 1))
    def _finish_pool_means():
        n_done = pl.num_programs(0) * iters_per_batch

        def rest(g, carry):
            pool_block_mean(g)
            return carry

        lax.fori_loop(jnp.minimum(n_done, n_pool_blk), n_pool_blk, rest, 0)
        for slot in range(ring):
            page_copy(slot, slot).wait()
        out_copy = pltpu.make_async_copy(kacc_ref, kmo_ref, out_sem.at[0])
        out_copy.start()
        out_copy.wait()


def _bucket_np(dist):
    n = np.maximum(dist, 0)
    max_exact = N_BUCKETS // 2
    nf = np.maximum(n, 1).astype(np.float32)
    large = max_exact + (np.log(nf / np.float32(max_exact)) / np.float32(math.log(MAX_DISTANCE / max_exact))
                         * np.float32(N_BUCKETS - max_exact)).astype(np.int32)
    return np.where(n < max_exact, n, np.minimum(large, N_BUCKETS - 1)).astype(np.int32)


def _bucket_tiles():
    key = np.arange(MOBA_BLOCK)[:, None]
    qry = np.arange(MOBA_BLOCK)[None, :]
    own = np.where(qry >= key, _bucket_np(qry - key), -1)
    prev = _bucket_np(qry - key + MOBA_BLOCK)
    return np.stack([own, prev]).astype(np.int32)


def _far_bucket():
    far = _bucket_np(np.array([MOBA_BLOCK + 1, 1 << 30]))
    assert far[0] == far[1]
    return int(far[0])


def _moba_call(rel_bias_flat, page_table, q, kb, vtb, km, kpool_t):
    n_batch, _, seq = q.shape
    n_blk = seq // MOBA_BLOCK
    n_dec, n_pages = page_table.shape
    assert n_pages % PAGES_PER_BLOCK == 0 and n_pages // PAGES_PER_BLOCK <= LANES
    qo = pl.BlockSpec((1, MOBA_BLOCK, D_A), lambda b, i, *_: (b, i, 0))

    def per_batch(shape, **kw):
        return pl.BlockSpec((1,) + shape, lambda b, i, *_: (b,) + (0,) * len(shape), **kw)

    once = dict(pipeline_mode=pl.Buffered(1))
    any_ = pl.BlockSpec(memory_space=pl.ANY)
    tile = (MOBA_BLOCK, MOBA_BLOCK)
    stat = pltpu.VMEM((N_HEADS_A, 1, MOBA_BLOCK), F32)
    return pl.pallas_call(
        functools.partial(_moba_kernel, n_blk=n_blk, far_bucket=_far_bucket(), n_dec=n_dec,
                          n_pool_blk_per_dec=n_pages // PAGES_PER_BLOCK),
        grid_spec=pltpu.PrefetchScalarGridSpec(
            num_scalar_prefetch=2,
            grid=(n_batch, n_blk),
            in_specs=[pl.BlockSpec((1, D_A, MOBA_BLOCK), lambda b, i, *_: (b, 0, i)), per_batch((seq, D_A), **once),
                      per_batch((n_blk, N_HEADS_A * V_ROWS, MOBA_BLOCK), **once),
                      per_batch((n_blk, D_A)), _resident((2,) + tile), any_],
            out_specs=[qo, any_],
            scratch_shapes=[pltpu.VMEM((N_HEADS_A, 2) + tile, F32),
                            pltpu.VMEM((N_HEADS_A, n_blk + SUBLANES, MOBA_BLOCK), F32),
                            pltpu.VMEM((N_HEADS_A, LANES, MOBA_BLOCK), BF16),
                            pltpu.VMEM((N_HEADS_A,) + tile, F32),
                            pltpu.VMEM((N_HEADS_A,) + tile, BF16),
                            pltpu.VMEM((N_HEADS_A, V_ROWS, MOBA_BLOCK), F32),
                            stat, stat, stat,
                            pltpu.VMEM((KMEAN_RING, D_A, PAGE_SIZE), F32),
                            pltpu.SemaphoreType.DMA((KMEAN_RING,)),
                            pltpu.VMEM((n_dec, D_A, LANES), F32),
                            pltpu.SemaphoreType.DMA((1,))]),
        out_shape=[jax.ShapeDtypeStruct((n_batch, seq, D_A), BF16),
                   jax.ShapeDtypeStruct((n_dec, D_A, LANES), F32)],
        compiler_params=_params(2),
        name="moba_prompt",
    )(rel_bias_flat, page_table.reshape(-1), q, kb, vtb, km, jnp.asarray(_bucket_tiles()), kpool_t)


def _pool_project(window_sum, u_new, cnt, pw_ref, ps_ref):
    ys = []
    for g, w in enumerate(POOL_WINDOWS):
        lanes = slice(g * POOL_GROUP, (g + 1) * POOL_GROUP)
        d = window_sum(g, w) / cnt(w) - u_new[:, lanes]
        ys.append(jnp.dot(d.astype(BF16), pw_ref[g], preferred_element_type=F32))
    return jnp.concatenate(ys, axis=1) * ps_ref[...]


def _out_project(x1, a, p, wout_ref, g_ref, b_ref, alpha):
    proj = (jnp.dot(a, wout_ref[:D_A, :], preferred_element_type=F32)
            + jnp.dot(p.astype(BF16), wout_ref[D_A:, :], preferred_element_type=F32))
    return _layer_norm(alpha * x1 + proj, g_ref[...], b_ref[...])


def _mix_kernel(x1_ref, a_ref, u_ref, uh_ref, pw_ref, ps_ref, wout_ref, g1_ref, b1_ref,
                wxq_ref, wxo_ref, mk_ref, mv_ref, g2_ref, b2_ref, x3_ref, ext_ref, *, alpha, tm, tpb):
    t_in_b = pl.program_id(0) % tpb
    ext_ref[:POOL_HALO, :] = jnp.where(t_in_b == 0, 0.0, uh_ref[...])
    ext_ref[POOL_HALO:, :] = u_ref[...]
    pos = t_in_b * tm + lax.broadcasted_iota(jnp.int32, (tm, 1), 0)

    def window_sum(g, w):
        lanes = slice(g * POOL_GROUP, (g + 1) * POOL_GROUP)
        s = ext_ref[POOL_HALO:POOL_HALO + tm, lanes]
        for back in range(1, w):
            s = s + ext_ref[POOL_HALO - back:POOL_HALO - back + tm, lanes]
        return s

    p = _pool_project(window_sum, u_ref[...], lambda w: jnp.minimum(w, pos + 1).astype(F32), pw_ref, ps_ref)
    x2 = _out_project(x1_ref[...], a_ref[...], p, wout_ref, g1_ref, b1_ref, alpha)

    qx = jnp.dot(x2.astype(BF16), wxq_ref[...], preferred_element_type=F32) * HEAD_DIM_X ** -0.5
    outs = []
    for h in range(N_HEADS_X):
        lanes = slice(h * HEAD_DIM_X, (h + 1) * HEAD_DIM_X)
        logits = lax.dot_general(qx[:, lanes].astype(BF16), mk_ref[0, :, lanes], _NT, preferred_element_type=F32)
        e = jnp.exp(logits - jnp.max(logits, axis=-1, keepdims=True))
        o = jnp.dot(e.astype(BF16), mv_ref[0, :, lanes], preferred_element_type=F32)
        outs.append(o / jnp.sum(e, axis=-1, keepdims=True))
    o = jnp.concatenate(outs, axis=1).astype(BF16)
    x3_ref[...] = _layer_norm(alpha * x2 + jnp.dot(o, wxo_ref[...], preferred_element_type=F32),
                              g2_ref[...], b2_ref[...])


def _mix_call(x1, a, u, pw, ps, wout, g1, b1, wxq, wxo, mk, mv, g2, b2, *, alpha, n_batch, tm):
    n = x1.shape[0]
    tpb = n // n_batch // tm
    n_mem = mk.shape[1]
    row = lambda w: pl.BlockSpec((tm, w), lambda i: (i, 0))
    halo = pl.BlockSpec((POOL_HALO, D_POOL), lambda i: (jnp.maximum(i * (tm // POOL_HALO) - 1, 0), 0))
    mem = pl.BlockSpec((1, n_mem, D_MODEL), lambda i: (i // tpb, 0, 0))
    return pl.pallas_call(
        functools.partial(_mix_kernel, alpha=alpha, tm=tm, tpb=tpb),
        grid=(n // tm,),
        in_specs=[row(D_MODEL), row(D_A), row(D_POOL), halo, _resident(pw.shape), _resident(ps.shape),
                  _resident(wout.shape), _resident(g1.shape), _resident(b1.shape),
                  _resident(wxq.shape), _resident(wxo.shape), mem, mem,
                  _resident(g2.shape), _resident(b2.shape)],
        out_specs=row(D_MODEL),
        out_shape=jax.ShapeDtypeStruct((n, D_MODEL), F32),
        scratch_shapes=[pltpu.VMEM((tm + POOL_HALO, D_POOL), F32)],
        compiler_params=_params(1),
        name="mix_prompt",
    )(x1, a, u, u, pw, ps, wout, g1, b1, wxq, wxo, mk, mv, g2, b2)


def _head_rows(q_row, n_rows):
    return _mask_heads(q_row, (n_rows, D_A), HEAD_DIM)


def _mask_heads(q_row, shape, head_dim):
    row = lax.broadcasted_iota(jnp.int32, shape, 0)
    lane_head = lax.broadcasted_iota(jnp.int32, shape, 1) // head_dim
    qb = jnp.broadcast_to(q_row.astype(F32), shape)
    return jnp.where(lane_head == row, qb, 0.0).astype(q_row.dtype)


def _gate_kernel(q_ref, km_ref, o_ref, *, n_blk):
    qm = _head_rows(q_ref[0], N_HEADS_A)
    gate = jnp.dot(qm, km_ref[0].astype(BF16), preferred_element_type=F32)
    lane = lax.broadcasted_iota(jnp.int32, gate.shape, 1)
    lane_f = lane.astype(F32)
    g = jnp.where(lane < n_blk, gate, NEG)
    out = jnp.zeros(gate.shape, F32)
    for t in range(MOBA_TOPK):
        m = jnp.max(g, axis=1, keepdims=True)
        idx = jnp.min(jnp.where(g == m, lane_f, float(LANES)), axis=1, keepdims=True)
        out = jnp.where(lane == t, idx, out)
        g = jnp.where(lane_f == idx, -jnp.inf, g)
    o_ref[0] = out.astype(jnp.int32)


def _gate_call(q, kmean_t, n_blk):
    n_dec = q.shape[0]
    return pl.pallas_call(
        functools.partial(_gate_kernel, n_blk=n_blk),
        grid=(n_dec,),
        in_specs=[pl.BlockSpec((1, 1, D_A), lambda b: (b, 0, 0)),
                  pl.BlockSpec((1, D_A, LANES), lambda b: (b, 0, 0))],
        out_specs=pl.BlockSpec((1, N_HEADS_A, LANES), lambda b: (b, 0, 0)),
        out_shape=jax.ShapeDtypeStruct((n_dec, N_HEADS_A, LANES), jnp.int32),
        compiler_params=_params(1),
        name="gate_sample",
    )(q, kmean_t)


def _head_page_copy(pool_ref, page, h, buf_ref, sem_ref, par, which, slot):
    rows = pl.ds(h * HEAD_DIM, HEAD_DIM)
    return pltpu.make_async_copy(pool_ref.at[page, rows, :], buf_ref.at[par, which, slot],
                                 sem_ref.at[par, which, slot])


def _sample_attn_kernel(top_ref, pt_ref, q_ref, kn_ref, vn_ref, tnear_ref, bias0_ref, bfar_ref,
                        kpool_ref, vpool_ref, o_ref, buf_ref, sem_ref, *, n_blk):
    b = pl.program_id(0)
    n_sel = MOBA_TOPK * PAGES_PER_BLOCK
    par = b % 2

    def copies(row, half_buf, h):
        out = []
        for t in range(MOBA_TOPK):
            blk = top_ref[(row * N_HEADS_A + h) * MOBA_TOPK + t]
            for half in range(PAGES_PER_BLOCK):
                page = pt_ref[row, blk * PAGES_PER_BLOCK + half]
                slot = h * n_sel + t * PAGES_PER_BLOCK + half
                out.append(_head_page_copy(kpool_ref, page, h, buf_ref, sem_ref, half_buf, 0, slot))
                out.append(_head_page_copy(vpool_ref, page, h, buf_ref, sem_ref, half_buf, 1, slot))
        return out

    def start_row(row, half_buf):
        for h in range(N_HEADS_A):
            for c in copies(row, half_buf, h):
                c.start()

    @pl.when(b == 0)
    def _():
        start_row(0, 0)

    @pl.when(b + 1 < pl.num_programs(0))
    def _():
        start_row(b + 1, 1 - par)

    q = q_ref[0]
    kn = kn_ref[0].astype(BF16).astype(F32)
    vn = vn_ref[0].astype(BF16).astype(F32)
    s_new_all = q.astype(F32) * kn
    outs = []
    for h in range(N_HEADS_A):
        for c in copies(b, par, h):
            c.wait()
        lanes = slice(h * HEAD_DIM, (h + 1) * HEAD_DIM)
        qh = jnp.broadcast_to(q[:, lanes], (SUBLANES, HEAD_DIM))
        kt = jnp.concatenate([buf_ref[par, 0, h * n_sel + s] for s in range(n_sel)], axis=1).astype(BF16)
        vt = jnp.concatenate([buf_ref[par, 1, h * n_sel + s] for s in range(n_sel)], axis=1).astype(BF16)
        bias = []
        for t in range(MOBA_TOPK):
            blk = top_ref[(b * N_HEADS_A + h) * MOBA_TOPK + t]
            bias.append(jnp.where(blk == n_blk - 1, tnear_ref[h:h + 1, :], bfar_ref[h]))
        s = jnp.dot(qh, kt, preferred_element_type=F32)[:1] + jnp.concatenate(bias, axis=1)
        s_new = jnp.sum(s_new_all[:, lanes], axis=1, keepdims=True) + bias0_ref[h]
        m = jnp.maximum(jnp.max(s, axis=1, keepdims=True), s_new)
        p = jnp.exp(s - m)
        p_new = jnp.exp(s_new - m)
        l = jnp.sum(p, axis=1, keepdims=True) + p_new
        pb = jnp.broadcast_to(p.astype(BF16), (SUBLANES, p.shape[1]))
        o = lax.dot_general(pb, vt, _NT, preferred_element_type=F32)[:1]
        o = o + p_new.astype(BF16).astype(F32) * vn[:, lanes]
        outs.append(o / l)
    o_ref[0] = jnp.concatenate(outs, axis=1)


def _sample_attn_call(top, page_table, q, kn, vn, tnear, bias0, bfar, kpool_t, vpool_t, n_blk):
    n_dec = q.shape[0]
    row = pl.BlockSpec((1, 1, D_A), lambda b, *_: (b, 0, 0))
    smem = pl.BlockSpec(memory_space=pltpu.SMEM)
    any_ = pl.BlockSpec(memory_space=pl.ANY)
    n_slots = N_HEADS_A * MOBA_TOPK * PAGES_PER_BLOCK
    return pl.pallas_call(
        functools.partial(_sample_attn_kernel, n_blk=n_blk),
        grid_spec=pltpu.PrefetchScalarGridSpec(
            num_scalar_prefetch=2,
            grid=(n_dec,),
            in_specs=[row, row, row, pl.BlockSpec(tnear.shape, lambda b, *_: (0, 0)), smem, smem, any_, any_],
            out_specs=row,
            scratch_shapes=[pltpu.VMEM((2, 2, n_slots, HEAD_DIM, PAGE_SIZE), F32),
                            pltpu.SemaphoreType.DMA((2, 2, n_slots))]),
        out_shape=jax.ShapeDtypeStruct((n_dec, 1, D_A), F32),
        compiler_params=_params(1),
        name="attn_sample",
    )(top, page_table, q, kn, vn, tnear, bias0, bfar, kpool_t, vpool_t)


def _mix_sample_kernel(x1_ref, a_ref, u_ref, st_ref, pw_ref, ps_ref, wout_ref, g1_ref, b1_ref, wxq_ref,
                       x2_ref, qx_ref, *, alpha, pos):
    u = u_ref[...]

    def window_sum(g, w):
        lanes = slice(g * POOL_GROUP, (g + 1) * POOL_GROUP)
        s = u[:, lanes]
        for back in range(1, w):
            s = s + st_ref[POOL_STATE - back, :, lanes]
        return s

    p = _pool_project(window_sum, u, lambda w: float(min(w, pos + 1)), pw_ref, ps_ref)
    x2 = _out_project(x1_ref[...], a_ref[...].astype(BF16), p, wout_ref, g1_ref, b1_ref, alpha)
    x2_ref[...] = x2
    qx = jnp.dot(x2.astype(BF16), wxq_ref[...], preferred_element_type=F32) * HEAD_DIM_X ** -0.5
    qx_ref[...] = qx.astype(BF16)


def _mix_sample_call(x1, a, u, state_t, pw, ps, wout, g1, b1, wxq, *, alpha, pos):
    n = x1.shape[0]
    args = (x1, a, u, state_t, pw, ps, wout, g1, b1, wxq)
    return pl.pallas_call(
        functools.partial(_mix_sample_kernel, alpha=alpha, pos=pos),
        grid=(1,),
        in_specs=[_resident(v.shape) for v in args],
        out_specs=[_resident((n, D_MODEL)), _resident((n, D_MODEL))],
        out_shape=[jax.ShapeDtypeStruct((n, D_MODEL), F32), jax.ShapeDtypeStruct((n, D_MODEL), BF16)],
        compiler_params=_params(1),
        name="mix_sample",
    )(*args)


def _xattn_sample_kernel(qx_ref, mk_ref, mv_ref, o_ref):
    q = qx_ref[0]
    outs = []
    for h in range(N_HEADS_X):
        lanes = slice(h * HEAD_DIM_X, (h + 1) * HEAD_DIM_X)
        qh = jnp.broadcast_to(q[:, lanes], (SUBLANES, HEAD_DIM_X))
        logits = lax.dot_general(qh, mk_ref[0, :, h, :].astype(BF16), _NT, preferred_element_type=F32)
        e = jnp.exp(logits - jnp.max(logits, axis=-1, keepdims=True))
        o = jnp.dot(e.astype(BF16), mv_ref[0, :, h, :].astype(BF16), preferred_element_type=F32)
        outs.append((o / jnp.sum(e, axis=-1, keepdims=True))[:1])
    o_ref[0] = jnp.concatenate(outs, axis=1)


def _xattn_sample_call(qx, mk, mv):
    n_dec, n_mem = mk.shape[:2]
    row = pl.BlockSpec((1, 1, D_MODEL), lambda b: (b, 0, 0))
    mem = pl.BlockSpec((1, n_mem, N_HEADS_X, HEAD_DIM_X), lambda b: (b, 0, 0, 0))
    return pl.pallas_call(
        _xattn_sample_kernel,
        grid=(n_dec,),
        in_specs=[row, mem, mem],
        out_specs=row,
        out_shape=jax.ShapeDtypeStruct((n_dec, 1, D_MODEL), F32),
        compiler_params=_params(1),
        name="xattn_sample",
    )(qx, mk, mv)


def _proj_ln_kernel(x_ref, o_ref, w_ref, g_ref, b_ref, y_ref, *, alpha):
    proj = jnp.dot(o_ref[...].astype(BF16), w_ref[...], preferred_element_type=F32)
    y_ref[...] = _layer_norm(alpha * x_ref[...] + proj, g_ref[...], b_ref[...])


def _proj_ln_call(x, o, w, g, b, *, alpha):
    args = (x, o, w, g, b)
    return pl.pallas_call(
        functools.partial(_proj_ln_kernel, alpha=alpha),
        grid=(1,),
        in_specs=[_resident(v.shape) for v in args],
        out_specs=_resident(x.shape),
        out_shape=jax.ShapeDtypeStruct(x.shape, F32),
        compiler_params=_params(1),
        name="proj_ln",
    )(*args)


def _t5_bucket(dist):
    n = jnp.maximum(dist, 0)
    max_exact = N_BUCKETS // 2
    nf = jnp.maximum(n, 1).astype(F32)
    large = max_exact + (jnp.log(nf / max_exact) / math.log(MAX_DISTANCE / max_exact)
                         * (N_BUCKETS - max_exact)).astype(jnp.int32)
    return jnp.where(n < max_exact, n, jnp.minimum(large, N_BUCKETS - 1))


def kernel(x_prompt, x_sample, cache_k, cache_v, cache_mem_k, cache_mem_v, state_pool, page_table, mem_prompt, rel_bias, ln_g, ln_b, w_ff1_gate, w_ff1_up, w_ff1_down, w_in, pool_w, pool_scale, w_out, w_xq, w_xk, w_xv, w_xo, w_ff2_gate, w_ff2_up, w_ff2_down):
    n_batch, seq, _ = x_prompt.shape
    n_dec, dec_seq, _ = x_sample.shape
    depth = ln_g.shape[0]
    n_pool = cache_k.shape[1]
    n_pages = page_table.shape[1]
    past_len = n_pages * PAGE_SIZE
    n_mem = mem_prompt.shape[1]
    assert depth == 1 and dec_seq == 1
    assert seq % MOBA_BLOCK == 0 and past_len % MOBA_BLOCK == 0
    assert past_len // MOBA_BLOCK >= MOBA_TOPK
    alpha = (2 * depth) ** 0.25
    tm = min(512, seq)
    n_blk = seq // MOBA_BLOCK

    bf = lambda w: w[0].astype(BF16)
    ff1 = (bf(w_ff1_gate), bf(w_ff1_up), bf(w_ff1_down))
    ff2 = (bf(w_ff2_gate), bf(w_ff2_up), bf(w_ff2_down))
    win = bf(w_in)
    ln = lambda i: (ln_g[0, i:i + 1], ln_b[0, i:i + 1])
    (g0, b0), (g1, b1), (g2, b2), (g3, b3) = ln(0), ln(1), ln(2), ln(3)
    pw, ps = pool_w[0].astype(BF16), pool_scale
    wout, wxq, wxo = bf(w_out), bf(w_xq), bf(w_xo)

    bias_by_dist = rel_bias[_t5_bucket(jnp.arange(2 * MOBA_BLOCK, dtype=jnp.int32))].T
    bfar = rel_bias[_far_bucket()]

    xp = x_prompt.reshape(n_batch * seq, D_MODEL)
    attn_scale = HEAD_DIM ** -0.5
    x1, q, kt, vt, u, kb, vtb, km = _ffn_proj_call(xp, *ff1, g0, b0, win, alpha=alpha, q_scale=attn_scale * LOG2E,
                                                   tm=tm, moba_batch=n_batch)
    kpool_t = cache_k[0].transpose(0, 2, 3, 1).reshape(n_pool, D_A, PAGE_SIZE)
    vpool_t = cache_v[0].transpose(0, 2, 3, 1).reshape(n_pool, D_A, PAGE_SIZE)
    a, kmean_t = _moba_call(rel_bias.reshape(-1) * LOG2E, page_table, q, kb.reshape(n_batch, seq, D_A), vtb,
                            km.reshape(n_batch, n_blk, D_A), kpool_t)
    mk, mv, mkb, mvb = _memkv_call(mem_prompt.reshape(n_batch * n_mem, D_MODEL), bf(w_xk), bf(w_xv),
                                   tm=min(512, n_batch * n_mem))
    x3 = _mix_call(x1, a.reshape(n_batch * seq, D_A), u, pw, ps, wout, g1, b1, wxq, wxo,
                   mkb.reshape(n_batch, n_mem, D_MODEL), mvb.reshape(n_batch, n_mem, D_MODEL), g2, b2,
                   alpha=alpha, n_batch=n_batch, tm=tm)
    y_prompt = _ffn_call(x3, *ff2, g3, b3, alpha=alpha, tm=tm).reshape(n_batch, seq, D_MODEL)

    to_heads = lambda t: t.reshape(t.shape[0], N_HEADS_A, HEAD_DIM, t.shape[2]).transpose(0, 3, 1, 2)[None]
    k_prompt, v_prompt = to_heads(kt), to_heads(vt)
    pool_prompt = u.reshape(n_batch, seq, D_POOL)[None, :, seq - POOL_STATE:]
    memk_prompt = mk.reshape(1, n_batch, n_mem, N_HEADS_X, HEAD_DIM_X)
    memv_prompt = mv.reshape(1, n_batch, n_mem, N_HEADS_X, HEAD_DIM_X)

    xs = x_sample.reshape(n_dec, D_MODEL)
    x1s, qs, kn, vn, us = _ffn_proj_call(xs, *ff1, g0, b0, win, alpha=alpha, q_scale=attn_scale, tm=n_dec)
    n_blk_s = past_len // MOBA_BLOCK
    qs3 = qs.reshape(n_dec, 1, D_A)
    top = _gate_call(qs3, kmean_t, n_blk_s)[:, :, :MOBA_TOPK].reshape(-1)
    tnear = bias_by_dist[:, MOBA_BLOCK - jnp.arange(MOBA_BLOCK)]
    a_s = _sample_attn_call(top, page_table, qs3, kn.reshape(n_dec, 1, D_A), vn.reshape(n_dec, 1, D_A),
                            tnear, bias_by_dist[:, 0], bfar, kpool_t, vpool_t, n_blk_s)
    state_t = state_pool[0].transpose(1, 0, 2)
    x2s, qxs = _mix_sample_call(x1s, a_s.reshape(n_dec, D_A), us, state_t, pw, ps, wout, g1, b1, wxq,
                                alpha=alpha, pos=past_len)
    o_s = _xattn_sample_call(qxs.reshape(n_dec, 1, D_MODEL),
                             cache_mem_k[0], cache_mem_v[0])
    x3s = _proj_ln_call(x2s, o_s.reshape(n_dec, D_MODEL), wxo, g2, b2, alpha=alpha)
    y_sample = _ffn_call(x3s, *ff2, g3, b3, alpha=alpha, tm=n_dec).reshape(n_dec, 1, D_MODEL)

    k_sample = kn.reshape(1, n_dec, 1, N_HEADS_A, HEAD_DIM)
    v_sample = vn.reshape(1, n_dec, 1, N_HEADS_A, HEAD_DIM)
    pool_sample = jnp.concatenate([state_t[1:], us[None]], axis=0).transpose(1, 0, 2)[None]
    return (y_prompt, y_sample, k_prompt, v_prompt, pool_prompt, memk_prompt, memv_prompt,
            k_sample, v_sample, pool_sample)
```

```python
import functools
import math

import jax
import jax.numpy as jnp
import numpy as np
from jax import lax
from jax.experimental import pallas as pl
from jax.experimental.pallas import tpu as pltpu

F32 = jnp.float32
BF16 = jnp.bfloat16

D_MODEL = 1024
HEAD_DIM = 64
N_HEADS_A = 8
D_A = N_HEADS_A * HEAD_DIM
MOBA_BLOCK = 256
MOBA_TOPK = 3
D_POOL = D_MODEL - D_A
POOL_WINDOWS = (2, 4, 8, 16)
POOL_GROUP = D_POOL // len(POOL_WINDOWS)
POOL_STATE = max(POOL_WINDOWS) - 1
N_BUCKETS = 32
MAX_DISTANCE = 128
N_HEADS_X = 4
HEAD_DIM_X = D_MODEL // N_HEADS_X
D_FF = 2816
LN_EPS = 1e-5
PAGE_SIZE = 128
NEG = -1e30

LANES = 128
SUBLANES = 8
VMEM_LIMIT_BYTES = 56 * 1024 * 1024

BF16_SUBLANES = 16
HEADS_PER_TILE = LANES // HEAD_DIM
V_ROWS = HEAD_DIM + BF16_SUBLANES
LOG2E = math.log2(math.e)
POOL_HALO = POOL_STATE + 1
PAGES_PER_BLOCK = MOBA_BLOCK // PAGE_SIZE
FFN_CHUNK = 256
KMEAN_RING = 16
SCORE_CHUNK = 32

_NT = (((1,), (1,)), ((), ()))


def _params(n_grid_dims):
    return pltpu.CompilerParams(
        dimension_semantics=("arbitrary",) * n_grid_dims, vmem_limit_bytes=VMEM_LIMIT_BYTES)


def _resident(shape):
    return pl.BlockSpec(shape, lambda *_: (0,) * len(shape), pipeline_mode=pl.Buffered(1))


def _layer_norm(z, g, b):
    mu = jnp.mean(z, axis=-1, keepdims=True)
    zc = z - mu
    var = jnp.mean(zc * zc, axis=-1, keepdims=True)
    return zc * lax.rsqrt(var + LN_EPS) * g + b


def _swiglu(x, wg_ref, wu_ref, wd_ref):
    xb = x.astype(BF16)
    acc = None
    for c in range(D_FF // FFN_CHUNK):
        sl = slice(c * FFN_CHUNK, (c + 1) * FFN_CHUNK)
        g = jnp.dot(xb, wg_ref[:, sl], preferred_element_type=F32)
        u = jnp.dot(xb, wu_ref[:, sl], preferred_element_type=F32)
        h = (g * jax.nn.sigmoid(g) * u).astype(BF16)
        part = jnp.dot(h, wd_ref[sl, :], preferred_element_type=F32)
        acc = part if acc is None else acc + part
    return acc


def _ffn_kernel(x_ref, wg_ref, wu_ref, wd_ref, g_ref, b_ref, y_ref, *, alpha):
    x = x_ref[...]
    y_ref[...] = _layer_norm(alpha * x + 0.5 * _swiglu(x, wg_ref, wu_ref, wd_ref), g_ref[...], b_ref[...])


def _ffn_proj_kernel(x_ref, wg_ref, wu_ref, wd_ref, g_ref, b_ref, win_ref,
                     x1_ref, q_ref, k_ref, v_ref, u_ref, *moba_refs, alpha, q_scale):
    x = x_ref[...]
    x1 = _layer_norm(alpha * x + 0.5 * _swiglu(x, wg_ref, wu_ref, wd_ref), g_ref[...], b_ref[...])
    x1_ref[...] = x1
    h = jnp.dot(x1.astype(BF16), win_ref[...], preferred_element_type=F32)
    q = h[:, :D_A] * q_scale
    k = h[:, D_A:2 * D_A]
    v = h[:, 2 * D_A:3 * D_A]
    u_ref[...] = h[:, 3 * D_A:]
    if not moba_refs:
        q_ref[...] = q.astype(BF16)
        k_ref[...] = k
        v_ref[...] = v
        return
    kb_ref, vtb_ref, km_ref = moba_refs
    vt = v.T
    q_ref[0] = q.T.astype(BF16)
    k_ref[0] = k.T
    v_ref[0] = vt
    kb_ref[...] = k.astype(BF16)
    ones = jnp.ones((BF16_SUBLANES, MOBA_BLOCK), BF16)
    for blk in range(k.shape[0] // MOBA_BLOCK):
        rows = slice(blk * MOBA_BLOCK, (blk + 1) * MOBA_BLOCK)
        for hd in range(N_HEADS_A):
            vtb_ref[0, blk, hd * V_ROWS:hd * V_ROWS + HEAD_DIM, :] = (
                vt[hd * HEAD_DIM:(hd + 1) * HEAD_DIM, rows].astype(BF16))
            vtb_ref[0, blk, hd * V_ROWS + HEAD_DIM:(hd + 1) * V_ROWS, :] = ones
        km_ref[blk] = jnp.mean(k[rows], axis=0, keepdims=True)


def _ffn_call(x, wg, wu, wd, g, b, *, alpha, tm):
    n = x.shape[0]
    row = pl.BlockSpec((tm, D_MODEL), lambda i: (i, 0))
    return pl.pallas_call(
        functools.partial(_ffn_kernel, alpha=alpha),
        grid=(n // tm,),
        in_specs=[row, _resident(wg.shape), _resident(wu.shape), _resident(wd.shape),
                  _resident(g.shape), _resident(b.shape)],
        out_specs=row,
        out_shape=jax.ShapeDtypeStruct((n, D_MODEL), F32),
        compiler_params=_params(1),
        name="ffn",
    )(x, wg, wu, wd, g, b)


def _ffn_proj_call(x, wg, wu, wd, g, b, win, *, alpha, q_scale, tm, moba_batch=None):
    n = x.shape[0]
    row = lambda w: pl.BlockSpec((tm, w), lambda i: (i, 0))
    sds = jax.ShapeDtypeStruct
    out_specs = [row(D_MODEL)]
    out_shape = [sds((n, D_MODEL), F32)]
    if moba_batch is None:
        out_specs += [row(D_A), row(D_A), row(D_A), row(D_POOL)]
        out_shape += [sds((n, D_A), BF16), sds((n, D_A), F32), sds((n, D_A), F32), sds((n, D_POOL), F32)]
    else:
        seq = n // moba_batch
        tpb = seq // tm
        bpt = tm // MOBA_BLOCK
        tcol = pl.BlockSpec((1, D_A, tm), lambda i: (i // tpb, 0, i % tpb))
        out_specs += [tcol, tcol, tcol, row(D_POOL), row(D_A),
                      pl.BlockSpec((1, bpt, N_HEADS_A * V_ROWS, MOBA_BLOCK), lambda i: (i // tpb, i % tpb, 0, 0)),
                      pl.BlockSpec((bpt, 1, D_A), lambda i: (i, 0, 0))]
        out_shape += [sds((moba_batch, D_A, seq), BF16),
                      sds((moba_batch, D_A, seq), F32), sds((moba_batch, D_A, seq), F32), sds((n, D_POOL), F32),
                      sds((n, D_A), BF16),
                      sds((moba_batch, seq // MOBA_BLOCK, N_HEADS_A * V_ROWS, MOBA_BLOCK), BF16),
                      sds((n // MOBA_BLOCK, 1, D_A), F32)]
    return pl.pallas_call(
        functools.partial(_ffn_proj_kernel, alpha=alpha, q_scale=q_scale),
        grid=(n // tm,),
        in_specs=[row(D_MODEL), _resident(wg.shape), _resident(wu.shape), _resident(wd.shape),
                  _resident(g.shape), _resident(b.shape), _resident(win.shape)],
        out_specs=out_specs,
        out_shape=out_shape,
        compiler_params=_params(1),
        name="ffn_proj",
    )(x, wg, wu, wd, g, b, win)


def _memkv_kernel(m_ref, wk_ref, wv_ref, k_ref, v_ref, kbt_ref, vb_ref):
    mb = m_ref[...].astype(BF16)
    k = jnp.dot(mb, wk_ref[...], preferred_element_type=F32)
    v = jnp.dot(mb, wv_ref[...], preferred_element_type=F32)
    k_ref[...] = k
    v_ref[...] = v
    kbt_ref[0] = k.T.astype(BF16)
    vb_ref[...] = v.astype(BF16)


def _memkv_call(mem, wk, wv, *, n_mem):
    n = mem.shape[0]
    row = pl.BlockSpec((n_mem, D_MODEL), lambda i: (i, 0))
    sds = lambda dt: jax.ShapeDtypeStruct((n, D_MODEL), dt)
    return pl.pallas_call(
        _memkv_kernel,
        grid=(n // n_mem,),
        in_specs=[row, _resident(wk.shape), _resident(wv.shape)],
        out_specs=[row, row, pl.BlockSpec((1, D_MODEL, n_mem), lambda i: (i, 0, 0)), row],
        out_shape=[sds(F32), sds(F32), jax.ShapeDtypeStruct((n // n_mem, D_MODEL, n_mem), BF16), sds(BF16)],
        compiler_params=_params(1),
        name="memkv",
    )(mem, wk, wv)


def _top3_rows(gate, n_valid, n_rows):
    rid = lax.broadcasted_iota(jnp.int32, gate.shape, 0).astype(F32)
    g = jnp.where(rid < n_valid, gate, NEG)
    sel = jnp.zeros(gate.shape, jnp.bool_)
    for _ in range(MOBA_TOPK):
        m = jnp.max(g, axis=0, keepdims=True)
        idx = jnp.min(jnp.where(g == m, rid, float(n_rows)), axis=0, keepdims=True)
        pick = rid == idx
        sel = jnp.logical_or(sel, jnp.logical_and(pick, idx < n_valid))
        g = jnp.where(pick, -jnp.inf, g)
    return sel


def _moba_kernel(rb_ref, pt_ref, q_ref, kb_ref, vtb_ref, km_ref, bkt_ref, kpool_ref, o_ref, kmo_ref,
                 tbl_ref, rbt_ref, qh_ref, s_ref, p_ref, acc_ref, m_ref, smx_ref, al_ref,
                 ring_ref, ring_sem, kacc_ref, out_sem, *, n_blk, far_bucket, n_dec, n_pool_blk_per_dec):
    b = pl.program_id(0)
    i = pl.program_id(1)
    row_zero, row_prev = n_blk, n_blk + 1
    n_chunk = MOBA_BLOCK // SCORE_CHUNK
    chunks = [slice(c * SCORE_CHUNK, (c + 1) * SCORE_CHUNK) for c in range(n_chunk)]
    head_lanes = lambda h: slice((h // HEADS_PER_TILE) * LANES, (h // HEADS_PER_TILE + 1) * LANES)

    @pl.when(jnp.logical_and(b == 0, i == 0))
    def _build_bias_tiles():
        def per_head(h, _):
            for slot in range(2):
                bk = bkt_ref[slot]
                t = jnp.zeros(bk.shape, F32)
                for bucket in range(N_BUCKETS):
                    t = jnp.where(bk == bucket, rb_ref[bucket * N_HEADS_A + h], t)
                tbl_ref[h, slot] = jnp.where(bk < 0, NEG, t)
            return 0
        lax.fori_loop(0, N_HEADS_A, per_head, 0)

    qt = q_ref[0]
    km = km_ref[0]
    i_f = i.astype(F32)
    jp_f = jnp.maximum(i - 1, 0).astype(F32)
    row_head = lax.broadcasted_iota(jnp.int32, (LANES, MOBA_BLOCK), 0) // HEAD_DIM
    rid = lax.broadcasted_iota(jnp.int32, (n_blk, MOBA_BLOCK), 0).astype(F32)
    for h in range(N_HEADS_A):
        q2 = qt[head_lanes(h), :].astype(F32)
        qh = jnp.where(row_head == h % HEADS_PER_TILE, q2, 0.0).astype(BF16)
        qh_ref[h] = qh
        gate = jnp.dot(km[:, head_lanes(h)].astype(BF16), qh, preferred_element_type=F32)
        sel = _top3_rows(gate, i_f, n_blk)
        rbt_ref[h, :n_blk] = jnp.where(sel, rb_ref[far_bucket * N_HEADS_A + h], NEG)
        prev_sel = jnp.max(jnp.where(jnp.logical_and(sel, rid == jp_f), 1.0, 0.0), axis=0, keepdims=True)
        rbt_ref[h, row_zero:row_zero + 1] = jnp.zeros((1, MOBA_BLOCK), F32)
        rbt_ref[h, row_prev:row_prev + 1] = jnp.where(prev_sel > 0.5, 0.0, NEG)
        m_ref[h] = jnp.full((1, MOBA_BLOCK), NEG, F32)
        al_ref[h] = jnp.ones((1, MOBA_BLOCK), F32)
        acc_ref[h] = jnp.zeros((V_ROWS, MOBA_BLOCK), F32)
        p_ref[h] = jnp.zeros((MOBA_BLOCK, MOBA_BLOCK), BF16)

    def list_block(t):
        return jnp.clip(jnp.where(t == 0, i, jnp.where(t == 1, i - 1, t - 2)), 0, n_blk - 1)

    def stage_a(t):
        rows = pl.ds(pl.multiple_of(list_block(t) * MOBA_BLOCK, MOBA_BLOCK), MOBA_BLOCK)
        for h in range(N_HEADS_A):
            s = jnp.dot(kb_ref[0, rows, head_lanes(h)], qh_ref[h], preferred_element_type=F32)
            s_ref[h] = s
            smx_ref[h] = jnp.max(s, axis=0, keepdims=True)

    def add_bias_tile(slot):
        for h in range(N_HEADS_A):
            s = s_ref[h] + tbl_ref[h, slot]
            s_ref[h] = s
            smx_ref[h] = jnp.max(s, axis=0, keepdims=True)

    def stage_b(t):
        row = jnp.where(t == 0, row_zero, jnp.where(t == 1, row_prev, t - 2))
        for h in range(N_HEADS_A):
            rb = rbt_ref[h, pl.ds(row, 1), :]
            m_old = m_ref[h]
            m_new = jnp.maximum(m_old, smx_ref[h] + rb)
            shift = m_new - rb
            for c in chunks:
                p_ref[h, c, :] = jnp.exp2(s_ref[h, c, :] - shift).astype(BF16)
            m_ref[h] = m_new
            al_ref[h] = jnp.exp2(m_old - m_new)

    def stage_c(t):
        j = list_block(t)
        for h in range(N_HEADS_A):
            vt = vtb_ref[0, j, h * V_ROWS:(h + 1) * V_ROWS, :]
            acc_ref[h] = al_ref[h] * acc_ref[h] + jnp.dot(vt, p_ref[h], preferred_element_type=F32)

    n_pool_blk = n_dec * n_pool_blk_per_dec
    n_pool_pages = n_pool_blk * PAGES_PER_BLOCK
    ring = min(KMEAN_RING, n_pool_pages)
    iters_per_batch = n_blk * (n_blk + 1) // 2
    lane = lax.broadcasted_iota(jnp.int32, (D_A, LANES), 1)

    def page_copy(page_no, slot):
        return pltpu.make_async_copy(kpool_ref.at[pt_ref[page_no]], ring_ref.at[slot], ring_sem.at[slot])

    def pool_block_mean(g):
        g = jnp.minimum(g, n_pool_blk - 1)
        total = None
        for half in range(PAGES_PER_BLOCK):
            page_no = g * PAGES_PER_BLOCK + half
            slot = page_no % ring
            page_copy(page_no, slot).wait()
            x = ring_ref[slot]
            total = x if total is None else total + x
            nxt = jnp.where(page_no + ring < n_pool_pages, page_no + ring, page_no)
            page_copy(nxt, slot).start()
        col = jnp.sum(total, axis=1, keepdims=True) * (1.0 / MOBA_BLOCK)
        dec, j = g // n_pool_blk_per_dec, g % n_pool_blk_per_dec
        kacc_ref[dec] = jnp.where(lane == j, col, kacc_ref[dec])

    @pl.when(jnp.logical_and(b == 0, i == 0))
    def _start_ring():
        for slot in range(ring):
            page_copy(slot, slot).start()
        kacc_ref[...] = jnp.zeros(kacc_ref.shape, F32)

    stage_a(0)
    add_bias_tile(0)
    g0 = b * iters_per_batch + i * (i + 1) // 2

    def step(t, carry):
        pool_block_mean(g0 + t - 1)
        stage_c(jnp.maximum(t - 2, 0))
        stage_b(t - 1)
        stage_a(t)

        @pl.when(t == 1)
        def _():
            add_bias_tile(1)

        return carry

    lax.fori_loop(1, i + 1, step, 0)
    pool_block_mean(g0 + i)
    stage_c(jnp.maximum(i - 1, 0))
    stage_b(i)
    stage_c(i)
    out_t = jnp.concatenate([acc_ref[h, :HEAD_DIM] / acc_ref[h, HEAD_DIM:HEAD_DIM + 1]
                             for h in range(N_HEADS_A)], axis=0)
    o_ref[0] = out_t.T.astype(BF16)

    @pl.when(jnp.logical_and(b == pl.num_programs(0) - 1, i == n_blk - 1))
    def _finish_pool_means():
        n_done = pl.num_programs(0) * iters_per_batch

        def rest(g, carry):
            pool_block_mean(g)
            return carry

        lax.fori_loop(jnp.minimum(n_done, n_pool_blk), n_pool_blk, rest, 0)
        for slot in range(ring):
            page_copy(slot, slot).wait()
        out_copy = pltpu.make_async_copy(kacc_ref, kmo_ref, out_sem.at[0])
        out_copy.start()
        out_copy.wait()


def _bucket_np(dist):
    n = np.maximum(dist, 0)
    max_exact = N_BUCKETS // 2
    nf = np.maximum(n, 1).astype(np.float32)
    large = max_exact + (np.log(nf / np.float32(max_exact)) / np.float32(math.log(MAX_DISTANCE / max_exact))
                         * np.float32(N_BUCKETS - max_exact)).astype(np.int32)
    return np.where(n < max_exact, n, np.minimum(large, N_BUCKETS - 1)).astype(np.int32)


def _bucket_tiles():
    key = np.arange(MOBA_BLOCK)[:, None]
    qry = np.arange(MOBA_BLOCK)[None, :]
    own = np.where(qry >= key, _bucket_np(qry - key), -1)
    prev = _bucket_np(qry - key + MOBA_BLOCK)
    return np.stack([own, prev]).astype(np.int32)


def _far_bucket():
    far = _bucket_np(np.array([MOBA_BLOCK + 1, 1 << 30]))
    assert far[0] == far[1]
    return int(far[0])


def _moba_call(rel_bias_flat, page_table, q, kb, vtb, km, kpool_t):
    n_batch, _, seq = q.shape
    n_blk = seq // MOBA_BLOCK
    n_dec, n_pages = page_table.shape
    assert n_pages % PAGES_PER_BLOCK == 0 and n_pages // PAGES_PER_BLOCK <= LANES
    qo = pl.BlockSpec((1, MOBA_BLOCK, D_A), lambda b, i, *_: (b, i, 0))

    def per_batch(shape, **kw):
        return pl.BlockSpec((1,) + shape, lambda b, i, *_: (b,) + (0,) * len(shape), **kw)

    once = dict(pipeline_mode=pl.Buffered(1))
    any_ = pl.BlockSpec(memory_space=pl.ANY)
    tile = (MOBA_BLOCK, MOBA_BLOCK)
    stat = pltpu.VMEM((N_HEADS_A, 1, MOBA_BLOCK), F32)
    return pl.pallas_call(
        functools.partial(_moba_kernel, n_blk=n_blk, far_bucket=_far_bucket(), n_dec=n_dec,
                          n_pool_blk_per_dec=n_pages // PAGES_PER_BLOCK),
        grid_spec=pltpu.PrefetchScalarGridSpec(
            num_scalar_prefetch=2,
            grid=(n_batch, n_blk),
            in_specs=[pl.BlockSpec((1, D_A, MOBA_BLOCK), lambda b, i, *_: (b, 0, i)), per_batch((seq, D_A), **once),
                      per_batch((n_blk, N_HEADS_A * V_ROWS, MOBA_BLOCK), **once),
                      per_batch((n_blk, D_A)), _resident((2,) + tile), any_],
            out_specs=[qo, any_],
            scratch_shapes=[pltpu.VMEM((N_HEADS_A, 2) + tile, F32),
                            pltpu.VMEM((N_HEADS_A, n_blk + SUBLANES, MOBA_BLOCK), F32),
                            pltpu.VMEM((N_HEADS_A, LANES, MOBA_BLOCK), BF16),
                            pltpu.VMEM((N_HEADS_A,) + tile, F32),
                            pltpu.VMEM((N_HEADS_A,) + tile, BF16),
                            pltpu.VMEM((N_HEADS_A, V_ROWS, MOBA_BLOCK), F32),
                            stat, stat, stat,
                            pltpu.VMEM((KMEAN_RING, D_A, PAGE_SIZE), F32),
                            pltpu.SemaphoreType.DMA((KMEAN_RING,)),
                            pltpu.VMEM((n_dec, D_A, LANES), F32),
                            pltpu.SemaphoreType.DMA((1,))]),
        out_shape=[jax.ShapeDtypeStruct((n_batch, seq, D_A), BF16),
                   jax.ShapeDtypeStruct((n_dec, D_A, LANES), F32)],
        compiler_params=_params(2),
        name="moba_prompt",
    )(rel_bias_flat, page_table.reshape(-1), q, kb, vtb, km, jnp.asarray(_bucket_tiles()), kpool_t)


def _pool_project(window_sum, u_new, cnt, pw_ref, ps_ref):
    ys = []
    for g, w in enumerate(POOL_WINDOWS):
        lanes = slice(g * POOL_GROUP, (g + 1) * POOL_GROUP)
        d = window_sum(g, w) / cnt(w) - u_new[:, lanes]
        ys.append(jnp.dot(d.astype(BF16), pw_ref[g], preferred_element_type=F32))
    return jnp.concatenate(ys, axis=1) * ps_ref[...]


def _out_project(x1, a, p, wout_ref, g_ref, b_ref, alpha):
    proj = (jnp.dot(a, wout_ref[:D_A, :], preferred_element_type=F32)
            + jnp.dot(p.astype(BF16), wout_ref[D_A:, :], preferred_element_type=F32))
    return _layer_norm(alpha * x1 + proj, g_ref[...], b_ref[...])


def _mix_kernel(x1_ref, a_ref, u_ref, uh_ref, pw_ref, ps_ref, wout_ref, g1_ref, b1_ref,
                wxq_ref, wxo_ref, mkt_ref, mv_ref, g2_ref, b2_ref, x3_ref, ext_ref, *, alpha, tm, tpb):
    t_in_b = pl.program_id(0) % tpb
    ext_ref[:POOL_HALO, :] = jnp.where(t_in_b == 0, 0.0, uh_ref[...])
    ext_ref[POOL_HALO:, :] = u_ref[...]
    pos = t_in_b * tm + lax.broadcasted_iota(jnp.int32, (tm, 1), 0)

    def window_sum(g, w):
        lanes = slice(g * POOL_GROUP, (g + 1) * POOL_GROUP)
        s = ext_ref[POOL_HALO:POOL_HALO + tm, lanes]
        for back in range(1, w):
            s = s + ext_ref[POOL_HALO - back:POOL_HALO - back + tm, lanes]
        return s

    p = _pool_project(window_sum, u_ref[...], lambda w: jnp.minimum(w, pos + 1).astype(F32), pw_ref, ps_ref)
    x2 = _out_project(x1_ref[...], a_ref[...], p, wout_ref, g1_ref, b1_ref, alpha)

    qx = jnp.dot(x2.astype(BF16), wxq_ref[...], preferred_element_type=F32) * HEAD_DIM_X ** -0.5
    outs = []
    for h in range(N_HEADS_X):
        lanes = slice(h * HEAD_DIM_X, (h + 1) * HEAD_DIM_X)
        logits = jnp.dot(qx[:, lanes].astype(BF16), mkt_ref[0, lanes, :], preferred_element_type=F32)
        e = jnp.exp(logits - jnp.max(logits, axis=-1, keepdims=True))
        o = jnp.dot(e.astype(BF16), mv_ref[0, :, lanes], preferred_element_type=F32)
        outs.append(o / jnp.sum(e, axis=-1, keepdims=True))
    o = jnp.concatenate(outs, axis=1).astype(BF16)
    x3_ref[...] = _layer_norm(alpha * x2 + jnp.dot(o, wxo_ref[...], preferred_element_type=F32),
                              g2_ref[...], b2_ref[...])


def _mix_call(x1, a, u, pw, ps, wout, g1, b1, wxq, wxo, mkt, mv, g2, b2, *, alpha, n_batch, tm):
    n = x1.shape[0]
    tpb = n // n_batch // tm
    n_mem = mv.shape[1]
    row = lambda w: pl.BlockSpec((tm, w), lambda i: (i, 0))
    halo = pl.BlockSpec((POOL_HALO, D_POOL), lambda i: (jnp.maximum(i * (tm // POOL_HALO) - 1, 0), 0))
    mem = pl.BlockSpec((1, n_mem, D_MODEL), lambda i: (i // tpb, 0, 0))
    memt = pl.BlockSpec((1, D_MODEL, n_mem), lambda i: (i // tpb, 0, 0))
    return pl.pallas_call(
        functools.partial(_mix_kernel, alpha=alpha, tm=tm, tpb=tpb),
        grid=(n // tm,),
        in_specs=[row(D_MODEL), row(D_A), row(D_POOL), halo, _resident(pw.shape), _resident(ps.shape),
                  _resident(wout.shape), _resident(g1.shape), _resident(b1.shape),
                  _resident(wxq.shape), _resident(wxo.shape), memt, mem,
                  _resident(g2.shape), _resident(b2.shape)],
        out_specs=row(D_MODEL),
        out_shape=jax.ShapeDtypeStruct((n, D_MODEL), F32),
        scratch_shapes=[pltpu.VMEM((tm + POOL_HALO, D_POOL), F32)],
        compiler_params=_params(1),
        name="mix_prompt",
    )(x1, a, u, u, pw, ps, wout, g1, b1, wxq, wxo, mkt, mv, g2, b2)


def _head_rows(q_row, n_rows):
    return _mask_heads(q_row, (n_rows, D_A), HEAD_DIM)


def _mask_heads(q_row, shape, head_dim):
    row = lax.broadcasted_iota(jnp.int32, shape, 0)
    lane_head = lax.broadcasted_iota(jnp.int32, shape, 1) // head_dim
    qb = jnp.broadcast_to(q_row.astype(F32), shape)
    return jnp.where(lane_head == row, qb, 0.0).astype(q_row.dtype)


def _gate_kernel(q_ref, km_ref, o_ref, *, n_blk):
    qm = _head_rows(q_ref[0], N_HEADS_A)
    gate = jnp.dot(qm, km_ref[0].astype(BF16), preferred_element_type=F32)
    lane = lax.broadcasted_iota(jnp.int32, gate.shape, 1)
    lane_f = lane.astype(F32)
    g = jnp.where(lane < n_blk, gate, NEG)
    out = jnp.zeros(gate.shape, F32)
    for t in range(MOBA_TOPK):
        m = jnp.max(g, axis=1, keepdims=True)
        idx = jnp.min(jnp.where(g == m, lane_f, float(LANES)), axis=1, keepdims=True)
        out = jnp.where(lane == t, idx, out)
        g = jnp.where(lane_f == idx, -jnp.inf, g)
    o_ref[0] = out.astype(jnp.int32)


def _gate_call(q, kmean_t, n_blk):
    n_dec = q.shape[0]
    return pl.pallas_call(
        functools.partial(_gate_kernel, n_blk=n_blk),
        grid=(n_dec,),
        in_specs=[pl.BlockSpec((1, 1, D_A), lambda b: (b, 0, 0)),
                  pl.BlockSpec((1, D_A, LANES), lambda b: (b, 0, 0))],
        out_specs=pl.BlockSpec((1, N_HEADS_A, LANES), lambda b: (b, 0, 0)),
        out_shape=jax.ShapeDtypeStruct((n_dec, N_HEADS_A, LANES), jnp.int32),
        compiler_params=_params(1),
        name="gate_sample",
    )(q, kmean_t)


def _head_page_copy(pool_ref, page, h, buf_ref, sem_ref, par, which, slot):
    rows = pl.ds(h * HEAD_DIM, HEAD_DIM)
    return pltpu.make_async_copy(pool_ref.at[page, rows, :], buf_ref.at[par, which, slot],
                                 sem_ref.at[par, which, slot])


def _sample_attn_kernel(top_ref, pt_ref, q_ref, kn_ref, vn_ref, tnear_ref, bias0_ref, bfar_ref,
                        kpool_ref, vpool_ref, o_ref, buf_ref, sem_ref, *, n_blk):
    b = pl.program_id(0)
    n_sel = MOBA_TOPK * PAGES_PER_BLOCK
    par = b % 2

    def copies(row, half_buf, h):
        out = []
        for t in range(MOBA_TOPK):
            blk = top_ref[(row * N_HEADS_A + h) * MOBA_TOPK + t]
            for half in range(PAGES_PER_BLOCK):
                page = pt_ref[row, blk * PAGES_PER_BLOCK + half]
                slot = h * n_sel + t * PAGES_PER_BLOCK + half
                out.append(_head_page_copy(kpool_ref, page, h, buf_ref, sem_ref, half_buf, 0, slot))
                out.append(_head_page_copy(vpool_ref, page, h, buf_ref, sem_ref, half_buf, 1, slot))
        return out

    def start_row(row, half_buf):
        for h in range(N_HEADS_A):
            for c in copies(row, half_buf, h):
                c.start()

    @pl.when(b == 0)
    def _():
        start_row(0, 0)

    @pl.when(b + 1 < pl.num_programs(0))
    def _():
        start_row(b + 1, 1 - par)

    for h in range(N_HEADS_A):
        for c in copies(b, par, h):
            c.wait()

    q = q_ref[0]
    kn = kn_ref[0].astype(BF16).astype(F32)
    vn = vn_ref[0].astype(BF16).astype(F32)
    s_new_all = q.astype(F32) * kn
    outs = []
    for h in range(N_HEADS_A):
        lanes = slice(h * HEAD_DIM, (h + 1) * HEAD_DIM)
        qh = jnp.broadcast_to(q[:, lanes], (SUBLANES, HEAD_DIM))
        kt = jnp.concatenate([buf_ref[par, 0, h * n_sel + s] for s in range(n_sel)], axis=1).astype(BF16)
        vt = jnp.concatenate([buf_ref[par, 1, h * n_sel + s] for s in range(n_sel)], axis=1).astype(BF16)
        bias = []
        for t in range(MOBA_TOPK):
            blk = top_ref[(b * N_HEADS_A + h) * MOBA_TOPK + t]
            bias.append(jnp.where(blk == n_blk - 1, tnear_ref[h:h + 1, :], bfar_ref[h]))
        s = jnp.dot(qh, kt, preferred_element_type=F32)[:1] + jnp.concatenate(bias, axis=1)
        s_new = jnp.sum(s_new_all[:, lanes], axis=1, keepdims=True) + bias0_ref[h]
        m = jnp.maximum(jnp.max(s, axis=1, keepdims=True), s_new)
        p = jnp.exp(s - m)
        p_new = jnp.exp(s_new - m)
        l = jnp.sum(p, axis=1, keepdims=True) + p_new
        pb = jnp.broadcast_to(p.astype(BF16), (SUBLANES, p.shape[1]))
        o = lax.dot_general(pb, vt, _NT, preferred_element_type=F32)[:1]
        o = o + p_new.astype(BF16).astype(F32) * vn[:, lanes]
        outs.append(o / l)
    o_ref[0] = jnp.concatenate(outs, axis=1)


def _sample_attn_call(top, page_table, q, kn, vn, tnear, bias0, bfar, kpool_t, vpool_t, n_blk):
    n_dec = q.shape[0]
    row = pl.BlockSpec((1, 1, D_A), lambda b, *_: (b, 0, 0))
    smem = pl.BlockSpec(memory_space=pltpu.SMEM)
    any_ = pl.BlockSpec(memory_space=pl.ANY)
    n_slots = N_HEADS_A * MOBA_TOPK * PAGES_PER_BLOCK
    return pl.pallas_call(
        functools.partial(_sample_attn_kernel, n_blk=n_blk),
        grid_spec=pltpu.PrefetchScalarGridSpec(
            num_scalar_prefetch=2,
            grid=(n_dec,),
            in_specs=[row, row, row, pl.BlockSpec(tnear.shape, lambda b, *_: (0, 0)), smem, smem, any_, any_],
            out_specs=row,
            scratch_shapes=[pltpu.VMEM((2, 2, n_slots, HEAD_DIM, PAGE_SIZE), F32),
                            pltpu.SemaphoreType.DMA((2, 2, n_slots))]),
        out_shape=jax.ShapeDtypeStruct((n_dec, 1, D_A), F32),
        compiler_params=_params(1),
        name="attn_sample",
    )(top, page_table, q, kn, vn, tnear, bias0, bfar, kpool_t, vpool_t)


def _mix_sample_kernel(x1_ref, a_ref, u_ref, st_ref, pw_ref, ps_ref, wout_ref, g1_ref, b1_ref, wxq_ref,
                       x2_ref, qx_ref, *, alpha, pos):
    u = u_ref[...]

    def window_sum(g, w):
        lanes = slice(g * POOL_GROUP, (g + 1) * POOL_GROUP)
        s = u[:, lanes]
        for back in range(1, w):
            s = s + st_ref[POOL_STATE - back, :, lanes]
        return s

    p = _pool_project(window_sum, u, lambda w: float(min(w, pos + 1)), pw_ref, ps_ref)
    x2 = _out_project(x1_ref[...], a_ref[...].astype(BF16), p, wout_ref, g1_ref, b1_ref, alpha)
    x2_ref[...] = x2
    qx = jnp.dot(x2.astype(BF16), wxq_ref[...], preferred_element_type=F32) * HEAD_DIM_X ** -0.5
    qx_ref[...] = qx.astype(BF16)


def _mix_sample_call(x1, a, u, state_t, pw, ps, wout, g1, b1, wxq, *, alpha, pos):
    n = x1.shape[0]
    args = (x1, a, u, state_t, pw, ps, wout, g1, b1, wxq)
    return pl.pallas_call(
        functools.partial(_mix_sample_kernel, alpha=alpha, pos=pos),
        grid=(1,),
        in_specs=[_resident(v.shape) for v in args],
        out_specs=[_resident((n, D_MODEL)), _resident((n, D_MODEL))],
        out_shape=[jax.ShapeDtypeStruct((n, D_MODEL), F32), jax.ShapeDtypeStruct((n, D_MODEL), BF16)],
        compiler_params=_params(1),
        name="mix_sample",
    )(*args)


def _xattn_sample_kernel(qx_ref, mk_ref, mv_ref, o_ref):
    qm = _mask_heads(qx_ref[0], (SUBLANES, D_MODEL), HEAD_DIM_X)
    logits = lax.dot_general(qm, mk_ref[0].astype(BF16), _NT, preferred_element_type=F32)
    e = jnp.exp(logits - jnp.max(logits, axis=-1, keepdims=True))
    o = jnp.dot(e.astype(BF16), mv_ref[0].astype(BF16), preferred_element_type=F32)
    o = o / jnp.sum(e, axis=-1, keepdims=True)
    row = lax.broadcasted_iota(jnp.int32, o.shape, 0)
    lane_head = lax.broadcasted_iota(jnp.int32, o.shape, 1) // HEAD_DIM_X
    o_ref[0] = jnp.sum(jnp.where(lane_head == row, o, 0.0), axis=0, keepdims=True)


def _xattn_sample_call(qx, mk, mv):
    n_dec, n_mem, _ = mk.shape
    row = pl.BlockSpec((1, 1, D_MODEL), lambda b: (b, 0, 0))
    mem = pl.BlockSpec((1, n_mem, D_MODEL), lambda b: (b, 0, 0))
    return pl.pallas_call(
        _xattn_sample_kernel,
        grid=(n_dec,),
        in_specs=[row, mem, mem],
        out_specs=row,
        out_shape=jax.ShapeDtypeStruct((n_dec, 1, D_MODEL), F32),
        compiler_params=_params(1),
        name="xattn_sample",
    )(qx, mk, mv)


def _proj_ln_kernel(x_ref, o_ref, w_ref, g_ref, b_ref, y_ref, *, alpha):
    proj = jnp.dot(o_ref[...].astype(BF16), w_ref[...], preferred_element_type=F32)
    y_ref[...] = _layer_norm(alpha * x_ref[...] + proj, g_ref[...], b_ref[...])


def _proj_ln_call(x, o, w, g, b, *, alpha):
    args = (x, o, w, g, b)
    return pl.pallas_call(
        functools.partial(_proj_ln_kernel, alpha=alpha),
        grid=(1,),
        in_specs=[_resident(v.shape) for v in args],
        out_specs=_resident(x.shape),
        out_shape=jax.ShapeDtypeStruct(x.shape, F32),
        compiler_params=_params(1),
        name="proj_ln",
    )(*args)


def _t5_bucket(dist):
    n = jnp.maximum(dist, 0)
    max_exact = N_BUCKETS // 2
    nf = jnp.maximum(n, 1).astype(F32)
    large = max_exact + (jnp.log(nf / max_exact) / math.log(MAX_DISTANCE / max_exact)
                         * (N_BUCKETS - max_exact)).astype(jnp.int32)
    return jnp.where(n < max_exact, n, jnp.minimum(large, N_BUCKETS - 1))


def kernel(x_prompt, x_sample, cache_k, cache_v, cache_mem_k, cache_mem_v, state_pool, page_table, mem_prompt, rel_bias, ln_g, ln_b, w_ff1_gate, w_ff1_up, w_ff1_down, w_in, pool_w, pool_scale, w_out, w_xq, w_xk, w_xv, w_xo, w_ff2_gate, w_ff2_up, w_ff2_down):
    n_batch, seq, _ = x_prompt.shape
    n_dec, dec_seq, _ = x_sample.shape
    depth = ln_g.shape[0]
    n_pool = cache_k.shape[1]
    n_pages = page_table.shape[1]
    past_len = n_pages * PAGE_SIZE
    n_mem = mem_prompt.shape[1]
    assert depth == 1 and dec_seq == 1
    assert seq % MOBA_BLOCK == 0 and past_len % MOBA_BLOCK == 0
    assert past_len // MOBA_BLOCK >= MOBA_TOPK
    alpha = (2 * depth) ** 0.25
    tm = min(512, seq)
    n_blk = seq // MOBA_BLOCK

    bf = lambda w: w[0].astype(BF16)
    ff1 = (bf(w_ff1_gate), bf(w_ff1_up), bf(w_ff1_down))
    ff2 = (bf(w_ff2_gate), bf(w_ff2_up), bf(w_ff2_down))
    win = bf(w_in)
    ln = lambda i: (ln_g[0, i:i + 1], ln_b[0, i:i + 1])
    (g0, b0), (g1, b1), (g2, b2), (g3, b3) = ln(0), ln(1), ln(2), ln(3)
    pw, ps = pool_w[0].astype(BF16), pool_scale
    wout, wxq, wxo = bf(w_out), bf(w_xq), bf(w_xo)

    bias_by_dist = rel_bias[_t5_bucket(jnp.arange(2 * MOBA_BLOCK, dtype=jnp.int32))].T
    bfar = rel_bias[_far_bucket()]

    xp = x_prompt.reshape(n_batch * seq, D_MODEL)
    attn_scale = HEAD_DIM ** -0.5
    x1, q, kt, vt, u, kb, vtb, km = _ffn_proj_call(xp, *ff1, g0, b0, win, alpha=alpha, q_scale=attn_scale * LOG2E,
                                                   tm=tm, moba_batch=n_batch)
    kpool_t = cache_k[0].transpose(0, 2, 3, 1).reshape(n_pool, D_A, PAGE_SIZE)
    vpool_t = cache_v[0].transpose(0, 2, 3, 1).reshape(n_pool, D_A, PAGE_SIZE)
    a, kmean_t = _moba_call(rel_bias.reshape(-1) * LOG2E, page_table, q, kb.reshape(n_batch, seq, D_A), vtb,
                            km.reshape(n_batch, n_blk, D_A), kpool_t)
    mk, mv, mkbt, mvb = _memkv_call(mem_prompt.reshape(n_batch * n_mem, D_MODEL), bf(w_xk), bf(w_xv), n_mem=n_mem)
    x3 = _mix_call(x1, a.reshape(n_batch * seq, D_A), u, pw, ps, wout, g1, b1, wxq, wxo,
                   mkbt, mvb.reshape(n_batch, n_mem, D_MODEL), g2, b2, alpha=alpha, n_batch=n_batch, tm=tm)
    y_prompt = _ffn_call(x3, *ff2, g3, b3, alpha=alpha, tm=tm).reshape(n_batch, seq, D_MODEL)

    to_heads = lambda t: t.reshape(t.shape[0], N_HEADS_A, HEAD_DIM, t.shape[2]).transpose(0, 3, 1, 2)[None]
    k_prompt, v_prompt = to_heads(kt), to_heads(vt)
    pool_prompt = u.reshape(n_batch, seq, D_POOL)[None, :, seq - POOL_STATE:]
    memk_prompt = mk.reshape(1, n_batch, n_mem, N_HEADS_X, HEAD_DIM_X)
    memv_prompt = mv.reshape(1, n_batch, n_mem, N_HEADS_X, HEAD_DIM_X)

    xs = x_sample.reshape(n_dec, D_MODEL)
    x1s, qs, kn, vn, us = _ffn_proj_call(xs, *ff1, g0, b0, win, alpha=alpha, q_scale=attn_scale, tm=n_dec)
    n_blk_s = past_len // MOBA_BLOCK
    qs3 = qs.reshape(n_dec, 1, D_A)
    top = _gate_call(qs3, kmean_t, n_blk_s)[:, :, :MOBA_TOPK].reshape(-1)
    tnear = bias_by_dist[:, MOBA_BLOCK - jnp.arange(MOBA_BLOCK)]
    a_s = _sample_attn_call(top, page_table, qs3, kn.reshape(n_dec, 1, D_A), vn.reshape(n_dec, 1, D_A),
                            tnear, bias_by_dist[:, 0], bfar, kpool_t, vpool_t, n_blk_s)
    state_t = state_pool[0].transpose(1, 0, 2)
    x2s, qxs = _mix_sample_call(x1s, a_s.reshape(n_dec, D_A), us, state_t, pw, ps, wout, g1, b1, wxq,
                                alpha=alpha, pos=past_len)
    o_s = _xattn_sample_call(qxs.reshape(n_dec, 1, D_MODEL),
                             cache_mem_k[0].reshape(n_dec, n_mem, D_MODEL),
                             cache_mem_v[0].reshape(n_dec, n_mem, D_MODEL))
    x3s = _proj_ln_call(x2s, o_s.reshape(n_dec, D_MODEL), wxo, g2, b2, alpha=alpha)
    y_sample = _ffn_call(x3s, *ff2, g3, b3, alpha=alpha, tm=n_dec).reshape(n_dec, 1, D_MODEL)

    k_sample = kn.reshape(1, n_dec, 1, N_HEADS_A, HEAD_DIM)
    v_sample = vn.reshape(1, n_dec, 1, N_HEADS_A, HEAD_DIM)
    pool_sample = jnp.concatenate([state_t[1:], us[None]], axis=0).transpose(1, 0, 2)[None]
    return (y_prompt, y_sample, k_prompt, v_prompt, pool_prompt, memk_prompt, memv_prompt,
            k_sample, v_sample, pool_sample)
```

```python
import functools
import math

import jax
import jax.numpy as jnp
import numpy as np
from jax import lax
from jax.experimental import pallas as pl
from jax.experimental.pallas import tpu as pltpu

F32 = jnp.float32
BF16 = jnp.bfloat16

D_MODEL = 1024
HEAD_DIM = 64
N_HEADS_A = 8
D_A = N_HEADS_A * HEAD_DIM
MOBA_BLOCK = 256
MOBA_TOPK = 3
D_POOL = D_MODEL - D_A
POOL_WINDOWS = (2, 4, 8, 16)
POOL_GROUP = D_POOL // len(POOL_WINDOWS)
POOL_STATE = max(POOL_WINDOWS) - 1
N_BUCKETS = 32
MAX_DISTANCE = 128
N_HEADS_X = 4
HEAD_DIM_X = D_MODEL // N_HEADS_X
D_FF = 2816
LN_EPS = 1e-5
PAGE_SIZE = 128
NEG = -1e30

LANES = 128
SUBLANES = 8
VMEM_LIMIT_BYTES = 56 * 1024 * 1024

BF16_SUBLANES = 16
HEADS_PER_TILE = LANES // HEAD_DIM
V_ROWS = HEAD_DIM + BF16_SUBLANES
LOG2E = math.log2(math.e)
POOL_HALO = POOL_STATE + 1
PAGES_PER_BLOCK = MOBA_BLOCK // PAGE_SIZE
FFN_CHUNK = 256
KMEAN_RING = 16
SCORE_CHUNK = 32

_NT = (((1,), (1,)), ((), ()))


def _params(n_grid_dims):
    return pltpu.CompilerParams(
        dimension_semantics=("arbitrary",) * n_grid_dims, vmem_limit_bytes=VMEM_LIMIT_BYTES)


def _resident(shape):
    return pl.BlockSpec(shape, lambda *_: (0,) * len(shape), pipeline_mode=pl.Buffered(1))


def _layer_norm(z, g, b):
    mu = jnp.mean(z, axis=-1, keepdims=True)
    zc = z - mu
    var = jnp.mean(zc * zc, axis=-1, keepdims=True)
    return zc * lax.rsqrt(var + LN_EPS) * g + b


def _swiglu(x, wg_ref, wu_ref, wd_ref):
    xb = x.astype(BF16)
    acc = None
    for c in range(D_FF // FFN_CHUNK):
        sl = slice(c * FFN_CHUNK, (c + 1) * FFN_CHUNK)
        g = jnp.dot(xb, wg_ref[:, sl], preferred_element_type=F32)
        u = jnp.dot(xb, wu_ref[:, sl], preferred_element_type=F32)
        h = (g * jax.nn.sigmoid(g) * u).astype(BF16)
        part = jnp.dot(h, wd_ref[sl, :], preferred_element_type=F32)
        acc = part if acc is None else acc + part
    return acc


def _ffn_kernel(x_ref, wg_ref, wu_ref, wd_ref, g_ref, b_ref, y_ref, *, alpha):
    x = x_ref[...]
    y_ref[...] = _layer_norm(alpha * x + 0.5 * _swiglu(x, wg_ref, wu_ref, wd_ref), g_ref[...], b_ref[...])


def _ffn_proj_kernel(x_ref, wg_ref, wu_ref, wd_ref, g_ref, b_ref, win_ref,
                     x1_ref, q_ref, k_ref, v_ref, u_ref, *moba_refs, alpha, q_scale):
    x = x_ref[...]
    x1 = _layer_norm(alpha * x + 0.5 * _swiglu(x, wg_ref, wu_ref, wd_ref), g_ref[...], b_ref[...])
    x1_ref[...] = x1
    h = jnp.dot(x1.astype(BF16), win_ref[...], preferred_element_type=F32)
    q = h[:, :D_A] * q_scale
    k = h[:, D_A:2 * D_A]
    v = h[:, 2 * D_A:3 * D_A]
    u_ref[...] = h[:, 3 * D_A:]
    if not moba_refs:
        q_ref[...] = q.astype(BF16)
        k_ref[...] = k
        v_ref[...] = v
        return
    kb_ref, vtb_ref, km_ref = moba_refs
    vt = v.T
    q_ref[0] = q.T.astype(BF16)
    k_ref[0] = k.T
    v_ref[0] = vt
    kb_ref[...] = k.astype(BF16)
    ones = jnp.ones((BF16_SUBLANES, MOBA_BLOCK), BF16)
    for blk in range(k.shape[0] // MOBA_BLOCK):
        rows = slice(blk * MOBA_BLOCK, (blk + 1) * MOBA_BLOCK)
        for hd in range(N_HEADS_A):
            vtb_ref[0, blk, hd * V_ROWS:hd * V_ROWS + HEAD_DIM, :] = (
                vt[hd * HEAD_DIM:(hd + 1) * HEAD_DIM, rows].astype(BF16))
            vtb_ref[0, blk, hd * V_ROWS + HEAD_DIM:(hd + 1) * V_ROWS, :] = ones
        km_ref[blk] = jnp.mean(k[rows], axis=0, keepdims=True)


def _ffn_call(x, wg, wu, wd, g, b, *, alpha, tm):
    n = x.shape[0]
    row = pl.BlockSpec((tm, D_MODEL), lambda i: (i, 0))
    return pl.pallas_call(
        functools.partial(_ffn_kernel, alpha=alpha),
        grid=(n // tm,),
        in_specs=[row, _resident(wg.shape), _resident(wu.shape), _resident(wd.shape),
                  _resident(g.shape), _resident(b.shape)],
        out_specs=row,
        out_shape=jax.ShapeDtypeStruct((n, D_MODEL), F32),
        compiler_params=_params(1),
        name="ffn",
    )(x, wg, wu, wd, g, b)


def _ffn_proj_call(x, wg, wu, wd, g, b, win, *, alpha, q_scale, tm, moba_batch=None):
    n = x.shape[0]
    row = lambda w: pl.BlockSpec((tm, w), lambda i: (i, 0))
    sds = jax.ShapeDtypeStruct
    out_specs = [row(D_MODEL)]
    out_shape = [sds((n, D_MODEL), F32)]
    if moba_batch is None:
        out_specs += [row(D_A), row(D_A), row(D_A), row(D_POOL)]
        out_shape += [sds((n, D_A), BF16), sds((n, D_A), F32), sds((n, D_A), F32), sds((n, D_POOL), F32)]
    else:
        seq = n // moba_batch
        tpb = seq // tm
        bpt = tm // MOBA_BLOCK
        tcol = pl.BlockSpec((1, D_A, tm), lambda i: (i // tpb, 0, i % tpb))
        out_specs += [tcol, tcol, tcol, row(D_POOL), row(D_A),
                      pl.BlockSpec((1, bpt, N_HEADS_A * V_ROWS, MOBA_BLOCK), lambda i: (i // tpb, i % tpb, 0, 0)),
                      pl.BlockSpec((bpt, 1, D_A), lambda i: (i, 0, 0))]
        out_shape += [sds((moba_batch, D_A, seq), BF16),
                      sds((moba_batch, D_A, seq), F32), sds((moba_batch, D_A, seq), F32), sds((n, D_POOL), F32),
                      sds((n, D_A), BF16),
                      sds((moba_batch, seq // MOBA_BLOCK, N_HEADS_A * V_ROWS, MOBA_BLOCK), BF16),
                      sds((n // MOBA_BLOCK, 1, D_A), F32)]
    return pl.pallas_call(
        functools.partial(_ffn_proj_kernel, alpha=alpha, q_scale=q_scale),
        grid=(n // tm,),
        in_specs=[row(D_MODEL), _resident(wg.shape), _resident(wu.shape), _resident(wd.shape),
                  _resident(g.shape), _resident(b.shape), _resident(win.shape)],
        out_specs=out_specs,
        out_shape=out_shape,
        compiler_params=_params(1),
        name="ffn_proj",
    )(x, wg, wu, wd, g, b, win)


def _memkv_kernel(m_ref, wk_ref, wv_ref, k_ref, v_ref, kbt_ref, vb_ref):
    mb = m_ref[...].astype(BF16)
    k = jnp.dot(mb, wk_ref[...], preferred_element_type=F32)
    v = jnp.dot(mb, wv_ref[...], preferred_element_type=F32)
    k_ref[...] = k
    v_ref[...] = v
    kbt_ref[0] = k.T.astype(BF16)
    vb_ref[...] = v.astype(BF16)


def _memkv_call(mem, wk, wv, *, n_mem):
    n = mem.shape[0]
    row = pl.BlockSpec((n_mem, D_MODEL), lambda i: (i, 0))
    sds = lambda dt: jax.ShapeDtypeStruct((n, D_MODEL), dt)
    return pl.pallas_call(
        _memkv_kernel,
        grid=(n // n_mem,),
        in_specs=[row, _resident(wk.shape), _resident(wv.shape)],
        out_specs=[row, row, pl.BlockSpec((1, D_MODEL, n_mem), lambda i: (i, 0, 0)), row],
        out_shape=[sds(F32), sds(F32), jax.ShapeDtypeStruct((n // n_mem, D_MODEL, n_mem), BF16), sds(BF16)],
        compiler_params=_params(1),
        name="memkv",
    )(mem, wk, wv)


def _top3_rows(gate, n_valid, n_rows):
    rid = lax.broadcasted_iota(jnp.int32, gate.shape, 0).astype(F32)
    g = jnp.where(rid < n_valid, gate, NEG)
    sel = jnp.zeros(gate.shape, jnp.bool_)
    for _ in range(MOBA_TOPK):
        m = jnp.max(g, axis=0, keepdims=True)
        idx = jnp.min(jnp.where(g == m, rid, float(n_rows)), axis=0, keepdims=True)
        pick = rid == idx
        sel = jnp.logical_or(sel, jnp.logical_and(pick, idx < n_valid))
        g = jnp.where(pick, -jnp.inf, g)
    return sel


def _moba_kernel(rb_ref, pt_ref, q_ref, kb_ref, vtb_ref, km_ref, bkt_ref, kpool_ref, o_ref, kmo_ref,
                 tbl_ref, rbt_ref, qh_ref, s_ref, p_ref, acc_ref, m_ref, smx_ref, al_ref,
                 ring_ref, ring_sem, kacc_ref, out_sem, *, n_blk, far_bucket, n_dec, n_pool_blk_per_dec):
    b = pl.program_id(0)
    i = pl.program_id(1)
    row_zero, row_prev = n_blk, n_blk + 1
    n_chunk = MOBA_BLOCK // SCORE_CHUNK
    chunks = [slice(c * SCORE_CHUNK, (c + 1) * SCORE_CHUNK) for c in range(n_chunk)]
    head_lanes = lambda h: slice((h // HEADS_PER_TILE) * LANES, (h // HEADS_PER_TILE + 1) * LANES)

    @pl.when(jnp.logical_and(b == 0, i == 0))
    def _build_bias_tiles():
        def per_head(h, _):
            for slot in range(2):
                bk = bkt_ref[slot]
                t = jnp.zeros(bk.shape, F32)
                for bucket in range(N_BUCKETS):
                    t = jnp.where(bk == bucket, rb_ref[bucket * N_HEADS_A + h], t)
                tbl_ref[h, slot] = jnp.where(bk < 0, NEG, t)
            return 0
        lax.fori_loop(0, N_HEADS_A, per_head, 0)

    qt = q_ref[0]
    km = km_ref[0]
    i_f = i.astype(F32)
    jp_f = jnp.maximum(i - 1, 0).astype(F32)
    row_head = lax.broadcasted_iota(jnp.int32, (LANES, MOBA_BLOCK), 0) // HEAD_DIM
    rid = lax.broadcasted_iota(jnp.int32, (n_blk, MOBA_BLOCK), 0).astype(F32)
    for h in range(N_HEADS_A):
        q2 = qt[head_lanes(h), :].astype(F32)
        qh = jnp.where(row_head == h % HEADS_PER_TILE, q2, 0.0).astype(BF16)
        qh_ref[h] = qh
        gate = jnp.dot(km[:, head_lanes(h)].astype(BF16), qh, preferred_element_type=F32)
        sel = _top3_rows(gate, i_f, n_blk)
        rbt_ref[h, :n_blk] = jnp.where(sel, rb_ref[far_bucket * N_HEADS_A + h], NEG)
        prev_sel = jnp.max(jnp.where(jnp.logical_and(sel, rid == jp_f), 1.0, 0.0), axis=0, keepdims=True)
        rbt_ref[h, row_zero:row_zero + 1] = jnp.zeros((1, MOBA_BLOCK), F32)
        rbt_ref[h, row_prev:row_prev + 1] = jnp.where(prev_sel > 0.5, 0.0, NEG)
        m_ref[h, :1] = jnp.full((1, MOBA_BLOCK), NEG, F32)
        al_ref[h, :1] = jnp.ones((1, MOBA_BLOCK), F32)
        acc_ref[h] = jnp.zeros((V_ROWS, MOBA_BLOCK), F32)
        p_ref[h] = jnp.zeros((MOBA_BLOCK, MOBA_BLOCK), BF16)

    def list_block(t):
        return jnp.clip(jnp.where(t == 0, i, jnp.where(t == 1, i - 1, t - 2)), 0, n_blk - 1)

    def stage_a(t):
        rows = pl.ds(pl.multiple_of(list_block(t) * MOBA_BLOCK, MOBA_BLOCK), MOBA_BLOCK)
        for h in range(N_HEADS_A):
            s = jnp.dot(kb_ref[0, rows, head_lanes(h)], qh_ref[h], preferred_element_type=F32)
            s_ref[h] = s
            smx_ref[h, :1] = jnp.max(s, axis=0, keepdims=True)

    def add_bias_tile(slot):
        for h in range(N_HEADS_A):
            s = s_ref[h] + tbl_ref[h, slot]
            s_ref[h] = s
            smx_ref[h, :1] = jnp.max(s, axis=0, keepdims=True)

    def stage_b(t):
        row = jnp.where(t == 0, row_zero, jnp.where(t == 1, row_prev, t - 2))
        for h in range(N_HEADS_A):
            rb = rbt_ref[h, pl.ds(row, 1), :]
            m_old = m_ref[h, :1]
            m_new = jnp.maximum(m_old, smx_ref[h, :1] + rb)
            shift = m_new - rb
            for c in chunks:
                p_ref[h, c, :] = jnp.exp2(s_ref[h, c, :] - shift).astype(BF16)
            m_ref[h, :1] = m_new
            al_ref[h, :1] = jnp.exp2(m_old - m_new)

    def stage_c(t):
        j = list_block(t)
        for h in range(N_HEADS_A):
            vt = vtb_ref[0, j, h * V_ROWS:(h + 1) * V_ROWS, :]
            acc_ref[h] = al_ref[h, :1] * acc_ref[h] + jnp.dot(vt, p_ref[h], preferred_element_type=F32)

    n_pool_blk = n_dec * n_pool_blk_per_dec
    n_pool_pages = n_pool_blk * PAGES_PER_BLOCK
    ring = min(KMEAN_RING, n_pool_pages)
    iters_per_batch = n_blk * (n_blk + 1) // 2
    lane = lax.broadcasted_iota(jnp.int32, (D_A, LANES), 1)

    def page_copy(page_no, slot):
        return pltpu.make_async_copy(kpool_ref.at[pt_ref[page_no]], ring_ref.at[slot], ring_sem.at[slot])

    def pool_block_mean(g):
        g = jnp.minimum(g, n_pool_blk - 1)
        total = None
        for half in range(PAGES_PER_BLOCK):
            page_no = g * PAGES_PER_BLOCK + half
            slot = page_no % ring
            page_copy(page_no, slot).wait()
            x = ring_ref[slot]
            total = x if total is None else total + x
            nxt = jnp.where(page_no + ring < n_pool_pages, page_no + ring, page_no)
            page_copy(nxt, slot).start()
        col = jnp.sum(total, axis=1, keepdims=True) * (1.0 / MOBA_BLOCK)
        dec, j = g // n_pool_blk_per_dec, g % n_pool_blk_per_dec
        kacc_ref[dec] = jnp.where(lane == j, col, kacc_ref[dec])

    @pl.when(jnp.logical_and(b == 0, i == 0))
    def _start_ring():
        for slot in range(ring):
            page_copy(slot, slot).start()
        kacc_ref[...] = jnp.zeros(kacc_ref.shape, F32)

    stage_a(0)
    add_bias_tile(0)
    g0 = b * iters_per_batch + i * (i + 1) // 2

    def step(t, carry):
        pool_block_mean(g0 + t - 1)
        stage_c(jnp.maximum(t - 2, 0))
        stage_b(t - 1)
        stage_a(t)

        @pl.when(t == 1)
        def _():
            add_bias_tile(1)

        return carry

    lax.fori_loop(1, i + 1, step, 0)
    pool_block_mean(g0 + i)
    stage_c(jnp.maximum(i - 1, 0))
    stage_b(i)
    stage_c(i)
    out_t = jnp.concatenate([acc_ref[h, :HEAD_DIM] / acc_ref[h, HEAD_DIM:HEAD_DIM + 1]
                             for h in range(N_HEADS_A)], axis=0)
    o_ref[0] = out_t.T.astype(BF16)

    @pl.when(jnp.logical_and(b == pl.num_programs(0) - 1, i == n_blk - 1))
    def _finish_pool_means():
        n_done = pl.num_programs(0) * iters_per_batch

        def rest(g, carry):
            pool_block_mean(g)
            return carry

        lax.fori_loop(jnp.minimum(n_done, n_pool_blk), n_pool_blk, rest, 0)
        for slot in range(ring):
            page_copy(slot, slot).wait()
        out_copy = pltpu.make_async_copy(kacc_ref, kmo_ref, out_sem.at[0])
        out_copy.start()
        out_copy.wait()


def _bucket_np(dist):
    n = np.maximum(dist, 0)
    max_exact = N_BUCKETS // 2
    nf = np.maximum(n, 1).astype(np.float32)
    large = max_exact + (np.log(nf / np.float32(max_exact)) / np.float32(math.log(MAX_DISTANCE / max_exact))
                         * np.float32(N_BUCKETS - max_exact)).astype(np.int32)
    return np.where(n < max_exact, n, np.minimum(large, N_BUCKETS - 1)).astype(np.int32)


def _bucket_tiles():
    key = np.arange(MOBA_BLOCK)[:, None]
    qry = np.arange(MOBA_BLOCK)[None, :]
    own = np.where(qry >= key, _bucket_np(qry - key), -1)
    prev = _bucket_np(qry - key + MOBA_BLOCK)
    return np.stack([own, prev]).astype(np.int32)


def _far_bucket():
    far = _bucket_np(np.array([MOBA_BLOCK + 1, 1 << 30]))
    assert far[0] == far[1]
    return int(far[0])


def _moba_call(rel_bias_flat, page_table, q, kb, vtb, km, kpool_t):
    n_batch, _, seq = q.shape
    n_blk = seq // MOBA_BLOCK
    n_dec, n_pages = page_table.shape
    assert n_pages % PAGES_PER_BLOCK == 0 and n_pages // PAGES_PER_BLOCK <= LANES
    qo = pl.BlockSpec((1, MOBA_BLOCK, D_A), lambda b, i, *_: (b, i, 0))

    def per_batch(shape, **kw):
        return pl.BlockSpec((1,) + shape, lambda b, i, *_: (b,) + (0,) * len(shape), **kw)

    once = dict(pipeline_mode=pl.Buffered(1))
    any_ = pl.BlockSpec(memory_space=pl.ANY)
    tile = (MOBA_BLOCK, MOBA_BLOCK)
    stat = pltpu.VMEM((N_HEADS_A, SUBLANES, MOBA_BLOCK), F32)
    return pl.pallas_call(
        functools.partial(_moba_kernel, n_blk=n_blk, far_bucket=_far_bucket(), n_dec=n_dec,
                          n_pool_blk_per_dec=n_pages // PAGES_PER_BLOCK),
        grid_spec=pltpu.PrefetchScalarGridSpec(
            num_scalar_prefetch=2,
            grid=(n_batch, n_blk),
            in_specs=[pl.BlockSpec((1, D_A, MOBA_BLOCK), lambda b, i, *_: (b, 0, i)), per_batch((seq, D_A), **once),
                      per_batch((n_blk, N_HEADS_A * V_ROWS, MOBA_BLOCK), **once),
                      per_batch((n_blk, D_A)), _resident((2,) + tile), any_],
            out_specs=[qo, any_],
            scratch_shapes=[pltpu.VMEM((N_HEADS_A, 2) + tile, F32),
                            pltpu.VMEM((N_HEADS_A, n_blk + SUBLANES, MOBA_BLOCK), F32),
                            pltpu.VMEM((N_HEADS_A, LANES, MOBA_BLOCK), BF16),
                            pltpu.VMEM((N_HEADS_A,) + tile, F32),
                            pltpu.VMEM((N_HEADS_A,) + tile, BF16),
                            pltpu.VMEM((N_HEADS_A, V_ROWS, MOBA_BLOCK), F32),
                            stat, stat, stat,
                            pltpu.VMEM((KMEAN_RING, D_A, PAGE_SIZE), F32),
                            pltpu.SemaphoreType.DMA((KMEAN_RING,)),
                            pltpu.VMEM((n_dec, D_A, LANES), F32),
                            pltpu.SemaphoreType.DMA((1,))]),
        out_shape=[jax.ShapeDtypeStruct((n_batch, seq, D_A), BF16),
                   jax.ShapeDtypeStruct((n_dec, D_A, LANES), F32)],
        compiler_params=_params(2),
        name="moba_prompt",
    )(rel_bias_flat, page_table.reshape(-1), q, kb, vtb, km, jnp.asarray(_bucket_tiles()), kpool_t)


def _pool_project(window_sum, u_new, cnt, pw_ref, ps_ref):
    ys = []
    for g, w in enumerate(POOL_WINDOWS):
        lanes = slice(g * POOL_GROUP, (g + 1) * POOL_GROUP)
        d = window_sum(g, w) / cnt(w) - u_new[:, lanes]
        ys.append(jnp.dot(d.astype(BF16), pw_ref[g], preferred_element_type=F32))
    return jnp.concatenate(ys, axis=1) * ps_ref[...]


def _out_project(x1, a, p, wout_ref, g_ref, b_ref, alpha):
    proj = (jnp.dot(a, wout_ref[:D_A, :], preferred_element_type=F32)
            + jnp.dot(p.astype(BF16), wout_ref[D_A:, :], preferred_element_type=F32))
    return _layer_norm(alpha * x1 + proj, g_ref[...], b_ref[...])


def _mix_kernel(x1_ref, a_ref, u_ref, uh_ref, pw_ref, ps_ref, wout_ref, g1_ref, b1_ref,
                wxq_ref, wxo_ref, mkt_ref, mv_ref, g2_ref, b2_ref, x3_ref, ext_ref, *, alpha, tm, tpb):
    t_in_b = pl.program_id(0) % tpb
    ext_ref[:POOL_HALO, :] = jnp.where(t_in_b == 0, 0.0, uh_ref[...])
    ext_ref[POOL_HALO:, :] = u_ref[...]
    pos = t_in_b * tm + lax.broadcasted_iota(jnp.int32, (tm, 1), 0)

    def window_sum(g, w):
        lanes = slice(g * POOL_GROUP, (g + 1) * POOL_GROUP)
        s = ext_ref[POOL_HALO:POOL_HALO + tm, lanes]
        for back in range(1, w):
            s = s + ext_ref[POOL_HALO - back:POOL_HALO - back + tm, lanes]
        return s

    p = _pool_project(window_sum, u_ref[...], lambda w: jnp.minimum(w, pos + 1).astype(F32), pw_ref, ps_ref)
    x2 = _out_project(x1_ref[...], a_ref[...], p, wout_ref, g1_ref, b1_ref, alpha)

    qx = jnp.dot(x2.astype(BF16), wxq_ref[...], preferred_element_type=F32) * HEAD_DIM_X ** -0.5
    outs = []
    for h in range(N_HEADS_X):
        lanes = slice(h * HEAD_DIM_X, (h + 1) * HEAD_DIM_X)
        logits = jnp.dot(qx[:, lanes].astype(BF16), mkt_ref[0, lanes, :], preferred_element_type=F32)
        e = jnp.exp(logits - jnp.max(logits, axis=-1, keepdims=True))
        o = jnp.dot(e.astype(BF16), mv_ref[0, :, lanes], preferred_element_type=F32)
        outs.append(o / jnp.sum(e, axis=-1, keepdims=True))
    o = jnp.concatenate(outs, axis=1).astype(BF16)
    x3_ref[...] = _layer_norm(alpha * x2 + jnp.dot(o, wxo_ref[...], preferred_element_type=F32),
                              g2_ref[...], b2_ref[...])


def _mix_call(x1, a, u, pw, ps, wout, g1, b1, wxq, wxo, mkt, mv, g2, b2, *, alpha, n_batch, tm):
    n = x1.shape[0]
    tpb = n // n_batch // tm
    n_mem = mv.shape[1]
    row = lambda w: pl.BlockSpec((tm, w), lambda i: (i, 0))
    halo = pl.BlockSpec((POOL_HALO, D_POOL), lambda i: (jnp.maximum(i * (tm // POOL_HALO) - 1, 0), 0))
    mem = pl.BlockSpec((1, n_mem, D_MODEL), lambda i: (i // tpb, 0, 0))
    memt = pl.BlockSpec((1, D_MODEL, n_mem), lambda i: (i // tpb, 0, 0))
    return pl.pallas_call(
        functools.partial(_mix_kernel, alpha=alpha, tm=tm, tpb=tpb),
        grid=(n // tm,),
        in_specs=[row(D_MODEL), row(D_A), row(D_POOL), halo, _resident(pw.shape), _resident(ps.shape),
                  _resident(wout.shape), _resident(g1.shape), _resident(b1.shape),
                  _resident(wxq.shape), _resident(wxo.shape), memt, mem,
                  _resident(g2.shape), _resident(b2.shape)],
        out_specs=row(D_MODEL),
        out_shape=jax.ShapeDtypeStruct((n, D_MODEL), F32),
        scratch_shapes=[pltpu.VMEM((tm + POOL_HALO, D_POOL), F32)],
        compiler_params=_params(1),
        name="mix_prompt",
    )(x1, a, u, u, pw, ps, wout, g1, b1, wxq, wxo, mkt, mv, g2, b2)


def _head_rows(q_row, n_rows):
    return _mask_heads(q_row, (n_rows, D_A), HEAD_DIM)


def _mask_heads(q_row, shape, head_dim):
    row = lax.broadcasted_iota(jnp.int32, shape, 0)
    lane_head = lax.broadcasted_iota(jnp.int32, shape, 1) // head_dim
    qb = jnp.broadcast_to(q_row.astype(F32), shape)
    return jnp.where(lane_head == row, qb, 0.0).astype(q_row.dtype)


def _gate_kernel(q_ref, km_ref, o_ref, *, n_blk):
    qm = _head_rows(q_ref[0], N_HEADS_A)
    gate = jnp.dot(qm, km_ref[0].astype(BF16), preferred_element_type=F32)
    lane = lax.broadcasted_iota(jnp.int32, gate.shape, 1)
    lane_f = lane.astype(F32)
    g = jnp.where(lane < n_blk, gate, NEG)
    out = jnp.zeros(gate.shape, F32)
    for t in range(MOBA_TOPK):
        m = jnp.max(g, axis=1, keepdims=True)
        idx = jnp.min(jnp.where(g == m, lane_f, float(LANES)), axis=1, keepdims=True)
        out = jnp.where(lane == t, idx, out)
        g = jnp.where(lane_f == idx, -jnp.inf, g)
    o_ref[0] = out.astype(jnp.int32)


def _gate_call(q, kmean_t, n_blk):
    n_dec = q.shape[0]
    return pl.pallas_call(
        functools.partial(_gate_kernel, n_blk=n_blk),
        grid=(n_dec,),
        in_specs=[pl.BlockSpec((1, 1, D_A), lambda b: (b, 0, 0)),
                  pl.BlockSpec((1, D_A, LANES), lambda b: (b, 0, 0))],
        out_specs=pl.BlockSpec((1, N_HEADS_A, LANES), lambda b: (b, 0, 0)),
        out_shape=jax.ShapeDtypeStruct((n_dec, N_HEADS_A, LANES), jnp.int32),
        compiler_params=_params(1),
        name="gate_sample",
    )(q, kmean_t)


def _head_page_copy(pool_ref, page, h, buf_ref, sem_ref, par, which, slot):
    rows = pl.ds(h * HEAD_DIM, HEAD_DIM)
    return pltpu.make_async_copy(pool_ref.at[page, rows, :], buf_ref.at[par, which, slot],
                                 sem_ref.at[par, which, slot])


def _sample_attn_kernel(top_ref, pt_ref, q_ref, kn_ref, vn_ref, tnear_ref, bias0_ref, bfar_ref,
                        kpool_ref, vpool_ref, o_ref, buf_ref, sem_ref, *, n_blk):
    b = pl.program_id(0)
    n_sel = MOBA_TOPK * PAGES_PER_BLOCK
    par = b % 2

    def copies(row, half_buf, h):
        out = []
        for t in range(MOBA_TOPK):
            blk = top_ref[(row * N_HEADS_A + h) * MOBA_TOPK + t]
            for half in range(PAGES_PER_BLOCK):
                page = pt_ref[row, blk * PAGES_PER_BLOCK + half]
                slot = h * n_sel + t * PAGES_PER_BLOCK + half
                out.append(_head_page_copy(kpool_ref, page, h, buf_ref, sem_ref, half_buf, 0, slot))
                out.append(_head_page_copy(vpool_ref, page, h, buf_ref, sem_ref, half_buf, 1, slot))
        return out

    def start_row(row, half_buf):
        for h in range(N_HEADS_A):
            for c in copies(row, half_buf, h):
                c.start()

    @pl.when(b == 0)
    def _():
        start_row(0, 0)

    @pl.when(b + 1 < pl.num_programs(0))
    def _():
        start_row(b + 1, 1 - par)

    for h in range(N_HEADS_A):
        for c in copies(b, par, h):
            c.wait()

    q = q_ref[0]
    kn = kn_ref[0].astype(BF16).astype(F32)
    vn = vn_ref[0].astype(BF16).astype(F32)
    s_new_all = q.astype(F32) * kn
    outs = []
    for h in range(N_HEADS_A):
        lanes = slice(h * HEAD_DIM, (h + 1) * HEAD_DIM)
        qh = jnp.broadcast_to(q[:, lanes], (SUBLANES, HEAD_DIM))
        kt = jnp.concatenate([buf_ref[par, 0, h * n_sel + s] for s in range(n_sel)], axis=1).astype(BF16)
        vt = jnp.concatenate([buf_ref[par, 1, h * n_sel + s] for s in range(n_sel)], axis=1).astype(BF16)
        bias = []
        for t in range(MOBA_TOPK):
            blk = top_ref[(b * N_HEADS_A + h) * MOBA_TOPK + t]
            bias.append(jnp.where(blk == n_blk - 1, tnear_ref[h:h + 1, :], bfar_ref[h]))
        s = jnp.dot(qh, kt, preferred_element_type=F32)[:1] + jnp.concatenate(bias, axis=1)
        s_new = jnp.sum(s_new_all[:, lanes], axis=1, keepdims=True) + bias0_ref[h]
        m = jnp.maximum(jnp.max(s, axis=1, keepdims=True), s_new)
        p = jnp.exp(s - m)
        p_new = jnp.exp(s_new - m)
        l = jnp.sum(p, axis=1, keepdims=True) + p_new
        pb = jnp.broadcast_to(p.astype(BF16), (SUBLANES, p.shape[1]))
        o = lax.dot_general(pb, vt, _NT, preferred_element_type=F32)[:1]
        o = o + p_new.astype(BF16).astype(F32) * vn[:, lanes]
        outs.append(o / l)
    o_ref[0] = jnp.concatenate(outs, axis=1)


def _sample_attn_call(top, page_table, q, kn, vn, tnear, bias0, bfar, kpool_t, vpool_t, n_blk):
    n_dec = q.shape[0]
    row = pl.BlockSpec((1, 1, D_A), lambda b, *_: (b, 0, 0))
    smem = pl.BlockSpec(memory_space=pltpu.SMEM)
    any_ = pl.BlockSpec(memory_space=pl.ANY)
    n_slots = N_HEADS_A * MOBA_TOPK * PAGES_PER_BLOCK
    return pl.pallas_call(
        functools.partial(_sample_attn_kernel, n_blk=n_blk),
        grid_spec=pltpu.PrefetchScalarGridSpec(
            num_scalar_prefetch=2,
            grid=(n_dec,),
            in_specs=[row, row, row, pl.BlockSpec(tnear.shape, lambda b, *_: (0, 0)), smem, smem, any_, any_],
            out_specs=row,
            scratch_shapes=[pltpu.VMEM((2, 2, n_slots, HEAD_DIM, PAGE_SIZE), F32),
                            pltpu.SemaphoreType.DMA((2, 2, n_slots))]),
        out_shape=jax.ShapeDtypeStruct((n_dec, 1, D_A), F32),
        compiler_params=_params(1),
        name="attn_sample",
    )(top, page_table, q, kn, vn, tnear, bias0, bfar, kpool_t, vpool_t)


def _mix_sample_kernel(x1_ref, a_ref, u_ref, st_ref, pw_ref, ps_ref, wout_ref, g1_ref, b1_ref, wxq_ref,
                       x2_ref, qx_ref, *, alpha, pos):
    u = u_ref[...]

    def window_sum(g, w):
        lanes = slice(g * POOL_GROUP, (g + 1) * POOL_GROUP)
        s = u[:, lanes]
        for back in range(1, w):
            s = s + st_ref[POOL_STATE - back, :, lanes]
        return s

    p = _pool_project(window_sum, u, lambda w: float(min(w, pos + 1)), pw_ref, ps_ref)
    x2 = _out_project(x1_ref[...], a_ref[...].astype(BF16), p, wout_ref, g1_ref, b1_ref, alpha)
    x2_ref[...] = x2
    qx = jnp.dot(x2.astype(BF16), wxq_ref[...], preferred_element_type=F32) * HEAD_DIM_X ** -0.5
    qx_ref[...] = qx.astype(BF16)


def _mix_sample_call(x1, a, u, state_t, pw, ps, wout, g1, b1, wxq, *, alpha, pos):
    n = x1.shape[0]
    args = (x1, a, u, state_t, pw, ps, wout, g1, b1, wxq)
    return pl.pallas_call(
        functools.partial(_mix_sample_kernel, alpha=alpha, pos=pos),
        grid=(1,),
        in_specs=[_resident(v.shape) for v in args],
        out_specs=[_resident((n, D_MODEL)), _resident((n, D_MODEL))],
        out_shape=[jax.ShapeDtypeStruct((n, D_MODEL), F32), jax.ShapeDtypeStruct((n, D_MODEL), BF16)],
        compiler_params=_params(1),
        name="mix_sample",
    )(*args)


def _xattn_sample_kernel(qx_ref, mk_ref, mv_ref, o_ref):
    qm = _mask_heads(qx_ref[0], (SUBLANES, D_MODEL), HEAD_DIM_X)
    logits = lax.dot_general(qm, mk_ref[0].astype(BF16), _NT, preferred_element_type=F32)
    e = jnp.exp(logits - jnp.max(logits, axis=-1, keepdims=True))
    o = jnp.dot(e.astype(BF16), mv_ref[0].astype(BF16), preferred_element_type=F32)
    o = o / jnp.sum(e, axis=-1, keepdims=True)
    row = lax.broadcasted_iota(jnp.int32, o.shape, 0)
    lane_head = lax.broadcasted_iota(jnp.int32, o.shape, 1) // HEAD_DIM_X
    o_ref[0] = jnp.sum(jnp.where(lane_head == row, o, 0.0), axis=0, keepdims=True)


def _xattn_sample_call(qx, mk, mv):
    n_dec, n_mem, _ = mk.shape
    row = pl.BlockSpec((1, 1, D_MODEL), lambda b: (b, 0, 0))
    mem = pl.BlockSpec((1, n_mem, D_MODEL), lambda b: (b, 0, 0))
    return pl.pallas_call(
        _xattn_sample_kernel,
        grid=(n_dec,),
        in_specs=[row, mem, mem],
        out_specs=row,
        out_shape=jax.ShapeDtypeStruct((n_dec, 1, D_MODEL), F32),
        compiler_params=_params(1),
        name="xattn_sample",
    )(qx, mk, mv)


def _proj_ln_kernel(x_ref, o_ref, w_ref, g_ref, b_ref, y_ref, *, alpha):
    proj = jnp.dot(o_ref[...].astype(BF16), w_ref[...], preferred_element_type=F32)
    y_ref[...] = _layer_norm(alpha * x_ref[...] + proj, g_ref[...], b_ref[...])


def _proj_ln_call(x, o, w, g, b, *, alpha):
    args = (x, o, w, g, b)
    return pl.pallas_call(
        functools.partial(_proj_ln_kernel, alpha=alpha),
        grid=(1,),
        in_specs=[_resident(v.shape) for v in args],
        out_specs=_resident(x.shape),
        out_shape=jax.ShapeDtypeStruct(x.shape, F32),
        compiler_params=_params(1),
        name="proj_ln",
    )(*args)


def _t5_bucket(dist):
    n = jnp.maximum(dist, 0)
    max_exact = N_BUCKETS // 2
    nf = jnp.maximum(n, 1).astype(F32)
    large = max_exact + (jnp.log(nf / max_exact) / math.log(MAX_DISTANCE / max_exact)
                         * (N_BUCKETS - max_exact)).astype(jnp.int32)
    return jnp.where(n < max_exact, n, jnp.minimum(large, N_BUCKETS - 1))


def kernel(x_prompt, x_sample, cache_k, cache_v, cache_mem_k, cache_mem_v, state_pool, page_table, mem_prompt, rel_bias, ln_g, ln_b, w_ff1_gate, w_ff1_up, w_ff1_down, w_in, pool_w, pool_scale, w_out, w_xq, w_xk, w_xv, w_xo, w_ff2_gate, w_ff2_up, w_ff2_down):
    n_batch, seq, _ = x_prompt.shape
    n_dec, dec_seq, _ = x_sample.shape
    depth = ln_g.shape[0]
    n_pool = cache_k.shape[1]
    n_pages = page_table.shape[1]
    past_len = n_pages * PAGE_SIZE
    n_mem = mem_prompt.shape[1]
    assert depth == 1 and dec_seq == 1
    assert seq % MOBA_BLOCK == 0 and past_len % MOBA_BLOCK == 0
    assert past_len // MOBA_BLOCK >= MOBA_TOPK
    alpha = (2 * depth) ** 0.25
    tm = min(512, seq)
    n_blk = seq // MOBA_BLOCK

    bf = lambda w: w[0].astype(BF16)
    ff1 = (bf(w_ff1_gate), bf(w_ff1_up), bf(w_ff1_down))
    ff2 = (bf(w_ff2_gate), bf(w_ff2_up), bf(w_ff2_down))
    win = bf(w_in)
    ln = lambda i: (ln_g[0, i:i + 1], ln_b[0, i:i + 1])
    (g0, b0), (g1, b1), (g2, b2), (g3, b3) = ln(0), ln(1), ln(2), ln(3)
    pw, ps = pool_w[0].astype(BF16), pool_scale
    wout, wxq, wxo = bf(w_out), bf(w_xq), bf(w_xo)

    bias_by_dist = rel_bias[_t5_bucket(jnp.arange(2 * MOBA_BLOCK, dtype=jnp.int32))].T
    bfar = rel_bias[_far_bucket()]

    xp = x_prompt.reshape(n_batch * seq, D_MODEL)
    attn_scale = HEAD_DIM ** -0.5
    x1, q, kt, vt, u, kb, vtb, km = _ffn_proj_call(xp, *ff1, g0, b0, win, alpha=alpha, q_scale=attn_scale * LOG2E,
                                                   tm=tm, moba_batch=n_batch)
    kpool_t = cache_k[0].transpose(0, 2, 3, 1).reshape(n_pool, D_A, PAGE_SIZE)
    vpool_t = cache_v[0].transpose(0, 2, 3, 1).reshape(n_pool, D_A, PAGE_SIZE)
    a, kmean_t = _moba_call(rel_bias.reshape(-1) * LOG2E, page_table, q, kb.reshape(n_batch, seq, D_A), vtb,
                            km.reshape(n_batch, n_blk, D_A), kpool_t)
    mk, mv, mkbt, mvb = _memkv_call(mem_prompt.reshape(n_batch * n_mem, D_MODEL), bf(w_xk), bf(w_xv), n_mem=n_mem)
    x3 = _mix_call(x1, a.reshape(n_batch * seq, D_A), u, pw, ps, wout, g1, b1, wxq, wxo,
                   mkbt, mvb.reshape(n_batch, n_mem, D_MODEL), g2, b2, alpha=alpha, n_batch=n_batch, tm=tm)
    y_prompt = _ffn_call(x3, *ff2, g3, b3, alpha=alpha, tm=tm).reshape(n_batch, seq, D_MODEL)

    to_heads = lambda t: t.reshape(t.shape[0], N_HEADS_A, HEAD_DIM, t.shape[2]).transpose(0, 3, 1, 2)[None]
    k_prompt, v_prompt = to_heads(kt), to_heads(vt)
    pool_prompt = u.reshape(n_batch, seq, D_POOL)[None, :, seq - POOL_STATE:]
    memk_prompt = mk.reshape(1, n_batch, n_mem, N_HEADS_X, HEAD_DIM_X)
    memv_prompt = mv.reshape(1, n_batch, n_mem, N_HEADS_X, HEAD_DIM_X)

    xs = x_sample.reshape(n_dec, D_MODEL)
    x1s, qs, kn, vn, us = _ffn_proj_call(xs, *ff1, g0, b0, win, alpha=alpha, q_scale=attn_scale, tm=n_dec)
    n_blk_s = past_len // MOBA_BLOCK
    qs3 = qs.reshape(n_dec, 1, D_A)
    top = _gate_call(qs3, kmean_t, n_blk_s)[:, :, :MOBA_TOPK].reshape(-1)
    tnear = bias_by_dist[:, MOBA_BLOCK - jnp.arange(MOBA_BLOCK)]
    a_s = _sample_attn_call(top, page_table, qs3, kn.reshape(n_dec, 1, D_A), vn.reshape(n_dec, 1, D_A),
                            tnear, bias_by_dist[:, 0], bfar, kpool_t, vpool_t, n_blk_s)
    state_t = state_pool[0].transpose(1, 0, 2)
    x2s, qxs = _mix_sample_call(x1s, a_s.reshape(n_dec, D_A), us, state_t, pw, ps, wout, g1, b1, wxq,
                                alpha=alpha, pos=past_len)
    o_s = _xattn_sample_call(qxs.reshape(n_dec, 1, D_MODEL),
                             cache_mem_k[0].reshape(n_dec, n_mem, D_MODEL),
                             cache_mem_v[0].reshape(n_dec, n_mem, D_MODEL))
    x3s = _proj_ln_call(x2s, o_s.reshape(n_dec, D_MODEL), wxo, g2, b2, alpha=alpha)
    y_sample = _ffn_call(x3s, *ff2, g3, b3, alpha=alpha, tm=n_dec).reshape(n_dec, 1, D_MODEL)

    k_sample = kn.reshape(1, n_dec, 1, N_HEADS_A, HEAD_DIM)
    v_sample = vn.reshape(1, n_dec, 1, N_HEADS_A, HEAD_DIM)
    pool_sample = jnp.concatenate([state_t[1:], us[None]], axis=0).transpose(1, 0, 2)[None]
    return (y_prompt, y_sample, k_prompt, v_prompt, pool_prompt, memk_prompt, memv_prompt,
            k_sample, v_sample, pool_sample)
```

```python
import functools
import math

import jax
import jax.numpy as jnp
import numpy as np
from jax import lax
from jax.experimental import pallas as pl
from jax.experimental.pallas import tpu as pltpu

F32 = jnp.float32
BF16 = jnp.bfloat16

D_MODEL = 1024
HEAD_DIM = 64
N_HEADS_A = 8
D_A = N_HEADS_A * HEAD_DIM
MOBA_BLOCK = 256
MOBA_TOPK = 3
D_POOL = D_MODEL - D_A
POOL_WINDOWS = (2, 4, 8, 16)
POOL_GROUP = D_POOL // len(POOL_WINDOWS)
POOL_STATE = max(POOL_WINDOWS) - 1
N_BUCKETS = 32
MAX_DISTANCE = 128
N_HEADS_X = 4
HEAD_DIM_X = D_MODEL // N_HEADS_X
D_FF = 2816
LN_EPS = 1e-5
PAGE_SIZE = 128
NEG = -1e30

LANES = 128
SUBLANES = 8
VMEM_LIMIT_BYTES = 56 * 1024 * 1024

BF16_SUBLANES = 16
HEADS_PER_TILE = LANES // HEAD_DIM
V_ROWS = HEAD_DIM + BF16_SUBLANES
LOG2E = math.log2(math.e)
POOL_HALO = POOL_STATE + 1
PAGES_PER_BLOCK = MOBA_BLOCK // PAGE_SIZE
FFN_CHUNK = 256
KMEAN_RING = 8
SCORE_CHUNK = 32

_NT = (((1,), (1,)), ((), ()))


def _params(n_grid_dims):
    return pltpu.CompilerParams(
        dimension_semantics=("arbitrary",) * n_grid_dims, vmem_limit_bytes=VMEM_LIMIT_BYTES)


def _resident(shape):
    return pl.BlockSpec(shape, lambda *_: (0,) * len(shape), pipeline_mode=pl.Buffered(1))


def _layer_norm(z, g, b):
    mu = jnp.mean(z, axis=-1, keepdims=True)
    zc = z - mu
    var = jnp.mean(zc * zc, axis=-1, keepdims=True)
    return zc * lax.rsqrt(var + LN_EPS) * g + b


def _swiglu(x, wg_ref, wu_ref, wd_ref):
    xb = x.astype(BF16)
    acc = None
    for c in range(D_FF // FFN_CHUNK):
        sl = slice(c * FFN_CHUNK, (c + 1) * FFN_CHUNK)
        g = jnp.dot(xb, wg_ref[:, sl], preferred_element_type=F32)
        u = jnp.dot(xb, wu_ref[:, sl], preferred_element_type=F32)
        h = (g * jax.nn.sigmoid(g) * u).astype(BF16)
        part = jnp.dot(h, wd_ref[sl, :], preferred_element_type=F32)
        acc = part if acc is None else acc + part
    return acc


def _ln_rows(ln_ref, first=0):
    return ln_ref[first:first + 1, :], ln_ref[first + 1:first + 2, :]


def _ffn_kernel(x_ref, wg_ref, wu_ref, wd_ref, ln_ref, y_ref, *, alpha):
    x = x_ref[...]
    y_ref[...] = _layer_norm(alpha * x + 0.5 * _swiglu(x, wg_ref, wu_ref, wd_ref), *_ln_rows(ln_ref))


def _ffn_proj_kernel(x_ref, wg_ref, wu_ref, wd_ref, ln_ref, win_ref,
                     x1_ref, q_ref, k_ref, v_ref, u_ref, *moba_refs, alpha, q_scale):
    x = x_ref[...]
    x1 = _layer_norm(alpha * x + 0.5 * _swiglu(x, wg_ref, wu_ref, wd_ref), *_ln_rows(ln_ref))
    x1_ref[...] = x1
    h = jnp.dot(x1.astype(BF16), win_ref[...], preferred_element_type=F32)
    q = h[:, :D_A] * q_scale
    k = h[:, D_A:2 * D_A]
    v = h[:, 2 * D_A:3 * D_A]
    u_ref[...] = h[:, 3 * D_A:]
    if not moba_refs:
        q_ref[...] = q.astype(BF16)
        k_ref[...] = k
        v_ref[...] = v
        return
    kb_ref, vtb_ref, km_ref = moba_refs
    vt = v.T
    q_ref[0] = q.T.astype(BF16)
    k_ref[0] = k.T
    v_ref[0] = vt
    kb_ref[...] = k.astype(BF16)
    ones = jnp.ones((BF16_SUBLANES, MOBA_BLOCK), BF16)
    for blk in range(k.shape[0] // MOBA_BLOCK):
        rows = slice(blk * MOBA_BLOCK, (blk + 1) * MOBA_BLOCK)
        for hd in range(N_HEADS_A):
            vtb_ref[0, blk, hd * V_ROWS:hd * V_ROWS + HEAD_DIM, :] = (
                vt[hd * HEAD_DIM:(hd + 1) * HEAD_DIM, rows].astype(BF16))
            vtb_ref[0, blk, hd * V_ROWS + HEAD_DIM:(hd + 1) * V_ROWS, :] = ones
        km_ref[blk] = jnp.mean(k[rows], axis=0, keepdims=True)


def _ffn_call(x, wg, wu, wd, ln, *, alpha, tm):
    n = x.shape[0]
    row = pl.BlockSpec((tm, D_MODEL), lambda i: (i, 0))
    return pl.pallas_call(
        functools.partial(_ffn_kernel, alpha=alpha),
        grid=(n // tm,),
        in_specs=[row, _resident(wg.shape), _resident(wu.shape), _resident(wd.shape), _resident(ln.shape)],
        out_specs=row,
        out_shape=jax.ShapeDtypeStruct((n, D_MODEL), F32),
        compiler_params=_params(1),
        name="ffn",
    )(x, wg, wu, wd, ln)


def _ffn_proj_call(x, wg, wu, wd, ln, win, *, alpha, q_scale, tm, moba_batch=None):
    n = x.shape[0]
    row = lambda w: pl.BlockSpec((tm, w), lambda i: (i, 0))
    sds = jax.ShapeDtypeStruct
    out_specs = [row(D_MODEL)]
    out_shape = [sds((n, D_MODEL), F32)]
    if moba_batch is None:
        out_specs += [row(D_A), row(D_A), row(D_A), row(D_POOL)]
        out_shape += [sds((n, D_A), BF16), sds((n, D_A), F32), sds((n, D_A), F32), sds((n, D_POOL), F32)]
    else:
        seq = n // moba_batch
        tpb = seq // tm
        bpt = tm // MOBA_BLOCK
        tcol = pl.BlockSpec((1, D_A, tm), lambda i: (i // tpb, 0, i % tpb))
        out_specs += [tcol, tcol, tcol, row(D_POOL), row(D_A),
                      pl.BlockSpec((1, bpt, N_HEADS_A * V_ROWS, MOBA_BLOCK), lambda i: (i // tpb, i % tpb, 0, 0)),
                      pl.BlockSpec((bpt, 1, D_A), lambda i: (i, 0, 0))]
        out_shape += [sds((moba_batch, D_A, seq), BF16),
                      sds((moba_batch, D_A, seq), F32), sds((moba_batch, D_A, seq), F32), sds((n, D_POOL), F32),
                      sds((n, D_A), BF16),
                      sds((moba_batch, seq // MOBA_BLOCK, N_HEADS_A * V_ROWS, MOBA_BLOCK), BF16),
                      sds((n // MOBA_BLOCK, 1, D_A), F32)]
    return pl.pallas_call(
        functools.partial(_ffn_proj_kernel, alpha=alpha, q_scale=q_scale),
        grid=(n // tm,),
        in_specs=[row(D_MODEL), _resident(wg.shape), _resident(wu.shape), _resident(wd.shape),
                  _resident(ln.shape), _resident(win.shape)],
        out_specs=out_specs,
        out_shape=out_shape,
        compiler_params=_params(1),
        name="ffn_proj",
    )(x, wg, wu, wd, ln, win)


def _memkv_kernel(m_ref, wk_ref, wv_ref, k_ref, v_ref, kbt_ref, vb_ref):
    mb = m_ref[...].astype(BF16)
    k = jnp.dot(mb, wk_ref[...], preferred_element_type=F32)
    v = jnp.dot(mb, wv_ref[...], preferred_element_type=F32)
    k_ref[...] = k
    v_ref[...] = v
    kbt_ref[0] = k.T.astype(BF16)
    vb_ref[...] = v.astype(BF16)


def _memkv_call(mem, wk, wv, *, n_mem):
    n = mem.shape[0]
    row = pl.BlockSpec((n_mem, D_MODEL), lambda i: (i, 0))
    sds = lambda dt: jax.ShapeDtypeStruct((n, D_MODEL), dt)
    return pl.pallas_call(
        _memkv_kernel,
        grid=(n // n_mem,),
        in_specs=[row, _resident(wk.shape), _resident(wv.shape)],
        out_specs=[row, row, pl.BlockSpec((1, D_MODEL, n_mem), lambda i: (i, 0, 0)), row],
        out_shape=[sds(F32), sds(F32), jax.ShapeDtypeStruct((n // n_mem, D_MODEL, n_mem), BF16), sds(BF16)],
        compiler_params=_params(1),
        name="memkv",
    )(mem, wk, wv)


def _top3_rows(gate, n_valid, n_rows):
    rid = lax.broadcasted_iota(jnp.int32, gate.shape, 0).astype(F32)
    g = jnp.where(rid < n_valid, gate, NEG)
    sel = jnp.zeros(gate.shape, jnp.bool_)
    for _ in range(MOBA_TOPK):
        m = jnp.max(g, axis=0, keepdims=True)
        idx = jnp.min(jnp.where(g == m, rid, float(n_rows)), axis=0, keepdims=True)
        pick = rid == idx
        sel = jnp.logical_or(sel, jnp.logical_and(pick, idx < n_valid))
        g = jnp.where(pick, -jnp.inf, g)
    return sel


def _moba_kernel(rb_ref, pt_ref, q_ref, kb_ref, vtb_ref, km_ref, bkt_ref, kpool_ref, o_ref, kmo_ref,
                 tbl_ref, rbt_ref, qh_ref, s_ref, p_ref, acc_ref, m_ref, smx_ref, al_ref,
                 ring_ref, ring_sem, kacc_ref, out_sem, *, n_blk, far_bucket, n_dec, n_pool_blk_per_dec):
    b = pl.program_id(0)
    i = pl.program_id(1)
    row_zero, row_prev = n_blk, n_blk + 1
    n_chunk = MOBA_BLOCK // SCORE_CHUNK
    chunks = [slice(c * SCORE_CHUNK, (c + 1) * SCORE_CHUNK) for c in range(n_chunk)]
    head_lanes = lambda h: slice((h // HEADS_PER_TILE) * LANES, (h // HEADS_PER_TILE + 1) * LANES)

    @pl.when(jnp.logical_and(b == 0, i == 0))
    def _build_bias_tiles():
        def per_head(h, _):
            for slot in range(2):
                bk = bkt_ref[slot]
                t = jnp.zeros(bk.shape, F32)
                for bucket in range(N_BUCKETS):
                    t = jnp.where(bk == bucket, rb_ref[bucket * N_HEADS_A + h], t)
                tbl_ref[h, slot] = jnp.where(bk < 0, NEG, t)
            return 0
        lax.fori_loop(0, N_HEADS_A, per_head, 0)

    qt = q_ref[0]
    km = km_ref[0]
    i_f = i.astype(F32)
    jp_f = jnp.maximum(i - 1, 0).astype(F32)
    row_head = lax.broadcasted_iota(jnp.int32, (LANES, MOBA_BLOCK), 0) // HEAD_DIM
    rid = lax.broadcasted_iota(jnp.int32, (n_blk, MOBA_BLOCK), 0).astype(F32)
    for h in range(N_HEADS_A):
        q2 = qt[head_lanes(h), :].astype(F32)
        qh = jnp.where(row_head == h % HEADS_PER_TILE, q2, 0.0).astype(BF16)
        qh_ref[h] = qh
        gate = jnp.dot(km[:, head_lanes(h)].astype(BF16), qh, preferred_element_type=F32)
        sel = _top3_rows(gate, i_f, n_blk)
        rbt_ref[h, :n_blk] = jnp.where(sel, rb_ref[far_bucket * N_HEADS_A + h], NEG)
        prev_sel = jnp.max(jnp.where(jnp.logical_and(sel, rid == jp_f), 1.0, 0.0), axis=0, keepdims=True)
        rbt_ref[h, row_zero:row_zero + 1] = jnp.zeros((1, MOBA_BLOCK), F32)
        rbt_ref[h, row_prev:row_prev + 1] = jnp.where(prev_sel > 0.5, 0.0, NEG)
        m_ref[h, :1] = jnp.full((1, MOBA_BLOCK), NEG, F32)
        al_ref[h, :1] = jnp.ones((1, MOBA_BLOCK), F32)
        acc_ref[h] = jnp.zeros((V_ROWS, MOBA_BLOCK), F32)
        p_ref[h] = jnp.zeros((MOBA_BLOCK, MOBA_BLOCK), BF16)

    def list_block(t):
        return jnp.clip(jnp.where(t == 0, i, jnp.where(t == 1, i - 1, t - 2)), 0, n_blk - 1)

    def stage_a(t):
        rows = pl.ds(pl.multiple_of(list_block(t) * MOBA_BLOCK, MOBA_BLOCK), MOBA_BLOCK)
        for h in range(N_HEADS_A):
            s = jnp.dot(kb_ref[0, rows, head_lanes(h)], qh_ref[h], preferred_element_type=F32)
            s_ref[h] = s
            smx_ref[h, :1] = jnp.max(s, axis=0, keepdims=True)

    def add_bias_tile(slot):
        for h in range(N_HEADS_A):
            s = s_ref[h] + tbl_ref[h, slot]
            s_ref[h] = s
            smx_ref[h, :1] = jnp.max(s, axis=0, keepdims=True)

    def stage_b(t):
        row = jnp.where(t == 0, row_zero, jnp.where(t == 1, row_prev, t - 2))
        for h in range(N_HEADS_A):
            rb = rbt_ref[h, pl.ds(row, 1), :]
            m_old = m_ref[h, :1]
            m_new = jnp.maximum(m_old, smx_ref[h, :1] + rb)
            shift = m_new - rb
            for c in chunks:
                p_ref[h, c, :] = jnp.exp2(s_ref[h, c, :] - shift).astype(BF16)
            m_ref[h, :1] = m_new
            al_ref[h, :1] = jnp.exp2(m_old - m_new)

    def stage_c(t):
        j = list_block(t)
        for h in range(N_HEADS_A):
            vt = vtb_ref[0, j, h * V_ROWS:(h + 1) * V_ROWS, :]
            acc_ref[h] = al_ref[h, :1] * acc_ref[h] + jnp.dot(vt, p_ref[h], preferred_element_type=F32)

    n_pool_blk = n_dec * n_pool_blk_per_dec
    ring = min(KMEAN_RING, n_pool_blk)
    iters_per_batch = n_blk * (n_blk + 1) // 2
    lane = lax.broadcasted_iota(jnp.int32, (D_A, LANES), 1)

    def block_copies(blk, slot):
        return [pltpu.make_async_copy(kpool_ref.at[pt_ref[blk * PAGES_PER_BLOCK + half]],
                                      ring_ref.at[slot, half], ring_sem.at[slot])
                for half in range(PAGES_PER_BLOCK)]

    def pool_block_mean(g):
        g = jnp.minimum(g, n_pool_blk - 1)
        slot = g % ring
        for c in block_copies(g, slot):
            c.wait()
        total = ring_ref[slot, 0]
        for half in range(1, PAGES_PER_BLOCK):
            total = total + ring_ref[slot, half]
        nxt = jnp.where(g + ring < n_pool_blk, g + ring, g)
        for c in block_copies(nxt, slot):
            c.start()
        col = jnp.sum(total, axis=1, keepdims=True) * (1.0 / MOBA_BLOCK)
        dec, j = g // n_pool_blk_per_dec, g % n_pool_blk_per_dec
        kacc_ref[dec] = jnp.where(lane == j, col, kacc_ref[dec])

    @pl.when(jnp.logical_and(b == 0, i == 0))
    def _start_ring():
        for slot in range(ring):
            for c in block_copies(slot, slot):
                c.start()
        kacc_ref[...] = jnp.zeros(kacc_ref.shape, F32)

    stage_a(0)
    add_bias_tile(0)
    g0 = b * iters_per_batch + i * (i + 1) // 2

    def step(t, carry):
        pool_block_mean(g0 + t - 1)
        stage_c(jnp.maximum(t - 2, 0))
        stage_b(t - 1)
        stage_a(t)

        @pl.when(t == 1)
        def _():
            add_bias_tile(1)

        return carry

    lax.fori_loop(1, i + 1, step, 0)
    pool_block_mean(g0 + i)
    stage_c(jnp.maximum(i - 1, 0))
    stage_b(i)
    stage_c(i)
    out_t = jnp.concatenate([acc_ref[h, :HEAD_DIM] / acc_ref[h, HEAD_DIM:HEAD_DIM + 1]
                             for h in range(N_HEADS_A)], axis=0)
    o_ref[0] = out_t.T.astype(BF16)

    @pl.when(jnp.logical_and(b == pl.num_programs(0) - 1, i == n_blk - 1))
    def _finish_pool_means():
        n_done = pl.num_programs(0) * iters_per_batch

        def rest(g, carry):
            pool_block_mean(g)
            return carry

        lax.fori_loop(jnp.minimum(n_done, n_pool_blk), n_pool_blk, rest, 0)
        for slot in range(ring):
            for c in block_copies(slot, slot):
                c.wait()
        out_copy = pltpu.make_async_copy(kacc_ref, kmo_ref, out_sem.at[0])
        out_copy.start()
        out_copy.wait()


def _bucket_np(dist):
    n = np.maximum(dist, 0)
    max_exact = N_BUCKETS // 2
    nf = np.maximum(n, 1).astype(np.float32)
    large = max_exact + (np.log(nf / np.float32(max_exact)) / np.float32(math.log(MAX_DISTANCE / max_exact))
                         * np.float32(N_BUCKETS - max_exact)).astype(np.int32)
    return np.where(n < max_exact, n, np.minimum(large, N_BUCKETS - 1)).astype(np.int32)


def _bucket_tiles():
    key = np.arange(MOBA_BLOCK)[:, None]
    qry = np.arange(MOBA_BLOCK)[None, :]
    own = np.where(qry >= key, _bucket_np(qry - key), -1)
    prev = _bucket_np(qry - key + MOBA_BLOCK)
    return np.stack([own, prev]).astype(np.int32)


def _far_bucket():
    far = _bucket_np(np.array([MOBA_BLOCK + 1, 1 << 30]))
    assert far[0] == far[1]
    return int(far[0])


def _moba_call(rel_bias_flat, page_table, q, kb, vtb, km, kpool_t):
    n_batch, _, seq = q.shape
    n_blk = seq // MOBA_BLOCK
    n_dec, n_pages = page_table.shape
    assert n_pages % PAGES_PER_BLOCK == 0 and n_pages // PAGES_PER_BLOCK <= LANES
    qo = pl.BlockSpec((1, MOBA_BLOCK, D_A), lambda b, i, *_: (b, i, 0))

    def per_batch(shape, **kw):
        return pl.BlockSpec((1,) + shape, lambda b, i, *_: (b,) + (0,) * len(shape), **kw)

    once = dict(pipeline_mode=pl.Buffered(1))
    any_ = pl.BlockSpec(memory_space=pl.ANY)
    tile = (MOBA_BLOCK, MOBA_BLOCK)
    stat = pltpu.VMEM((N_HEADS_A, SUBLANES, MOBA_BLOCK), F32)
    return pl.pallas_call(
        functools.partial(_moba_kernel, n_blk=n_blk, far_bucket=_far_bucket(), n_dec=n_dec,
                          n_pool_blk_per_dec=n_pages // PAGES_PER_BLOCK),
        grid_spec=pltpu.PrefetchScalarGridSpec(
            num_scalar_prefetch=2,
            grid=(n_batch, n_blk),
            in_specs=[pl.BlockSpec((1, D_A, MOBA_BLOCK), lambda b, i, *_: (b, 0, i)), per_batch((seq, D_A), **once),
                      per_batch((n_blk, N_HEADS_A * V_ROWS, MOBA_BLOCK), **once),
                      per_batch((n_blk, D_A)), _resident((2,) + tile), any_],
            out_specs=[qo, any_],
            scratch_shapes=[pltpu.VMEM((N_HEADS_A, 2) + tile, F32),
                            pltpu.VMEM((N_HEADS_A, n_blk + SUBLANES, MOBA_BLOCK), F32),
                            pltpu.VMEM((N_HEADS_A, LANES, MOBA_BLOCK), BF16),
                            pltpu.VMEM((N_HEADS_A,) + tile, F32),
                            pltpu.VMEM((N_HEADS_A,) + tile, BF16),
                            pltpu.VMEM((N_HEADS_A, V_ROWS, MOBA_BLOCK), F32),
                            stat, stat, stat,
                            pltpu.VMEM((KMEAN_RING, PAGES_PER_BLOCK, D_A, PAGE_SIZE), F32),
                            pltpu.SemaphoreType.DMA((KMEAN_RING,)),
                            pltpu.VMEM((n_dec, D_A, LANES), F32),
                            pltpu.SemaphoreType.DMA((1,))]),
        out_shape=[jax.ShapeDtypeStruct((n_batch, seq, D_A), BF16),
                   jax.ShapeDtypeStruct((n_dec, D_A, LANES), F32)],
        compiler_params=_params(2),
        name="moba_prompt",
    )(rel_bias_flat, page_table.reshape(-1), q, kb, vtb, km, jnp.asarray(_bucket_tiles()), kpool_t)


def _pool_project(window_sum, u_new, cnt, pw_ref, ps_ref):
    ys = []
    for g, w in enumerate(POOL_WINDOWS):
        lanes = slice(g * POOL_GROUP, (g + 1) * POOL_GROUP)
        d = window_sum(g, w) / cnt(w) - u_new[:, lanes]
        ys.append(jnp.dot(d.astype(BF16), pw_ref[g], preferred_element_type=F32))
    return jnp.concatenate(ys, axis=1) * ps_ref[...]


def _out_project(x1, a, p, wout_ref, g_ref, b_ref, alpha):
    proj = (jnp.dot(a, wout_ref[:D_A, :], preferred_element_type=F32)
            + jnp.dot(p.astype(BF16), wout_ref[D_A:, :], preferred_element_type=F32))
    return _layer_norm(alpha * x1 + proj, g_ref[...], b_ref[...])


def _mix_kernel(x1_ref, a_ref, u_ref, uh_ref, pw_ref, vec_ref, wout_ref, wxq_ref, wxo_ref, mkt_ref, mv_ref,
                x3_ref, ext_ref, *, alpha, tm, tpb):
    g1_ref, b1_ref = vec_ref.at[pl.ds(0, 1)], vec_ref.at[pl.ds(1, 1)]
    ps_ref = vec_ref.at[pl.ds(4, 1), pl.ds(0, D_POOL)]
    t_in_b = pl.program_id(0) % tpb
    ext_ref[:POOL_HALO, :] = jnp.where(t_in_b == 0, 0.0, uh_ref[...])
    ext_ref[POOL_HALO:, :] = u_ref[...]
    pos = t_in_b * tm + lax.broadcasted_iota(jnp.int32, (tm, 1), 0)

    def window_sum(g, w):
        lanes = slice(g * POOL_GROUP, (g + 1) * POOL_GROUP)
        s = ext_ref[POOL_HALO:POOL_HALO + tm, lanes]
        for back in range(1, w):
            s = s + ext_ref[POOL_HALO - back:POOL_HALO - back + tm, lanes]
        return s

    p = _pool_project(window_sum, u_ref[...], lambda w: jnp.minimum(w, pos + 1).astype(F32), pw_ref, ps_ref)
    x2 = _out_project(x1_ref[...], a_ref[...], p, wout_ref, g1_ref, b1_ref, alpha)

    qx = jnp.dot(x2.astype(BF16), wxq_ref[...], preferred_element_type=F32) * HEAD_DIM_X ** -0.5
    outs = []
    for h in range(N_HEADS_X):
        lanes = slice(h * HEAD_DIM_X, (h + 1) * HEAD_DIM_X)
        logits = jnp.dot(qx[:, lanes].astype(BF16), mkt_ref[0, lanes, :], preferred_element_type=F32)
        e = jnp.exp(logits - jnp.max(logits, axis=-1, keepdims=True))
        o = jnp.dot(e.astype(BF16), mv_ref[0, :, lanes], preferred_element_type=F32)
        outs.append(o / jnp.sum(e, axis=-1, keepdims=True))
    o = jnp.concatenate(outs, axis=1).astype(BF16)
    x3_ref[...] = _layer_norm(alpha * x2 + jnp.dot(o, wxo_ref[...], preferred_element_type=F32),
                              *_ln_rows(vec_ref, 2))


def _mix_call(x1, a, u, pw, vec, wout, wxq, wxo, mkt, mv, *, alpha, n_batch, tm):
    n = x1.shape[0]
    tpb = n // n_batch // tm
    n_mem = mv.shape[1]
    row = lambda w: pl.BlockSpec((tm, w), lambda i: (i, 0))
    halo = pl.BlockSpec((POOL_HALO, D_POOL), lambda i: (jnp.maximum(i * (tm // POOL_HALO) - 1, 0), 0))
    mem = pl.BlockSpec((1, n_mem, D_MODEL), lambda i: (i // tpb, 0, 0))
    memt = pl.BlockSpec((1, D_MODEL, n_mem), lambda i: (i // tpb, 0, 0))
    return pl.pallas_call(
        functools.partial(_mix_kernel, alpha=alpha, tm=tm, tpb=tpb),
        grid=(n // tm,),
        in_specs=[row(D_MODEL), row(D_A), row(D_POOL), halo, _resident(pw.shape), _resident(vec.shape),
                  _resident(wout.shape), _resident(wxq.shape), _resident(wxo.shape), memt, mem],
        out_specs=row(D_MODEL),
        out_shape=jax.ShapeDtypeStruct((n, D_MODEL), F32),
        scratch_shapes=[pltpu.VMEM((tm + POOL_HALO, D_POOL), F32)],
        compiler_params=_params(1),
        name="mix_prompt",
    )(x1, a, u, u, pw, vec, wout, wxq, wxo, mkt, mv)


def _head_rows(q_row, n_rows):
    return _mask_heads(q_row, (n_rows, D_A), HEAD_DIM)


def _mask_heads(q_row, shape, head_dim):
    row = lax.broadcasted_iota(jnp.int32, shape, 0)
    lane_head = lax.broadcasted_iota(jnp.int32, shape, 1) // head_dim
    qb = jnp.broadcast_to(q_row.astype(F32), shape)
    return jnp.where(lane_head == row, qb, 0.0).astype(q_row.dtype)


def _gate_kernel(q_ref, km_ref, o_ref, *, n_blk):
    qm = _head_rows(q_ref[0], N_HEADS_A)
    gate = jnp.dot(qm, km_ref[0].astype(BF16), preferred_element_type=F32)
    lane = lax.broadcasted_iota(jnp.int32, gate.shape, 1)
    lane_f = lane.astype(F32)
    g = jnp.where(lane < n_blk, gate, NEG)
    out = jnp.zeros(gate.shape, F32)
    for t in range(MOBA_TOPK):
        m = jnp.max(g, axis=1, keepdims=True)
        idx = jnp.min(jnp.where(g == m, lane_f, float(LANES)), axis=1, keepdims=True)
        out = jnp.where(lane == t, idx, out)
        g = jnp.where(lane_f == idx, -jnp.inf, g)
    o_ref[0] = out.astype(jnp.int32)


def _gate_call(q, kmean_t, n_blk):
    n_dec = q.shape[0]
    return pl.pallas_call(
        functools.partial(_gate_kernel, n_blk=n_blk),
        grid=(n_dec,),
        in_specs=[pl.BlockSpec((1, 1, D_A), lambda b: (b, 0, 0)),
                  pl.BlockSpec((1, D_A, LANES), lambda b: (b, 0, 0))],
        out_specs=pl.BlockSpec((1, N_HEADS_A, LANES), lambda b: (b, 0, 0)),
        out_shape=jax.ShapeDtypeStruct((n_dec, N_HEADS_A, LANES), jnp.int32),
        compiler_params=_params(1),
        name="gate_sample",
    )(q, kmean_t)


def _head_page_copy(pool_ref, page, h, buf_ref, sem_ref, par, which, slot):
    rows = pl.ds(h * HEAD_DIM, HEAD_DIM)
    return pltpu.make_async_copy(pool_ref.at[page, rows, :], buf_ref.at[par, which, slot],
                                 sem_ref.at[par, which, slot])


def _sample_attn_kernel(top_ref, pt_ref, q_ref, kn_ref, vn_ref, tnear_ref, bias0_ref, bfar_ref,
                        kpool_ref, vpool_ref, o_ref, buf_ref, sem_ref, *, n_blk):
    b = pl.program_id(0)
    n_sel = MOBA_TOPK * PAGES_PER_BLOCK
    par = b % 2

    def copies(row, half_buf, h):
        out = []
        for t in range(MOBA_TOPK):
            blk = top_ref[(row * N_HEADS_A + h) * MOBA_TOPK + t]
            for half in range(PAGES_PER_BLOCK):
                page = pt_ref[row, blk * PAGES_PER_BLOCK + half]
                slot = h * n_sel + t * PAGES_PER_BLOCK + half
                out.append(_head_page_copy(kpool_ref, page, h, buf_ref, sem_ref, half_buf, 0, slot))
                out.append(_head_page_copy(vpool_ref, page, h, buf_ref, sem_ref, half_buf, 1, slot))
        return out

    def start_row(row, half_buf):
        for h in range(N_HEADS_A):
            for c in copies(row, half_buf, h):
                c.start()

    @pl.when(b == 0)
    def _():
        start_row(0, 0)

    @pl.when(b + 1 < pl.num_programs(0))
    def _():
        start_row(b + 1, 1 - par)

    for h in range(N_HEADS_A):
        for c in copies(b, par, h):
            c.wait()

    q = q_ref[0]
    kn = kn_ref[0].astype(BF16).astype(F32)
    vn = vn_ref[0].astype(BF16).astype(F32)
    s_new_all = q.astype(F32) * kn
    outs = []
    for h in range(N_HEADS_A):
        lanes = slice(h * HEAD_DIM, (h + 1) * HEAD_DIM)
        qh = jnp.broadcast_to(q[:, lanes], (SUBLANES, HEAD_DIM))
        kt = jnp.concatenate([buf_ref[par, 0, h * n_sel + s] for s in range(n_sel)], axis=1).astype(BF16)
        vt = jnp.concatenate([buf_ref[par, 1, h * n_sel + s] for s in range(n_sel)], axis=1).astype(BF16)
        bias = []
        for t in range(MOBA_TOPK):
            blk = top_ref[(b * N_HEADS_A + h) * MOBA_TOPK + t]
            bias.append(jnp.where(blk == n_blk - 1, tnear_ref[h:h + 1, :], bfar_ref[h]))
        s = jnp.dot(qh, kt, preferred_element_type=F32)[:1] + jnp.concatenate(bias, axis=1)
        s_new = jnp.sum(s_new_all[:, lanes], axis=1, keepdims=True) + bias0_ref[h]
        m = jnp.maximum(jnp.max(s, axis=1, keepdims=True), s_new)
        p = jnp.exp(s - m)
        p_new = jnp.exp(s_new - m)
        l = jnp.sum(p, axis=1, keepdims=True) + p_new
        pb = jnp.broadcast_to(p.astype(BF16), (SUBLANES, p.shape[1]))
        o = lax.dot_general(pb, vt, _NT, preferred_element_type=F32)[:1]
        o = o + p_new.astype(BF16).astype(F32) * vn[:, lanes]
        outs.append(o / l)
    o_ref[0] = jnp.concatenate(outs, axis=1)


def _sample_attn_call(top, page_table, q, kn, vn, tnear, bias0, bfar, kpool_t, vpool_t, n_blk):
    n_dec = q.shape[0]
    row = pl.BlockSpec((1, 1, D_A), lambda b, *_: (b, 0, 0))
    smem = pl.BlockSpec(memory_space=pltpu.SMEM)
    any_ = pl.BlockSpec(memory_space=pl.ANY)
    n_slots = N_HEADS_A * MOBA_TOPK * PAGES_PER_BLOCK
    return pl.pallas_call(
        functools.partial(_sample_attn_kernel, n_blk=n_blk),
        grid_spec=pltpu.PrefetchScalarGridSpec(
            num_scalar_prefetch=2,
            grid=(n_dec,),
            in_specs=[row, row, row, pl.BlockSpec(tnear.shape, lambda b, *_: (0, 0)), smem, smem, any_, any_],
            out_specs=row,
            scratch_shapes=[pltpu.VMEM((2, 2, n_slots, HEAD_DIM, PAGE_SIZE), F32),
                            pltpu.SemaphoreType.DMA((2, 2, n_slots))]),
        out_shape=jax.ShapeDtypeStruct((n_dec, 1, D_A), F32),
        compiler_params=_params(1),
        name="attn_sample",
    )(top, page_table, q, kn, vn, tnear, bias0, bfar, kpool_t, vpool_t)


def _mix_sample_kernel(x1_ref, a_ref, u_ref, st_ref, pw_ref, ps_ref, wout_ref, g1_ref, b1_ref, wxq_ref,
                       x2_ref, qx_ref, *, alpha, pos):
    u = u_ref[...]

    def window_sum(g, w):
        lanes = slice(g * POOL_GROUP, (g + 1) * POOL_GROUP)
        s = u[:, lanes]
        for back in range(1, w):
            s = s + st_ref[POOL_STATE - back, :, lanes]
        return s

    p = _pool_project(window_sum, u, lambda w: float(min(w, pos + 1)), pw_ref, ps_ref)
    x2 = _out_project(x1_ref[...], a_ref[...].astype(BF16), p, wout_ref, g1_ref, b1_ref, alpha)
    x2_ref[...] = x2
    qx = jnp.dot(x2.astype(BF16), wxq_ref[...], preferred_element_type=F32) * HEAD_DIM_X ** -0.5
    qx_ref[...] = qx.astype(BF16)


def _mix_sample_call(x1, a, u, state_t, pw, ps, wout, g1, b1, wxq, *, alpha, pos):
    n = x1.shape[0]
    args = (x1, a, u, state_t, pw, ps, wout, g1, b1, wxq)
    return pl.pallas_call(
        functools.partial(_mix_sample_kernel, alpha=alpha, pos=pos),
        grid=(1,),
        in_specs=[_resident(v.shape) for v in args],
        out_specs=[_resident((n, D_MODEL)), _resident((n, D_MODEL))],
        out_shape=[jax.ShapeDtypeStruct((n, D_MODEL), F32), jax.ShapeDtypeStruct((n, D_MODEL), BF16)],
        compiler_params=_params(1),
        name="mix_sample",
    )(*args)


def _xattn_sample_kernel(qx_ref, mk_ref, mv_ref, o_ref):
    qm = _mask_heads(qx_ref[0], (SUBLANES, D_MODEL), HEAD_DIM_X)
    logits = lax.dot_general(qm, mk_ref[0].astype(BF16), _NT, preferred_element_type=F32)
    e = jnp.exp(logits - jnp.max(logits, axis=-1, keepdims=True))
    o = jnp.dot(e.astype(BF16), mv_ref[0].astype(BF16), preferred_element_type=F32)
    o = o / jnp.sum(e, axis=-1, keepdims=True)
    row = lax.broadcasted_iota(jnp.int32, o.shape, 0)
    lane_head = lax.broadcasted_iota(jnp.int32, o.shape, 1) // HEAD_DIM_X
    o_ref[0] = jnp.sum(jnp.where(lane_head == row, o, 0.0), axis=0, keepdims=True)


def _xattn_sample_call(qx, mk, mv):
    n_dec, n_mem, _ = mk.shape
    row = pl.BlockSpec((1, 1, D_MODEL), lambda b: (b, 0, 0))
    mem = pl.BlockSpec((1, n_mem, D_MODEL), lambda b: (b, 0, 0))
    return pl.pallas_call(
        _xattn_sample_kernel,
        grid=(n_dec,),
        in_specs=[row, mem, mem],
        out_specs=row,
        out_shape=jax.ShapeDtypeStruct((n_dec, 1, D_MODEL), F32),
        compiler_params=_params(1),
        name="xattn_sample",
    )(qx, mk, mv)


def _proj_ln_kernel(x_ref, o_ref, w_ref, g_ref, b_ref, y_ref, *, alpha):
    proj = jnp.dot(o_ref[...].astype(BF16), w_ref[...], preferred_element_type=F32)
    y_ref[...] = _layer_norm(alpha * x_ref[...] + proj, g_ref[...], b_ref[...])


def _proj_ln_call(x, o, w, g, b, *, alpha):
    args = (x, o, w, g, b)
    return pl.pallas_call(
        functools.partial(_proj_ln_kernel, alpha=alpha),
        grid=(1,),
        in_specs=[_resident(v.shape) for v in args],
        out_specs=_resident(x.shape),
        out_shape=jax.ShapeDtypeStruct(x.shape, F32),
        compiler_params=_params(1),
        name="proj_ln",
    )(*args)


def _t5_bucket(dist):
    n = jnp.maximum(dist, 0)
    max_exact = N_BUCKETS // 2
    nf = jnp.maximum(n, 1).astype(F32)
    large = max_exact + (jnp.log(nf / max_exact) / math.log(MAX_DISTANCE / max_exact)
                         * (N_BUCKETS - max_exact)).astype(jnp.int32)
    return jnp.where(n < max_exact, n, jnp.minimum(large, N_BUCKETS - 1))


def kernel(x_prompt, x_sample, cache_k, cache_v, cache_mem_k, cache_mem_v, state_pool, page_table, mem_prompt, rel_bias, ln_g, ln_b, w_ff1_gate, w_ff1_up, w_ff1_down, w_in, pool_w, pool_scale, w_out, w_xq, w_xk, w_xv, w_xo, w_ff2_gate, w_ff2_up, w_ff2_down):
    n_batch, seq, _ = x_prompt.shape
    n_dec, dec_seq, _ = x_sample.shape
    depth = ln_g.shape[0]
    n_pool = cache_k.shape[1]
    n_pages = page_table.shape[1]
    past_len = n_pages * PAGE_SIZE
    n_mem = mem_prompt.shape[1]
    assert depth == 1 and dec_seq == 1
    assert seq % MOBA_BLOCK == 0 and past_len % MOBA_BLOCK == 0
    assert past_len // MOBA_BLOCK >= MOBA_TOPK
    alpha = (2 * depth) ** 0.25
    tm = min(512, seq)
    n_blk = seq // MOBA_BLOCK

    bf = lambda w: w[0].astype(BF16)
    ff1 = (bf(w_ff1_gate), bf(w_ff1_up), bf(w_ff1_down))
    ff2 = (bf(w_ff2_gate), bf(w_ff2_up), bf(w_ff2_down))
    win = bf(w_in)
    ln = lambda i: (ln_g[0, i:i + 1], ln_b[0, i:i + 1])
    (g0, b0), (g1, b1), (g2, b2), (g3, b3) = ln(0), ln(1), ln(2), ln(3)
    pw, ps = pool_w[0].astype(BF16), pool_scale

    def packed(*rows):
        rows = [jnp.pad(r, ((0, 0), (0, D_MODEL - r.shape[1]))) for r in rows]
        return jnp.concatenate(rows + [jnp.zeros((SUBLANES - len(rows), D_MODEL), F32)], axis=0)

    ln0, ln3, mix_vec = packed(g0, b0), packed(g3, b3), packed(g1, b1, g2, b2, ps)
    wout, wxq, wxo = bf(w_out), bf(w_xq), bf(w_xo)

    bias_by_dist = rel_bias[_t5_bucket(jnp.arange(2 * MOBA_BLOCK, dtype=jnp.int32))].T
    bfar = rel_bias[_far_bucket()]

    xp = x_prompt.reshape(n_batch * seq, D_MODEL)
    attn_scale = HEAD_DIM ** -0.5
    x1, q, kt, vt, u, kb, vtb, km = _ffn_proj_call(xp, *ff1, ln0, win, alpha=alpha, q_scale=attn_scale * LOG2E,
                                                   tm=tm, moba_batch=n_batch)
    kpool_t = cache_k[0].transpose(0, 2, 3, 1).reshape(n_pool, D_A, PAGE_SIZE)
    vpool_t = cache_v[0].transpose(0, 2, 3, 1).reshape(n_pool, D_A, PAGE_SIZE)
    a, kmean_t = _moba_call(rel_bias.reshape(-1) * LOG2E, page_table, q, kb.reshape(n_batch, seq, D_A), vtb,
                            km.reshape(n_batch, n_blk, D_A), kpool_t)
    mk, mv, mkbt, mvb = _memkv_call(mem_prompt.reshape(n_batch * n_mem, D_MODEL), bf(w_xk), bf(w_xv), n_mem=n_mem)
    x3 = _mix_call(x1, a.reshape(n_batch * seq, D_A), u, pw, mix_vec, wout, wxq, wxo,
                   mkbt, mvb.reshape(n_batch, n_mem, D_MODEL), alpha=alpha, n_batch=n_batch, tm=tm)
    y_prompt = _ffn_call(x3, *ff2, ln3, alpha=alpha, tm=tm).reshape(n_batch, seq, D_MODEL)

    to_heads = lambda t: t.reshape(t.shape[0], N_HEADS_A, HEAD_DIM, t.shape[2]).transpose(0, 3, 1, 2)[None]
    k_prompt, v_prompt = to_heads(kt), to_heads(vt)
    pool_prompt = u.reshape(n_batch, seq, D_POOL)[None, :, seq - POOL_STATE:]
    memk_prompt = mk.reshape(1, n_batch, n_mem, N_HEADS_X, HEAD_DIM_X)
    memv_prompt = mv.reshape(1, n_batch, n_mem, N_HEADS_X, HEAD_DIM_X)

    xs = x_sample.reshape(n_dec, D_MODEL)
    x1s, qs, kn, vn, us = _ffn_proj_call(xs, *ff1, ln0, win, alpha=alpha, q_scale=attn_scale, tm=n_dec)
    n_blk_s = past_len // MOBA_BLOCK
    qs3 = qs.reshape(n_dec, 1, D_A)
    top = _gate_call(qs3, kmean_t, n_blk_s)[:, :, :MOBA_TOPK].reshape(-1)
    tnear = bias_by_dist[:, MOBA_BLOCK - jnp.arange(MOBA_BLOCK)]
    a_s = _sample_attn_call(top, page_table, qs3, kn.reshape(n_dec, 1, D_A), vn.reshape(n_dec, 1, D_A),
                            tnear, bias_by_dist[:, 0], bfar, kpool_t, vpool_t, n_blk_s)
    state_t = state_pool[0].transpose(1, 0, 2)
    x2s, qxs = _mix_sample_call(x1s, a_s.reshape(n_dec, D_A), us, state_t, pw, ps, wout, g1, b1, wxq,
                                alpha=alpha, pos=past_len)
    o_s = _xattn_sample_call(qxs.reshape(n_dec, 1, D_MODEL),
                             cache_mem_k[0].reshape(n_dec, n_mem, D_MODEL),
                             cache_mem_v[0].reshape(n_dec, n_mem, D_MODEL))
    x3s = _proj_ln_call(x2s, o_s.reshape(n_dec, D_MODEL), wxo, g2, b2, alpha=alpha)
    y_sample = _ffn_call(x3s, *ff2, ln3, alpha=alpha, tm=n_dec).reshape(n_dec, 1, D_MODEL)

    k_sample = kn.reshape(1, n_dec, 1, N_HEADS_A, HEAD_DIM)
    v_sample = vn.reshape(1, n_dec, 1, N_HEADS_A, HEAD_DIM)
    pool_sample = jnp.concatenate([state_t[1:], us[None]], axis=0).transpose(1, 0, 2)[None]
    return (y_prompt, y_sample, k_prompt, v_prompt, pool_prompt, memk_prompt, memv_prompt,
            k_sample, v_sample, pool_sample)
```

```python
import functools
import math

import jax
import jax.numpy as jnp
import numpy as np
from jax import lax
from jax.experimental import pallas as pl
from jax.experimental.pallas import tpu as pltpu

F32 = jnp.float32
BF16 = jnp.bfloat16

D_MODEL = 1024
HEAD_DIM = 64
N_HEADS_A = 8
D_A = N_HEADS_A * HEAD_DIM
MOBA_BLOCK = 256
MOBA_TOPK = 3
D_POOL = D_MODEL - D_A
POOL_WINDOWS = (2, 4, 8, 16)
POOL_GROUP = D_POOL // len(POOL_WINDOWS)
POOL_STATE = max(POOL_WINDOWS) - 1
N_BUCKETS = 32
MAX_DISTANCE = 128
N_HEADS_X = 4
HEAD_DIM_X = D_MODEL // N_HEADS_X
D_FF = 2816
LN_EPS = 1e-5
PAGE_SIZE = 128
NEG = -1e30

LANES = 128
SUBLANES = 8
VMEM_LIMIT_BYTES = 56 * 1024 * 1024

BF16_SUBLANES = 16
HEADS_PER_TILE = LANES // HEAD_DIM
V_ROWS = HEAD_DIM + BF16_SUBLANES
LOG2E = math.log2(math.e)
POOL_HALO = POOL_STATE + 1
PAGES_PER_BLOCK = MOBA_BLOCK // PAGE_SIZE
FFN_CHUNK = 256
KMEAN_RING = 8
SCORE_CHUNK = 32

_NT = (((1,), (1,)), ((), ()))


def _params(n_grid_dims):
    return pltpu.CompilerParams(
        dimension_semantics=("arbitrary",) * n_grid_dims, vmem_limit_bytes=VMEM_LIMIT_BYTES)


def _resident(shape):
    return pl.BlockSpec(shape, lambda *_: (0,) * len(shape), pipeline_mode=pl.Buffered(1))


def _layer_norm(z, g, b):
    mu = jnp.mean(z, axis=-1, keepdims=True)
    zc = z - mu
    var = jnp.mean(zc * zc, axis=-1, keepdims=True)
    return zc * lax.rsqrt(var + LN_EPS) * g + b


def _swiglu(x, wg_ref, wu_ref, wd_ref):
    xb = x.astype(BF16)
    acc = None
    for c in range(D_FF // FFN_CHUNK):
        sl = slice(c * FFN_CHUNK, (c + 1) * FFN_CHUNK)
        g = jnp.dot(xb, wg_ref[:, sl], preferred_element_type=F32)
        u = jnp.dot(xb, wu_ref[:, sl], preferred_element_type=F32)
        h = (g * jax.nn.sigmoid(g) * u).astype(BF16)
        part = jnp.dot(h, wd_ref[sl, :], preferred_element_type=F32)
        acc = part if acc is None else acc + part
    return acc


def _ln_rows(ln_ref, first=0):
    return ln_ref[first:first + 1, :], ln_ref[first + 1:first + 2, :]


def _ffn_kernel(x_ref, wg_ref, wu_ref, wd_ref, ln_ref, y_ref, *, alpha):
    x = x_ref[...]
    y_ref[...] = _layer_norm(alpha * x + 0.5 * _swiglu(x, wg_ref, wu_ref, wd_ref), *_ln_rows(ln_ref))


def _ffn_proj_kernel(x_ref, wg_ref, wu_ref, wd_ref, ln_ref, win_ref,
                     x1_ref, q_ref, k_ref, v_ref, u_ref, *moba_refs, alpha, q_scale):
    x = x_ref[...]
    x1 = _layer_norm(alpha * x + 0.5 * _swiglu(x, wg_ref, wu_ref, wd_ref), *_ln_rows(ln_ref))
    x1_ref[...] = x1
    h = jnp.dot(x1.astype(BF16), win_ref[...], preferred_element_type=F32)
    q = h[:, :D_A] * q_scale
    k = h[:, D_A:2 * D_A]
    v = h[:, 2 * D_A:3 * D_A]
    u_ref[...] = h[:, 3 * D_A:]
    if not moba_refs:
        q_ref[...] = q.astype(BF16)
        k_ref[...] = k
        v_ref[...] = v
        return
    kb_ref, vtb_ref, km_ref = moba_refs
    vt = v.T
    q_ref[0] = q.T.astype(BF16)
    k_ref[0] = k.T
    v_ref[0] = vt
    kb_ref[...] = k.astype(BF16)
    ones = jnp.ones((BF16_SUBLANES, MOBA_BLOCK), BF16)
    for blk in range(k.shape[0] // MOBA_BLOCK):
        rows = slice(blk * MOBA_BLOCK, (blk + 1) * MOBA_BLOCK)
        for hd in range(N_HEADS_A):
            vtb_ref[0, blk, hd * V_ROWS:hd * V_ROWS + HEAD_DIM, :] = (
                vt[hd * HEAD_DIM:(hd + 1) * HEAD_DIM, rows].astype(BF16))
            vtb_ref[0, blk, hd * V_ROWS + HEAD_DIM:(hd + 1) * V_ROWS, :] = ones
        km_ref[blk] = jnp.mean(k[rows], axis=0, keepdims=True)


def _ffn_call(x, wg, wu, wd, ln, *, alpha, tm):
    n = x.shape[0]
    row = pl.BlockSpec((tm, D_MODEL), lambda i: (i, 0))
    return pl.pallas_call(
        functools.partial(_ffn_kernel, alpha=alpha),
        grid=(n // tm,),
        in_specs=[row, _resident(wg.shape), _resident(wu.shape), _resident(wd.shape), _resident(ln.shape)],
        out_specs=row,
        out_shape=jax.ShapeDtypeStruct((n, D_MODEL), F32),
        compiler_params=_params(1),
        name="ffn",
    )(x, wg, wu, wd, ln)


def _ffn_proj_call(x, wg, wu, wd, ln, win, *, alpha, q_scale, tm, moba_batch=None):
    n = x.shape[0]
    row = lambda w: pl.BlockSpec((tm, w), lambda i: (i, 0))
    sds = jax.ShapeDtypeStruct
    out_specs = [row(D_MODEL)]
    out_shape = [sds((n, D_MODEL), F32)]
    if moba_batch is None:
        out_specs += [row(D_A), row(D_A), row(D_A), row(D_POOL)]
        out_shape += [sds((n, D_A), BF16), sds((n, D_A), F32), sds((n, D_A), F32), sds((n, D_POOL), F32)]
    else:
        seq = n // moba_batch
        tpb = seq // tm
        bpt = tm // MOBA_BLOCK
        tcol = pl.BlockSpec((1, D_A, tm), lambda i: (i // tpb, 0, i % tpb))
        out_specs += [tcol, tcol, tcol, row(D_POOL), row(D_A),
                      pl.BlockSpec((1, bpt, N_HEADS_A * V_ROWS, MOBA_BLOCK), lambda i: (i // tpb, i % tpb, 0, 0)),
                      pl.BlockSpec((bpt, 1, D_A), lambda i: (i, 0, 0))]
        out_shape += [sds((moba_batch, D_A, seq), BF16),
                      sds((moba_batch, D_A, seq), F32), sds((moba_batch, D_A, seq), F32), sds((n, D_POOL), F32),
                      sds((n, D_A), BF16),
                      sds((moba_batch, seq // MOBA_BLOCK, N_HEADS_A * V_ROWS, MOBA_BLOCK), BF16),
                      sds((n // MOBA_BLOCK, 1, D_A), F32)]
    return pl.pallas_call(
        functools.partial(_ffn_proj_kernel, alpha=alpha, q_scale=q_scale),
        grid=(n // tm,),
        in_specs=[row(D_MODEL), _resident(wg.shape), _resident(wu.shape), _resident(wd.shape),
                  _resident(ln.shape), _resident(win.shape)],
        out_specs=out_specs,
        out_shape=out_shape,
        compiler_params=_params(1),
        name="ffn_proj",
    )(x, wg, wu, wd, ln, win)


def _memkv_kernel(m_ref, wk_ref, wv_ref, k_ref, v_ref, kbt_ref, vb_ref):
    mb = m_ref[...].astype(BF16)
    k = jnp.dot(mb, wk_ref[...], preferred_element_type=F32)
    v = jnp.dot(mb, wv_ref[...], preferred_element_type=F32)
    k_ref[...] = k
    v_ref[...] = v
    kbt_ref[0] = k.T.astype(BF16)
    vb_ref[...] = v.astype(BF16)


def _memkv_call(mem, wk, wv, *, n_mem):
    n = mem.shape[0]
    row = pl.BlockSpec((n_mem, D_MODEL), lambda i: (i, 0))
    sds = lambda dt: jax.ShapeDtypeStruct((n, D_MODEL), dt)
    return pl.pallas_call(
        _memkv_kernel,
        grid=(n // n_mem,),
        in_specs=[row, _resident(wk.shape), _resident(wv.shape)],
        out_specs=[row, row, pl.BlockSpec((1, D_MODEL, n_mem), lambda i: (i, 0, 0)), row],
        out_shape=[sds(F32), sds(F32), jax.ShapeDtypeStruct((n // n_mem, D_MODEL, n_mem), BF16), sds(BF16)],
        compiler_params=_params(1),
        name="memkv",
    )(mem, wk, wv)


def _top3_rows(gate, n_valid, n_rows):
    rid = lax.broadcasted_iota(jnp.int32, gate.shape, 0).astype(F32)
    g = jnp.where(rid < n_valid, gate, NEG)
    sel = jnp.zeros(gate.shape, jnp.bool_)
    for _ in range(MOBA_TOPK):
        m = jnp.max(g, axis=0, keepdims=True)
        idx = jnp.min(jnp.where(g == m, rid, float(n_rows)), axis=0, keepdims=True)
        pick = rid == idx
        sel = jnp.logical_or(sel, jnp.logical_and(pick, idx < n_valid))
        g = jnp.where(pick, -jnp.inf, g)
    return sel


def _moba_kernel(rb_ref, pt_ref, q_ref, kb_ref, vtb_ref, km_ref, bkt_ref, kpool_ref, o_ref, kmo_ref,
                 tbl_ref, rbt_ref, qh_ref, s_ref, p_ref, acc_ref, m_ref, smx_ref, al_ref,
                 ring_ref, ring_sem, kacc_ref, out_sem, *, n_blk, far_bucket, n_dec, n_pool_blk_per_dec):
    b = pl.program_id(0)
    i = pl.program_id(1)
    row_zero, row_prev = n_blk, n_blk + 1
    n_chunk = MOBA_BLOCK // SCORE_CHUNK
    chunks = [slice(c * SCORE_CHUNK, (c + 1) * SCORE_CHUNK) for c in range(n_chunk)]
    head_lanes = lambda h: slice((h // HEADS_PER_TILE) * LANES, (h // HEADS_PER_TILE + 1) * LANES)

    @pl.when(jnp.logical_and(b == 0, i == 0))
    def _build_bias_tiles():
        def per_head(h, _):
            for slot in range(2):
                bk = bkt_ref[slot]
                t = jnp.zeros(bk.shape, F32)
                for bucket in range(N_BUCKETS):
                    t = jnp.where(bk == bucket, rb_ref[bucket * N_HEADS_A + h], t)
                tbl_ref[h, slot] = jnp.where(bk < 0, NEG, t)
            return 0
        lax.fori_loop(0, N_HEADS_A, per_head, 0)

    qt = q_ref[0]
    km = km_ref[0]
    i_f = i.astype(F32)
    jp_f = jnp.maximum(i - 1, 0).astype(F32)
    row_head = lax.broadcasted_iota(jnp.int32, (LANES, MOBA_BLOCK), 0) // HEAD_DIM
    rid = lax.broadcasted_iota(jnp.int32, (n_blk, MOBA_BLOCK), 0).astype(F32)
    for h in range(N_HEADS_A):
        q2 = qt[head_lanes(h), :].astype(F32)
        qh = jnp.where(row_head == h % HEADS_PER_TILE, q2, 0.0).astype(BF16)
        qh_ref[h] = qh
        gate = jnp.dot(km[:, head_lanes(h)].astype(BF16), qh, preferred_element_type=F32)
        sel = _top3_rows(gate, i_f, n_blk)
        rbt_ref[h, :n_blk] = jnp.where(sel, rb_ref[far_bucket * N_HEADS_A + h], NEG)
        prev_sel = jnp.max(jnp.where(jnp.logical_and(sel, rid == jp_f), 1.0, 0.0), axis=0, keepdims=True)
        rbt_ref[h, row_zero:row_zero + 1] = jnp.zeros((1, MOBA_BLOCK), F32)
        rbt_ref[h, row_prev:row_prev + 1] = jnp.where(prev_sel > 0.5, 0.0, NEG)
        m_ref[h, :1] = jnp.full((1, MOBA_BLOCK), NEG, F32)
        al_ref[h, :1] = jnp.ones((1, MOBA_BLOCK), F32)
        acc_ref[h] = jnp.zeros((V_ROWS, MOBA_BLOCK), F32)
        p_ref[h] = jnp.zeros((MOBA_BLOCK, MOBA_BLOCK), BF16)

    def list_block(t):
        return jnp.clip(jnp.where(t == 0, i, jnp.where(t == 1, i - 1, t - 2)), 0, n_blk - 1)

    def stage_a(t):
        rows = pl.ds(pl.multiple_of(list_block(t) * MOBA_BLOCK, MOBA_BLOCK), MOBA_BLOCK)
        for h in range(N_HEADS_A):
            s = jnp.dot(kb_ref[0, rows, head_lanes(h)], qh_ref[h], preferred_element_type=F32)
            s_ref[h] = s
            smx_ref[h, :1] = jnp.max(s, axis=0, keepdims=True)

    def add_bias_tile(slot):
        for h in range(N_HEADS_A):
            s = s_ref[h] + tbl_ref[h, slot]
            s_ref[h] = s
            smx_ref[h, :1] = jnp.max(s, axis=0, keepdims=True)

    def stage_b(t):
        row = jnp.where(t == 0, row_zero, jnp.where(t == 1, row_prev, t - 2))
        for h in range(N_HEADS_A):
            rb = rbt_ref[h, pl.ds(row, 1), :]
            m_old = m_ref[h, :1]
            m_new = jnp.maximum(m_old, smx_ref[h, :1] + rb)
            shift = m_new - rb
            for c in chunks:
                p_ref[h, c, :] = jnp.exp2(s_ref[h, c, :] - shift).astype(BF16)
            m_ref[h, :1] = m_new
            al_ref[h, :1] = jnp.exp2(m_old - m_new)

    def stage_c(t):
        j = list_block(t)
        for h in range(N_HEADS_A):
            vt = vtb_ref[0, j, h * V_ROWS:(h + 1) * V_ROWS, :]
            acc_ref[h] = al_ref[h, :1] * acc_ref[h] + jnp.dot(vt, p_ref[h], preferred_element_type=F32)

    n_pool_blk = n_dec * n_pool_blk_per_dec
    ring = min(KMEAN_RING, n_pool_blk)
    iters_per_batch = n_blk * (n_blk + 1) // 2
    lane = lax.broadcasted_iota(jnp.int32, (D_A, LANES), 1)

    def block_copies(blk, slot):
        return [pltpu.make_async_copy(kpool_ref.at[pt_ref[blk * PAGES_PER_BLOCK + half]],
                                      ring_ref.at[slot, half], ring_sem.at[slot])
                for half in range(PAGES_PER_BLOCK)]

    def pool_block_mean(g):
        g = jnp.minimum(g, n_pool_blk - 1)
        slot = g % ring
        for c in block_copies(g, slot):
            c.wait()
        total = ring_ref[slot, 0]
        for half in range(1, PAGES_PER_BLOCK):
            total = total + ring_ref[slot, half]
        nxt = jnp.where(g + ring < n_pool_blk, g + ring, g)
        for c in block_copies(nxt, slot):
            c.start()
        col = jnp.sum(total, axis=1, keepdims=True) * (1.0 / MOBA_BLOCK)
        dec, j = g // n_pool_blk_per_dec, g % n_pool_blk_per_dec
        kacc_ref[dec] = jnp.where(lane == j, col, kacc_ref[dec])

    @pl.when(jnp.logical_and(b == 0, i == 0))
    def _start_ring():
        for slot in range(ring):
            for c in block_copies(slot, slot):
                c.start()
        kacc_ref[...] = jnp.zeros(kacc_ref.shape, F32)

    stage_a(0)
    add_bias_tile(0)
    g0 = b * iters_per_batch + i * (i + 1) // 2

    @pl.when(i >= 1)
    def _first_iteration():
        pool_block_mean(g0)
        stage_b(0)
        stage_a(1)
        add_bias_tile(1)

    def step(t, carry):
        pool_block_mean(g0 + t - 1)
        stage_c(t - 2)
        stage_b(t - 1)
        stage_a(t)
        return carry

    lax.fori_loop(2, i + 1, step, 0)
    pool_block_mean(g0 + i)
    stage_c(jnp.maximum(i - 1, 0))
    stage_b(i)
    stage_c(i)
    out_t = jnp.concatenate([acc_ref[h, :HEAD_DIM] / acc_ref[h, HEAD_DIM:HEAD_DIM + 1]
                             for h in range(N_HEADS_A)], axis=0)
    o_ref[0] = out_t.T.astype(BF16)

    @pl.when(jnp.logical_and(b == pl.num_programs(0) - 1, i == n_blk - 1))
    def _finish_pool_means():
        n_done = pl.num_programs(0) * iters_per_batch

        def rest(g, carry):
            pool_block_mean(g)
            return carry

        lax.fori_loop(jnp.minimum(n_done, n_pool_blk), n_pool_blk, rest, 0)
        for slot in range(ring):
            for c in block_copies(slot, slot):
                c.wait()
        out_copy = pltpu.make_async_copy(kacc_ref, kmo_ref, out_sem.at[0])
        out_copy.start()
        out_copy.wait()


def _bucket_np(dist):
    n = np.maximum(dist, 0)
    max_exact = N_BUCKETS // 2
    nf = np.maximum(n, 1).astype(np.float32)
    large = max_exact + (np.log(nf / np.float32(max_exact)) / np.float32(math.log(MAX_DISTANCE / max_exact))
                         * np.float32(N_BUCKETS - max_exact)).astype(np.int32)
    return np.where(n < max_exact, n, np.minimum(large, N_BUCKETS - 1)).astype(np.int32)


def _bucket_tiles():
    key = np.arange(MOBA_BLOCK)[:, None]
    qry = np.arange(MOBA_BLOCK)[None, :]
    own = np.where(qry >= key, _bucket_np(qry - key), -1)
    prev = _bucket_np(qry - key + MOBA_BLOCK)
    return np.stack([own, prev]).astype(np.int32)


def _far_bucket():
    far = _bucket_np(np.array([MOBA_BLOCK + 1, 1 << 30]))
    assert far[0] == far[1]
    return int(far[0])


def _moba_call(rel_bias_flat, page_table, q, kb, vtb, km, kpool_t):
    n_batch, _, seq = q.shape
    n_blk = seq // MOBA_BLOCK
    n_dec, n_pages = page_table.shape
    assert n_pages % PAGES_PER_BLOCK == 0 and n_pages // PAGES_PER_BLOCK <= LANES
    qo = pl.BlockSpec((1, MOBA_BLOCK, D_A), lambda b, i, *_: (b, i, 0))

    def per_batch(shape, **kw):
        return pl.BlockSpec((1,) + shape, lambda b, i, *_: (b,) + (0,) * len(shape), **kw)

    once = dict(pipeline_mode=pl.Buffered(1))
    any_ = pl.BlockSpec(memory_space=pl.ANY)
    tile = (MOBA_BLOCK, MOBA_BLOCK)
    stat = pltpu.VMEM((N_HEADS_A, SUBLANES, MOBA_BLOCK), F32)
    return pl.pallas_call(
        functools.partial(_moba_kernel, n_blk=n_blk, far_bucket=_far_bucket(), n_dec=n_dec,
                          n_pool_blk_per_dec=n_pages // PAGES_PER_BLOCK),
        grid_spec=pltpu.PrefetchScalarGridSpec(
            num_scalar_prefetch=2,
            grid=(n_batch, n_blk),
            in_specs=[pl.BlockSpec((1, D_A, MOBA_BLOCK), lambda b, i, *_: (b, 0, i)), per_batch((seq, D_A), **once),
                      per_batch((n_blk, N_HEADS_A * V_ROWS, MOBA_BLOCK), **once),
                      per_batch((n_blk, D_A)), _resident((2,) + tile), any_],
            out_specs=[qo, any_],
            scratch_shapes=[pltpu.VMEM((N_HEADS_A, 2) + tile, F32),
                            pltpu.VMEM((N_HEADS_A, n_blk + SUBLANES, MOBA_BLOCK), F32),
                            pltpu.VMEM((N_HEADS_A, LANES, MOBA_BLOCK), BF16),
                            pltpu.VMEM((N_HEADS_A,) + tile, F32),
                            pltpu.VMEM((N_HEADS_A,) + tile, BF16),
                            pltpu.VMEM((N_HEADS_A, V_ROWS, MOBA_BLOCK), F32),
                            stat, stat, stat,
                            pltpu.VMEM((KMEAN_RING, PAGES_PER_BLOCK, D_A, PAGE_SIZE), F32),
                            pltpu.SemaphoreType.DMA((KMEAN_RING,)),
                            pltpu.VMEM((n_dec, D_A, LANES), F32),
                            pltpu.SemaphoreType.DMA((1,))]),
        out_shape=[jax.ShapeDtypeStruct((n_batch, seq, D_A), BF16),
                   jax.ShapeDtypeStruct((n_dec, D_A, LANES), F32)],
        compiler_params=_params(2),
        name="moba_prompt",
    )(rel_bias_flat, page_table.reshape(-1), q, kb, vtb, km, jnp.asarray(_bucket_tiles()), kpool_t)


def _pool_project(window_sum, u_new, cnt, pw_ref, ps_ref):
    ys = []
    for g, w in enumerate(POOL_WINDOWS):
        lanes = slice(g * POOL_GROUP, (g + 1) * POOL_GROUP)
        d = window_sum(g, w) / cnt(w) - u_new[:, lanes]
        ys.append(jnp.dot(d.astype(BF16), pw_ref[g], preferred_element_type=F32))
    return jnp.concatenate(ys, axis=1) * ps_ref[...]


def _out_project(x1, a, p, wout_ref, g_ref, b_ref, alpha):
    proj = (jnp.dot(a, wout_ref[:D_A, :], preferred_element_type=F32)
            + jnp.dot(p.astype(BF16), wout_ref[D_A:, :], preferred_element_type=F32))
    return _layer_norm(alpha * x1 + proj, g_ref[...], b_ref[...])


def _mix_kernel(x1_ref, a_ref, u_ref, uh_ref, pw_ref, vec_ref, wout_ref, wxq_ref, wxo_ref, mkt_ref, mv_ref,
                x3_ref, ext_ref, *, alpha, tm, tpb):
    g1_ref, b1_ref = vec_ref.at[pl.ds(0, 1)], vec_ref.at[pl.ds(1, 1)]
    ps_ref = vec_ref.at[pl.ds(4, 1), pl.ds(0, D_POOL)]
    t_in_b = pl.program_id(0) % tpb
    ext_ref[:POOL_HALO, :] = jnp.where(t_in_b == 0, 0.0, uh_ref[...])
    ext_ref[POOL_HALO:, :] = u_ref[...]
    pos = t_in_b * tm + lax.broadcasted_iota(jnp.int32, (tm, 1), 0)

    def window_sum(g, w):
        lanes = slice(g * POOL_GROUP, (g + 1) * POOL_GROUP)
        s = ext_ref[POOL_HALO:POOL_HALO + tm, lanes]
        for back in range(1, w):
            s = s + ext_ref[POOL_HALO - back:POOL_HALO - back + tm, lanes]
        return s

    p = _pool_project(window_sum, u_ref[...], lambda w: jnp.minimum(w, pos + 1).astype(F32), pw_ref, ps_ref)
    x2 = _out_project(x1_ref[...], a_ref[...], p, wout_ref, g1_ref, b1_ref, alpha)

    qx = jnp.dot(x2.astype(BF16), wxq_ref[...], preferred_element_type=F32) * HEAD_DIM_X ** -0.5
    outs = []
    for h in range(N_HEADS_X):
        lanes = slice(h * HEAD_DIM_X, (h + 1) * HEAD_DIM_X)
        logits = jnp.dot(qx[:, lanes].astype(BF16), mkt_ref[0, lanes, :], preferred_element_type=F32)
        e = jnp.exp(logits - jnp.max(logits, axis=-1, keepdims=True))
        o = jnp.dot(e.astype(BF16), mv_ref[0, :, lanes], preferred_element_type=F32)
        outs.append(o / jnp.sum(e, axis=-1, keepdims=True))
    o = jnp.concatenate(outs, axis=1).astype(BF16)
    x3_ref[...] = _layer_norm(alpha * x2 + jnp.dot(o, wxo_ref[...], preferred_element_type=F32),
                              *_ln_rows(vec_ref, 2))


def _mix_call(x1, a, u, pw, vec, wout, wxq, wxo, mkt, mv, *, alpha, n_batch, tm):
    n = x1.shape[0]
    tpb = n // n_batch // tm
    n_mem = mv.shape[1]
    row = lambda w: pl.BlockSpec((tm, w), lambda i: (i, 0))
    halo = pl.BlockSpec((POOL_HALO, D_POOL), lambda i: (jnp.maximum(i * (tm // POOL_HALO) - 1, 0), 0))
    mem = pl.BlockSpec((1, n_mem, D_MODEL), lambda i: (i // tpb, 0, 0))
    memt = pl.BlockSpec((1, D_MODEL, n_mem), lambda i: (i // tpb, 0, 0))
    return pl.pallas_call(
        functools.partial(_mix_kernel, alpha=alpha, tm=tm, tpb=tpb),
        grid=(n // tm,),
        in_specs=[row(D_MODEL), row(D_A), row(D_POOL), halo, _resident(pw.shape), _resident(vec.shape),
                  _resident(wout.shape), _resident(wxq.shape), _resident(wxo.shape), memt, mem],
        out_specs=row(D_MODEL),
        out_shape=jax.ShapeDtypeStruct((n, D_MODEL), F32),
        scratch_shapes=[pltpu.VMEM((tm + POOL_HALO, D_POOL), F32)],
        compiler_params=_params(1),
        name="mix_prompt",
    )(x1, a, u, u, pw, vec, wout, wxq, wxo, mkt, mv)


def _head_rows(q_row, n_rows):
    return _mask_heads(q_row, (n_rows, D_A), HEAD_DIM)


def _mask_heads(q_row, shape, head_dim):
    row = lax.broadcasted_iota(jnp.int32, shape, 0)
    lane_head = lax.broadcasted_iota(jnp.int32, shape, 1) // head_dim
    qb = jnp.broadcast_to(q_row.astype(F32), shape)
    return jnp.where(lane_head == row, qb, 0.0).astype(q_row.dtype)


def _gate_kernel(q_ref, km_ref, o_ref, *, n_blk):
    for r in range(q_ref.shape[0]):
        qm = _head_rows(q_ref[r], N_HEADS_A)
        gate = jnp.dot(qm, km_ref[r].astype(BF16), preferred_element_type=F32)
        lane = lax.broadcasted_iota(jnp.int32, gate.shape, 1)
        lane_f = lane.astype(F32)
        g = jnp.where(lane < n_blk, gate, NEG)
        out = jnp.zeros(gate.shape, F32)
        for t in range(MOBA_TOPK):
            m = jnp.max(g, axis=1, keepdims=True)
            idx = jnp.min(jnp.where(g == m, lane_f, float(LANES)), axis=1, keepdims=True)
            out = jnp.where(lane == t, idx, out)
            g = jnp.where(lane_f == idx, -jnp.inf, g)
        o_ref[r] = out.astype(jnp.int32)


def _gate_call(q, kmean_t, n_blk):
    n_dec = q.shape[0]
    rows = math.gcd(n_dec, SUBLANES)
    return pl.pallas_call(
        functools.partial(_gate_kernel, n_blk=n_blk),
        grid=(n_dec // rows,),
        in_specs=[pl.BlockSpec((rows, 1, D_A), lambda b: (b, 0, 0)),
                  pl.BlockSpec((rows, D_A, LANES), lambda b: (b, 0, 0))],
        out_specs=pl.BlockSpec((rows, N_HEADS_A, LANES), lambda b: (b, 0, 0)),
        out_shape=jax.ShapeDtypeStruct((n_dec, N_HEADS_A, LANES), jnp.int32),
        compiler_params=_params(1),
        name="gate_sample",
    )(q, kmean_t)


def _head_page_copy(pool_ref, page, h, buf_ref, sem_ref, par, which, slot):
    rows = pl.ds(h * HEAD_DIM, HEAD_DIM)
    return pltpu.make_async_copy(pool_ref.at[page, rows, :], buf_ref.at[par, which, slot],
                                 sem_ref.at[par, which, slot])


def _sample_attn_kernel(top_ref, pt_ref, q_ref, kn_ref, vn_ref, tnear_ref, bias0_ref, bfar_ref,
                        kpool_ref, vpool_ref, o_ref, buf_ref, sem_ref, *, n_blk):
    b = pl.program_id(0)
    n_sel = MOBA_TOPK * PAGES_PER_BLOCK
    par = b % 2

    def copies(row, half_buf, h):
        out = []
        for t in range(MOBA_TOPK):
            blk = top_ref[(row * N_HEADS_A + h) * MOBA_TOPK + t]
            for half in range(PAGES_PER_BLOCK):
                page = pt_ref[row, blk * PAGES_PER_BLOCK + half]
                slot = h * n_sel + t * PAGES_PER_BLOCK + half
                out.append(_head_page_copy(kpool_ref, page, h, buf_ref, sem_ref, half_buf, 0, slot))
                out.append(_head_page_copy(vpool_ref, page, h, buf_ref, sem_ref, half_buf, 1, slot))
        return out

    def start_row(row, half_buf):
        for h in range(N_HEADS_A):
            for c in copies(row, half_buf, h):
                c.start()

    @pl.when(b == 0)
    def _():
        start_row(0, 0)

    @pl.when(b + 1 < pl.num_programs(0))
    def _():
        start_row(b + 1, 1 - par)

    for h in range(N_HEADS_A):
        for c in copies(b, par, h):
            c.wait()

    q = q_ref[0]
    kn = kn_ref[0].astype(BF16).astype(F32)
    vn = vn_ref[0].astype(BF16).astype(F32)
    s_new_all = q.astype(F32) * kn
    outs = []
    for h in range(N_HEADS_A):
        lanes = slice(h * HEAD_DIM, (h + 1) * HEAD_DIM)
        qh = jnp.broadcast_to(q[:, lanes], (SUBLANES, HEAD_DIM))
        kt = jnp.concatenate([buf_ref[par, 0, h * n_sel + s] for s in range(n_sel)], axis=1).astype(BF16)
        vt = jnp.concatenate([buf_ref[par, 1, h * n_sel + s] for s in range(n_sel)], axis=1).astype(BF16)
        bias = []
        for t in range(MOBA_TOPK):
            blk = top_ref[(b * N_HEADS_A + h) * MOBA_TOPK + t]
            bias.append(jnp.where(blk == n_blk - 1, tnear_ref[h:h + 1, :], bfar_ref[h]))
        s = jnp.dot(qh, kt, preferred_element_type=F32)[:1] + jnp.concatenate(bias, axis=1)
        s_new = jnp.sum(s_new_all[:, lanes], axis=1, keepdims=True) + bias0_ref[h]
        m = jnp.maximum(jnp.max(s, axis=1, keepdims=True), s_new)
        p = jnp.exp(s - m)
        p_new = jnp.exp(s_new - m)
        l = jnp.sum(p, axis=1, keepdims=True) + p_new
        pb = jnp.broadcast_to(p.astype(BF16), (SUBLANES, p.shape[1]))
        o = lax.dot_general(pb, vt, _NT, preferred_element_type=F32)[:1]
        o = o + p_new.astype(BF16).astype(F32) * vn[:, lanes]
        outs.append(o / l)
    o_ref[0] = jnp.concatenate(outs, axis=1)


def _sample_attn_call(top, page_table, q, kn, vn, tnear, bias0, bfar, kpool_t, vpool_t, n_blk):
    n_dec = q.shape[0]
    row = pl.BlockSpec((1, 1, D_A), lambda b, *_: (b, 0, 0))
    smem = pl.BlockSpec(memory_space=pltpu.SMEM)
    any_ = pl.BlockSpec(memory_space=pl.ANY)
    n_slots = N_HEADS_A * MOBA_TOPK * PAGES_PER_BLOCK
    return pl.pallas_call(
        functools.partial(_sample_attn_kernel, n_blk=n_blk),
        grid_spec=pltpu.PrefetchScalarGridSpec(
            num_scalar_prefetch=2,
            grid=(n_dec,),
            in_specs=[row, row, row, pl.BlockSpec(tnear.shape, lambda b, *_: (0, 0)), smem, smem, any_, any_],
            out_specs=row,
            scratch_shapes=[pltpu.VMEM((2, 2, n_slots, HEAD_DIM, PAGE_SIZE), F32),
                            pltpu.SemaphoreType.DMA((2, 2, n_slots))]),
        out_shape=jax.ShapeDtypeStruct((n_dec, 1, D_A), F32),
        compiler_params=_params(1),
        name="attn_sample",
    )(top, page_table, q, kn, vn, tnear, bias0, bfar, kpool_t, vpool_t)


def _mix_sample_kernel(x1_ref, a_ref, u_ref, st_ref, pw_ref, ps_ref, wout_ref, g1_ref, b1_ref, wxq_ref,
                       x2_ref, qx_ref, *, alpha, pos):
    u = u_ref[...]

    def window_sum(g, w):
        lanes = slice(g * POOL_GROUP, (g + 1) * POOL_GROUP)
        s = u[:, lanes]
        for back in range(1, w):
            s = s + st_ref[POOL_STATE - back, :, lanes]
        return s

    p = _pool_project(window_sum, u, lambda w: float(min(w, pos + 1)), pw_ref, ps_ref)
    x2 = _out_project(x1_ref[...], a_ref[...].astype(BF16), p, wout_ref, g1_ref, b1_ref, alpha)
    x2_ref[...] = x2
    qx = jnp.dot(x2.astype(BF16), wxq_ref[...], preferred_element_type=F32) * HEAD_DIM_X ** -0.5
    qx_ref[...] = qx.astype(BF16)


def _mix_sample_call(x1, a, u, state_t, pw, ps, wout, g1, b1, wxq, *, alpha, pos):
    n = x1.shape[0]
    args = (x1, a, u, state_t, pw, ps, wout, g1, b1, wxq)
    return pl.pallas_call(
        functools.partial(_mix_sample_kernel, alpha=alpha, pos=pos),
        grid=(1,),
        in_specs=[_resident(v.shape) for v in args],
        out_specs=[_resident((n, D_MODEL)), _resident((n, D_MODEL))],
        out_shape=[jax.ShapeDtypeStruct((n, D_MODEL), F32), jax.ShapeDtypeStruct((n, D_MODEL), BF16)],
        compiler_params=_params(1),
        name="mix_sample",
    )(*args)


def _xattn_sample_kernel(qx_ref, mk_ref, mv_ref, o_ref):
    qm = _mask_heads(qx_ref[0], (SUBLANES, D_MODEL), HEAD_DIM_X)
    logits = lax.dot_general(qm, mk_ref[0].astype(BF16), _NT, preferred_element_type=F32)
    e = jnp.exp(logits - jnp.max(logits, axis=-1, keepdims=True))
    o = jnp.dot(e.astype(BF16), mv_ref[0].astype(BF16), preferred_element_type=F32)
    o = o / jnp.sum(e, axis=-1, keepdims=True)
    row = lax.broadcasted_iota(jnp.int32, o.shape, 0)
    lane_head = lax.broadcasted_iota(jnp.int32, o.shape, 1) // HEAD_DIM_X
    o_ref[0] = jnp.sum(jnp.where(lane_head == row, o, 0.0), axis=0, keepdims=True)


def _xattn_sample_call(qx, mk, mv):
    n_dec, n_mem, _ = mk.shape
    row = pl.BlockSpec((1, 1, D_MODEL), lambda b: (b, 0, 0))
    mem = pl.BlockSpec((1, n_mem, D_MODEL), lambda b: (b, 0, 0))
    return pl.pallas_call(
        _xattn_sample_kernel,
        grid=(n_dec,),
        in_specs=[row, mem, mem],
        out_specs=row,
        out_shape=jax.ShapeDtypeStruct((n_dec, 1, D_MODEL), F32),
        compiler_params=_params(1),
        name="xattn_sample",
    )(qx, mk, mv)


def _proj_ln_kernel(x_ref, o_ref, w_ref, g_ref, b_ref, y_ref, *, alpha):
    proj = jnp.dot(o_ref[...].astype(BF16), w_ref[...], preferred_element_type=F32)
    y_ref[...] = _layer_norm(alpha * x_ref[...] + proj, g_ref[...], b_ref[...])


def _proj_ln_call(x, o, w, g, b, *, alpha):
    args = (x, o, w, g, b)
    return pl.pallas_call(
        functools.partial(_proj_ln_kernel, alpha=alpha),
        grid=(1,),
        in_specs=[_resident(v.shape) for v in args],
        out_specs=_resident(x.shape),
        out_shape=jax.ShapeDtypeStruct(x.shape, F32),
        compiler_params=_params(1),
        name="proj_ln",
    )(*args)


def _t5_bucket(dist):
    n = jnp.maximum(dist, 0)
    max_exact = N_BUCKETS // 2
    nf = jnp.maximum(n, 1).astype(F32)
    large = max_exact + (jnp.log(nf / max_exact) / math.log(MAX_DISTANCE / max_exact)
                         * (N_BUCKETS - max_exact)).astype(jnp.int32)
    return jnp.where(n < max_exact, n, jnp.minimum(large, N_BUCKETS - 1))


def kernel(x_prompt, x_sample, cache_k, cache_v, cache_mem_k, cache_mem_v, state_pool, page_table, mem_prompt, rel_bias, ln_g, ln_b, w_ff1_gate, w_ff1_up, w_ff1_down, w_in, pool_w, pool_scale, w_out, w_xq, w_xk, w_xv, w_xo, w_ff2_gate, w_ff2_up, w_ff2_down):
    n_batch, seq, _ = x_prompt.shape
    n_dec, dec_seq, _ = x_sample.shape
    depth = ln_g.shape[0]
    n_pool = cache_k.shape[1]
    n_pages = page_table.shape[1]
    past_len = n_pages * PAGE_SIZE
    n_mem = mem_prompt.shape[1]
    assert depth == 1 and dec_seq == 1
    assert seq % MOBA_BLOCK == 0 and past_len % MOBA_BLOCK == 0
    assert past_len // MOBA_BLOCK >= MOBA_TOPK
    alpha = (2 * depth) ** 0.25
    tm = min(512, seq)
    n_blk = seq // MOBA_BLOCK

    bf = lambda w: w[0].astype(BF16)
    ff1 = (bf(w_ff1_gate), bf(w_ff1_up), bf(w_ff1_down))
    ff2 = (bf(w_ff2_gate), bf(w_ff2_up), bf(w_ff2_down))
    win = bf(w_in)
    ln = lambda i: (ln_g[0, i:i + 1], ln_b[0, i:i + 1])
    (g0, b0), (g1, b1), (g2, b2), (g3, b3) = ln(0), ln(1), ln(2), ln(3)
    pw, ps = pool_w[0].astype(BF16), pool_scale

    def packed(*rows):
        rows = [jnp.pad(r, ((0, 0), (0, D_MODEL - r.shape[1]))) for r in rows]
        return jnp.concatenate(rows + [jnp.zeros((SUBLANES - len(rows), D_MODEL), F32)], axis=0)

    ln0, ln3, mix_vec = packed(g0, b0), packed(g3, b3), packed(g1, b1, g2, b2, ps)
    wout, wxq, wxo = bf(w_out), bf(w_xq), bf(w_xo)

    bias_by_dist = rel_bias[_t5_bucket(jnp.arange(2 * MOBA_BLOCK, dtype=jnp.int32))].T
    bfar = rel_bias[_far_bucket()]

    xp = x_prompt.reshape(n_batch * seq, D_MODEL)
    attn_scale = HEAD_DIM ** -0.5
    x1, q, kt, vt, u, kb, vtb, km = _ffn_proj_call(xp, *ff1, ln0, win, alpha=alpha, q_scale=attn_scale * LOG2E,
                                                   tm=tm, moba_batch=n_batch)
    kpool_t = cache_k[0].transpose(0, 2, 3, 1).reshape(n_pool, D_A, PAGE_SIZE)
    vpool_t = cache_v[0].transpose(0, 2, 3, 1).reshape(n_pool, D_A, PAGE_SIZE)
    a, kmean_t = _moba_call(rel_bias.reshape(-1) * LOG2E, page_table, q, kb.reshape(n_batch, seq, D_A), vtb,
                            km.reshape(n_batch, n_blk, D_A), kpool_t)
    mk, mv, mkbt, mvb = _memkv_call(mem_prompt.reshape(n_batch * n_mem, D_MODEL), bf(w_xk), bf(w_xv), n_mem=n_mem)
    x3 = _mix_call(x1, a.reshape(n_batch * seq, D_A), u, pw, mix_vec, wout, wxq, wxo,
                   mkbt, mvb.reshape(n_batch, n_mem, D_MODEL), alpha=alpha, n_batch=n_batch, tm=tm)
    y_prompt = _ffn_call(x3, *ff2, ln3, alpha=alpha, tm=tm).reshape(n_batch, seq, D_MODEL)

    to_heads = lambda t: t.reshape(t.shape[0], N_HEADS_A, HEAD_DIM, t.shape[2]).transpose(0, 3, 1, 2)[None]
    k_prompt, v_prompt = to_heads(kt), to_heads(vt)
    pool_prompt = u.reshape(n_batch, seq, D_POOL)[None, :, seq - POOL_STATE:]
    memk_prompt = mk.reshape(1, n_batch, n_mem, N_HEADS_X, HEAD_DIM_X)
    memv_prompt = mv.reshape(1, n_batch, n_mem, N_HEADS_X, HEAD_DIM_X)

    xs = x_sample.reshape(n_dec, D_MODEL)
    x1s, qs, kn, vn, us = _ffn_proj_call(xs, *ff1, ln0, win, alpha=alpha, q_scale=attn_scale, tm=n_dec)
    n_blk_s = past_len // MOBA_BLOCK
    qs3 = qs.reshape(n_dec, 1, D_A)
    top = _gate_call(qs3, kmean_t, n_blk_s)[:, :, :MOBA_TOPK].reshape(-1)
    tnear = bias_by_dist[:, MOBA_BLOCK - jnp.arange(MOBA_BLOCK)]
    a_s = _sample_attn_call(top, page_table, qs3, kn.reshape(n_dec, 1, D_A), vn.reshape(n_dec, 1, D_A),
                            tnear, bias_by_dist[:, 0], bfar, kpool_t, vpool_t, n_blk_s)
    state_t = state_pool[0].transpose(1, 0, 2)
    x2s, qxs = _mix_sample_call(x1s, a_s.reshape(n_dec, D_A), us, state_t, pw, ps, wout, g1, b1, wxq,
                                alpha=alpha, pos=past_len)
    o_s = _xattn_sample_call(qxs.reshape(n_dec, 1, D_MODEL),
                             cache_mem_k[0].reshape(n_dec, n_mem, D_MODEL),
                             cache_mem_v[0].reshape(n_dec, n_mem, D_MODEL))
    x3s = _proj_ln_call(x2s, o_s.reshape(n_dec, D_MODEL), wxo, g2, b2, alpha=alpha)
    y_sample = _ffn_call(x3s, *ff2, ln3, alpha=alpha, tm=n_dec).reshape(n_dec, 1, D_MODEL)

    k_sample = kn.reshape(1, n_dec, 1, N_HEADS_A, HEAD_DIM)
    v_sample = vn.reshape(1, n_dec, 1, N_HEADS_A, HEAD_DIM)
    pool_sample = jnp.concatenate([state_t[1:], us[None]], axis=0).transpose(1, 0, 2)[None]
    return (y_prompt, y_sample, k_prompt, v_prompt, pool_prompt, memk_prompt, memv_prompt,
            k_sample, v_sample, pool_sample)
```
